```python
import math
import jax
import jax.numpy as jnp
from jax import lax
import numpy as np


D_MODEL = 1024
BATCH = 32
SEQ = 2048
DEPTH = 2

N_A_LAYERS = DEPTH // 2
N_B_LAYERS = DEPTH - N_A_LAYERS
NORM_EPS = 1e-6

MEM_LEN = 256
MEM_HEADS = 4
MEM_HEAD_DIM = D_MODEL // 8
MEM_WIDTH = MEM_HEADS * MEM_HEAD_DIM

SSD_HEAD_DIM = 64
SSD_INNER = 3 * D_MODEL // 2
SSD_HEADS = SSD_INNER // SSD_HEAD_DIM
SSD_GROUPS = 4
SSD_STATE = 128
CONV_WIDTH = 4
SSD_CHUNK = 128
DT_MIN = 1e-3
DT_MAX = 1e-1
CONV_CH = SSD_INNER + 2 * SSD_GROUPS * SSD_STATE
A_IN_WIDTH = SSD_INNER + CONV_CH + SSD_HEADS + MEM_WIDTH
A_MIX_WIDTH = SSD_INNER + MEM_WIDTH

HEAD_DIM = 64
Q_HEADS = D_MODEL // HEAD_DIM
KV_HEADS = 4
WINDOW = 128
ATTN_BLOCK = 128
ROT_DIM = HEAD_DIM // 4
ROPE_THETA = 500000.0
Q_WIDTH = Q_HEADS * HEAD_DIM
KV_WIDTH = KV_HEADS * HEAD_DIM
B_MIX_WIDTH = Q_WIDTH + MEM_WIDTH

N_EXPERT_GROUPS = 4
EXPERTS_PER_GROUP = 8
N_EXPERTS = N_EXPERT_GROUPS * EXPERTS_PER_GROUP
TOP_K_IN_GROUP = 2
D_EXPERT = D_MODEL // 2

kernel_name = 'yoco_ssd_swa_sink_memory_hmoe'


def rms_norm(x, w):
    xf = x.astype(jnp.float32)
    y = xf * lax.rsqrt(jnp.mean(xf * xf, axis=-1, keepdims=True) + NORM_EPS)
    return (y * w.astype(jnp.float32)).astype(x.dtype)


def partial_rope(x, positions):
    half = ROT_DIM // 2
    inv_freq = ROPE_THETA ** (-2.0 * jnp.arange(half, dtype=jnp.float32) / ROT_DIM)
    ang = positions.astype(jnp.float32)[:, None] * inv_freq[None, :]
    cos = jnp.cos(ang)[None, :, None, :]
    sin = jnp.sin(ang)[None, :, None, :]
    xf = x.astype(jnp.float32)
    x1, x2, rest = xf[..., :half], xf[..., half:ROT_DIM], xf[..., ROT_DIM:]
    out = jnp.concatenate([x1 * cos - x2 * sin, x2 * cos + x1 * sin, rest], axis=-1)
    return out.astype(x.dtype)


def memory_cross_attention(q, mem, norm_w, w_kv):
    bsz, s, _ = q.shape
    m = mem.shape[1]
    kv = (rms_norm(mem, norm_w) @ w_kv).reshape(bsz, m, 2, MEM_HEADS, MEM_HEAD_DIM)
    k, v = kv[:, :, 0], kv[:, :, 1]
    qh = q.reshape(bsz, s, MEM_HEADS, MEM_HEAD_DIM)
    scores = jnp.einsum('bshd,bmhd->bhsm', qh, k, preferred_element_type=jnp.float32) * (MEM_HEAD_DIM ** -0.5)
    p = jax.nn.softmax(scores, axis=-1).astype(v.dtype)
    return jnp.einsum('bhsm,bmhd->bshd', p, v).reshape(bsz, s, MEM_WIDTH)


def causal_depthwise_conv(x, w, b):
    out = lax.conv_general_dilated(
        x, w[:, None, :].astype(x.dtype), window_strides=(1,),
        padding=((CONV_WIDTH - 1, 0),), dimension_numbers=('NWC', 'WIO', 'NWC'),
        feature_group_count=x.shape[-1])
    return out + b.astype(x.dtype)


def ssd_chunked_scan(x, dt, a, bmat, cmat):
    bsz, s = x.shape[0], x.shape[1]
    nc = s // SSD_CHUNK
    r = SSD_HEADS // SSD_GROUPS

    def to_chunks(t):
        return jnp.moveaxis(t.reshape(bsz, nc, SSD_CHUNK, *t.shape[2:]), 1, 0)

    xdt = (x * dt[..., None]).reshape(bsz, s, SSD_GROUPS, r, SSD_HEAD_DIM)
    adt = (dt * a).reshape(bsz, s, SSD_GROUPS, r)
    causal = jnp.tril(jnp.ones((SSD_CHUNK, SSD_CHUNK), dtype=bool))[None, :, :, None, None]

    def step(state, inp):
        xc, ac, bc, cc = inp
        acum = jnp.cumsum(ac, axis=1)
        seg = acum[:, :, None] - acum[:, None, :]
        decay = jnp.exp(jnp.where(causal, seg, -jnp.inf))
        cb = jnp.einsum('blgn,bsgn->blsg', cc, bc)
        y_diag = jnp.einsum('blsgr,bsgrp->blgrp', cb[..., None] * decay, xc)
        y_off = jnp.einsum('blgn,bgrpn->blgrp', cc, state) * jnp.exp(acum)[..., None]
        a_last = acum[:, -1]
        to_end = jnp.exp(a_last[:, None] - acum)
        new_state = state * jnp.exp(a_last)[..., None, None] + jnp.einsum(
            'bsgn,bsgrp->bgrpn', bc, xc * to_end[..., None])
        return new_state, y_diag + y_off

    state0 = jnp.zeros((bsz, SSD_GROUPS, r, SSD_HEAD_DIM, SSD_STATE), jnp.float32)
    _, y = lax.scan(step, state0, (to_chunks(xdt), to_chunks(adt), to_chunks(bmat), to_chunks(cmat)))
    return jnp.moveaxis(y, 0, 1).reshape(bsz, s, SSD_HEADS, SSD_HEAD_DIM)


def ssd_memory_mixer(h, mem, norm_w, in_w, conv_w, conv_b, dt_bias, a_log, d_skip, gnorm_w, out_w, mem_norm_w, mem_w_kv):
    bsz, s, _ = h.shape
    f32 = jnp.float32
    proj = rms_norm(h, norm_w) @ in_w
    z, xbc, dt, q_mem = jnp.split(proj, [SSD_INNER, SSD_INNER + CONV_CH, SSD_INNER + CONV_CH + SSD_HEADS], axis=-1)
    xbc = jax.nn.silu(causal_depthwise_conv(xbc, conv_w, conv_b))
    xs, bm, cm = jnp.split(xbc, [SSD_INNER, SSD_INNER + SSD_GROUPS * SSD_STATE], axis=-1)
    xs = xs.astype(f32).reshape(bsz, s, SSD_HEADS, SSD_HEAD_DIM)
    dt = jax.nn.softplus(dt.astype(f32) + dt_bias.astype(f32))
    a = -jnp.exp(a_log.astype(f32))
    y = ssd_chunked_scan(xs, dt, a,
                         bm.astype(f32).reshape(bsz, s, SSD_GROUPS, SSD_STATE),
                         cm.astype(f32).reshape(bsz, s, SSD_GROUPS, SSD_STATE))
    y = y + d_skip.astype(f32)[:, None] * xs
    y = y.reshape(bsz, s, SSD_INNER) * jax.nn.silu(z.astype(f32))
    yg = y.reshape(bsz, s, SSD_GROUPS, SSD_INNER // SSD_GROUPS)
    yg = yg * lax.rsqrt(jnp.mean(yg * yg, axis=-1, keepdims=True) + NORM_EPS)
    y = (yg.reshape(bsz, s, SSD_INNER) * gnorm_w.astype(f32)).astype(h.dtype)
    m = memory_cross_attention(q_mem, mem, mem_norm_w, mem_w_kv)
    return jnp.concatenate([y, m], axis=-1) @ out_w


def shared_kv(h, norm_w, w_kv, positions):
    bsz, s, _ = h.shape
    kv = (rms_norm(h, norm_w) @ w_kv).reshape(bsz, s, 2, KV_HEADS, HEAD_DIM)
    return partial_rope(kv[:, :, 0], positions), kv[:, :, 1]


def sliding_window_sink_attention(q, k, v, sinks):
    bsz, s = q.shape[0], q.shape[1]
    nblk = s // ATTN_BLOCK
    g = Q_HEADS // KV_HEADS
    f32 = jnp.float32
    qb = jnp.moveaxis(q.reshape(bsz, nblk, ATTN_BLOCK, KV_HEADS, g, HEAD_DIM), 1, 0)
    pad = ((0, 0), (ATTN_BLOCK, 0), (0, 0), (0, 0))
    kp, vp = jnp.pad(k, pad), jnp.pad(v, pad)
    t_idx = jnp.arange(ATTN_BLOCK)[:, None]
    s_idx = jnp.arange(2 * ATTN_BLOCK)[None, :]
    band = (s_idx > t_idx + ATTN_BLOCK - WINDOW) & (s_idx <= t_idx + ATTN_BLOCK)
    sink = sinks.astype(f32).reshape(KV_HEADS, g)[None, :, :, None]
    scale = HEAD_DIM ** -0.5

    def one_block(args):
        i, qi = args
        start = i * ATTN_BLOCK
        ki = lax.dynamic_slice_in_dim(kp, start, 2 * ATTN_BLOCK, axis=1)
        vi = lax.dynamic_slice_in_dim(vp, start, 2 * ATTN_BLOCK, axis=1)
        valid = band & (start - ATTN_BLOCK + s_idx >= 0)
        sc = jnp.einsum('bqhgd,bkhd->bhgqk', qi, ki, preferred_element_type=f32) * scale
        sc = jnp.where(valid, sc, -jnp.inf)
        mx = jnp.maximum(jnp.max(sc, axis=-1), sink)
        p = jnp.exp(sc - mx[..., None])
        denom = jnp.sum(p, axis=-1) + jnp.exp(sink - mx)
        return jnp.einsum('bhgqk,bkhd->bqhgd', (p / denom[..., None]).astype(vi.dtype), vi)

    out = lax.map(one_block, (jnp.arange(nblk), qb))
    return jnp.moveaxis(out, 0, 1).reshape(bsz, s, Q_WIDTH)


def swa_memory_mixer(h, k_sh, v_sh, mem, positions, norm_w, in_w, sinks, out_w, mem_norm_w, mem_w_kv):
    bsz, s, _ = h.shape
    proj = rms_norm(h, norm_w) @ in_w
    q, q_mem = jnp.split(proj, [Q_WIDTH], axis=-1)
    q = partial_rope(q.reshape(bsz, s, Q_HEADS, HEAD_DIM), positions)
    att = sliding_window_sink_attention(q, k_sh, v_sh, sinks)
    m = memory_cross_attention(q_mem, mem, mem_norm_w, mem_w_kv)
    return jnp.concatenate([att, m], axis=-1) @ out_w


def hierarchical_moe(h, norm_w, rg_w, rg_b, re_w, re_b, w_gate, w_up, w_down):
    bsz, s, d = h.shape
    f32 = jnp.float32
    xt = rms_norm(h, norm_w).reshape(-1, d)
    group_p = jax.nn.softmax((xt @ rg_w).astype(f32) + rg_b.astype(f32), axis=-1)
    g_w, g_idx = lax.top_k(group_p, 1)
    exp_logits = ((xt @ re_w).astype(f32) + re_b.astype(f32)).reshape(-1, N_EXPERT_GROUPS, EXPERTS_PER_GROUP)
    in_group = jnp.einsum('tg,tge->te', jax.nn.one_hot(g_idx[:, 0], N_EXPERT_GROUPS, dtype=f32), exp_logits)
    e_val, e_idx = lax.top_k(in_group, TOP_K_IN_GROUP)
    e_w = jax.nn.softmax(e_val, axis=-1) * g_w
    expert_id = g_idx * EXPERTS_PER_GROUP + e_idx
    gates = jnp.sum(jax.nn.one_hot(expert_id, N_EXPERTS, dtype=f32) * e_w[..., None], axis=1).astype(xt.dtype)
    out = jnp.zeros_like(xt)
    for e in range(N_EXPERTS):
        he = jax.nn.silu(xt @ w_gate[e]) * (xt @ w_up[e])
        out = out + gates[:, e:e + 1] * (he @ w_down[e])
    return out.reshape(bsz, s, d)


def setup_inputs(seed: int = 0) -> dict:
    key = jax.random.key(seed)
    ks = iter(jax.random.split(key, 40))
    f32 = jnp.float32
    d, na, nb = D_MODEL, N_A_LAYERS, N_B_LAYERS

    def nrm(shape, scale):
        return jax.random.normal(next(ks), shape, f32) * scale

    def gain(shape):
        return 1.0 + nrm(shape, 0.02)

    x = nrm((BATCH, SEQ, d), 1.0)
    mem = nrm((BATCH, MEM_LEN, d), 1.0)
    a_norm_w = gain((na, d))
    a_in_w = nrm((na, d, A_IN_WIDTH), d ** -0.5)
    a_conv_w = nrm((na, CONV_WIDTH, CONV_CH), CONV_WIDTH ** -0.5)
    a_conv_b = nrm((na, CONV_CH), 0.01)
    dt0 = jnp.exp(jax.random.uniform(next(ks), (na, SSD_HEADS), f32, math.log(DT_MIN), math.log(DT_MAX)))
    a_dt_bias = dt0 + jnp.log(-jnp.expm1(-dt0))
    a_a_log = jnp.log(jax.random.uniform(next(ks), (na, SSD_HEADS), f32, 1.0, 16.0))
    a_d_skip = gain((na, SSD_HEADS))
    a_gnorm_w = gain((na, SSD_INNER))
    a_out_w = nrm((na, A_MIX_WIDTH, d), A_MIX_WIDTH ** -0.5)
    kv_norm_w = gain((d,))
    w_kv = nrm((d, 2 * KV_WIDTH), d ** -0.5)
    b_norm_w = gain((nb, d))
    b_in_w = nrm((nb, d, B_MIX_WIDTH), d ** -0.5)
    b_sinks = nrm((nb, Q_HEADS), 0.5)
    b_out_w = nrm((nb, B_MIX_WIDTH, d), B_MIX_WIDTH ** -0.5)
    mem_norm_w = gain((DEPTH, d))
    mem_w_kv = nrm((DEPTH, d, 2 * MEM_WIDTH), d ** -0.5)
    ffn_norm_w = gain((DEPTH, d))
    router_group_w = nrm((DEPTH, d, N_EXPERT_GROUPS), d ** -0.5)
    router_group_b = nrm((DEPTH, N_EXPERT_GROUPS), 0.01)
    router_expert_w = nrm((DEPTH, d, N_EXPERTS), d ** -0.5)
    router_expert_b = nrm((DEPTH, N_EXPERTS), 0.01)
    w_gate = nrm((DEPTH, N_EXPERTS, d, D_EXPERT), d ** -0.5)
    w_up = nrm((DEPTH, N_EXPERTS, d, D_EXPERT), d ** -0.5)
    w_down = nrm((DEPTH, N_EXPERTS, D_EXPERT, d), D_EXPERT ** -0.5)
    final_norm_w = gain((d,))
    return {'x': x, 'mem': mem, 'a_norm_w': a_norm_w, 'a_in_w': a_in_w, 'a_conv_w': a_conv_w,
            'a_conv_b': a_conv_b, 'a_dt_bias': a_dt_bias, 'a_a_log': a_a_log, 'a_d_skip': a_d_skip,
            'a_gnorm_w': a_gnorm_w, 'a_out_w': a_out_w, 'kv_norm_w': kv_norm_w, 'w_kv': w_kv,
            'b_norm_w': b_norm_w, 'b_in_w': b_in_w, 'b_sinks': b_sinks, 'b_out_w': b_out_w,
            'mem_norm_w': mem_norm_w, 'mem_w_kv': mem_w_kv, 'ffn_norm_w': ffn_norm_w,
            'router_group_w': router_group_w, 'router_group_b': router_group_b,
            'router_expert_w': router_expert_w, 'router_expert_b': router_expert_b,
            'w_gate': w_gate, 'w_up': w_up, 'w_down': w_down, 'final_norm_w': final_norm_w}


def reference(x, mem, a_norm_w, a_in_w, a_conv_w, a_conv_b, a_dt_bias, a_a_log, a_d_skip,
              a_gnorm_w, a_out_w, kv_norm_w, w_kv, b_norm_w, b_in_w, b_sinks, b_out_w,
              mem_norm_w, mem_w_kv, ffn_norm_w, router_group_w, router_group_b,
              router_expert_w, router_expert_b, w_gate, w_up, w_down, final_norm_w):
    positions = jnp.arange(x.shape[1], dtype=jnp.int32)
    h = x
    k_sh = v_sh = None
    for layer in range(DEPTH):
        if layer < N_A_LAYERS:
            i = layer
            h = h + ssd_memory_mixer(h, mem, a_norm_w[i], a_in_w[i], a_conv_w[i], a_conv_b[i],
                                     a_dt_bias[i], a_a_log[i], a_d_skip[i], a_gnorm_w[i], a_out_w[i],
                                     mem_norm_w[layer], mem_w_kv[layer])
        else:
            j = layer - N_A_LAYERS
            if j == 0:
                k_sh, v_sh = shared_kv(h, kv_norm_w, w_kv, positions)
            h = h + swa_memory_mixer(h, k_sh, v_sh, mem, positions, b_norm_w[j], b_in_w[j],
                                     b_sinks[j], b_out_w[j], mem_norm_w[layer], mem_w_kv[layer])
        h = h + hierarchical_moe(h, ffn_norm_w[layer], router_group_w[layer], router_group_b[layer],
                                 router_expert_w[layer], router_expert_b[layer],
                                 w_gate[layer], w_up[layer], w_down[layer])
    return rms_norm(h, final_norm_w)
```

```python
import functools

import jax
import jax.numpy as jnp
from jax import lax
from jax.experimental import pallas as pl
from jax.experimental.pallas import tpu as pltpu

F32 = jnp.float32
BF16 = jnp.bfloat16
NORM_EPS = 1e-6

LANES = 128
SUBLANES = 8
VMEM_LIMIT = 56 * 1024 * 1024

MEM_HEADS = 4
MEM_HEAD_DIM = 128
MEM_WIDTH = MEM_HEADS * MEM_HEAD_DIM
SSD_HEAD_DIM = 64
SSD_HEADS = 24
SSD_INNER = SSD_HEADS * SSD_HEAD_DIM
SSD_GROUPS = 4
SSD_STATE = 128
SSD_CHUNK = 128
CONV_WIDTH = 4
HEAD_DIM = 64
Q_HEADS = 16
KV_HEADS = 4
ATTN_BLOCK = 128
ROT_DIM = 16
ROPE_THETA = 500000.0
N_GROUPS = 4
EXPERTS_PER_GROUP = 8
N_EXPERTS = 32
D_EXPERT = 512

ROW_TILE = 512
MOE_BLOCK = 2048
MOE_ROWS = 128
MOE_STRIDE = MOE_ROWS + SUBLANES


def _params():
    return pltpu.CompilerParams(vmem_limit_bytes=VMEM_LIMIT)


def _resident(shape):
    nd = len(shape)
    return pl.BlockSpec(shape, lambda *_: (0,) * nd, pipeline_mode=pl.Buffered(1))


def _rms(x, w):
    ms = jnp.mean(x * x, axis=-1, keepdims=True)
    return x * lax.rsqrt(ms + NORM_EPS) * w


def _sigmoid(x):
    return 1.0 / (1.0 + jnp.exp(-x))


def _norm_matmul_kernel(x_ref, nw_ref, *refs, n_w):
    w_refs, o_refs = refs[:n_w], refs[n_w:]
    xn = _rms(x_ref[...], nw_ref[...]).astype(BF16)
    for w_ref, o_ref in zip(w_refs, o_refs):
        n = w_ref.shape[1]
        for c0 in range(0, n, 512):
            cw = min(512, n - c0)
            acc = jnp.dot(xn, w_ref[:, c0:c0 + cw], preferred_element_type=F32)
            o_ref[:, c0:c0 + cw] = acc.astype(o_ref.dtype)


def _norm_matmul(x, norm_w, weights, out_dtypes, name):
    t, d = x.shape
    tm = min(ROW_TILE, t)
    in_specs = [pl.BlockSpec((tm, d), lambda i: (i, 0)), _resident((1, d))]
    in_specs += [_resident(w.shape) for w in weights]
    out_specs = [pl.BlockSpec((tm, w.shape[1]), lambda i: (i, 0)) for w in weights]
    out_shape = [jax.ShapeDtypeStruct((t, w.shape[1]), dt) for w, dt in zip(weights, out_dtypes)]
    return pl.pallas_call(
        functools.partial(_norm_matmul_kernel, n_w=len(weights)),
        grid=(t // tm,), in_specs=in_specs, out_specs=out_specs, out_shape=out_shape,
        compiler_params=_params(), name=name,
    )(x, norm_w.reshape(1, d), *weights)


def _mix_out_kernel(a1_ref, a2_ref, w1_ref, w2_ref, r_ref, o_ref):
    acc = jnp.dot(a1_ref[...], w1_ref[...], preferred_element_type=F32)
    acc = acc + jnp.dot(a2_ref[...], w2_ref[...], preferred_element_type=F32)
    o_ref[...] = r_ref[...] + acc


def _mix_out(a1, a2, w1, w2, res, name):
    t, d = res.shape
    tm = min(ROW_TILE, t)
    return pl.pallas_call(
        _mix_out_kernel, grid=(t // tm,),
        in_specs=[pl.BlockSpec((tm, a1.shape[1]), lambda i: (i, 0)),
                  pl.BlockSpec((tm, a2.shape[1]), lambda i: (i, 0)),
                  _resident(w1.shape), _resident(w2.shape),
                  pl.BlockSpec((tm, d), lambda i: (i, 0))],
        out_specs=pl.BlockSpec((tm, d), lambda i: (i, 0)),
        out_shape=jax.ShapeDtypeStruct((t, d), F32),
        compiler_params=_params(), name=name,
    )(a1, a2, w1, w2, res)


def _mem_attn_kernel(q_ref, kv_ref, o_ref):
    scale = MEM_HEAD_DIM ** -0.5
    outs = []
    for h in range(MEM_HEADS):
        lo = h * MEM_HEAD_DIM
        q = q_ref[:, lo:lo + MEM_HEAD_DIM]
        k = kv_ref[:, lo:lo + MEM_HEAD_DIM]
        v = kv_ref[:, MEM_WIDTH + lo:MEM_WIDTH + lo + MEM_HEAD_DIM]
        s = lax.dot_general(q, k, (((1,), (1,)), ((), ())), preferred_element_type=F32) * scale
        p = jnp.exp(s - jnp.max(s, axis=-1, keepdims=True))
        p = p / jnp.sum(p, axis=-1, keepdims=True)
        outs.append(jnp.dot(p.astype(BF16), v, preferred_element_type=F32))
    o_ref[...] = jnp.concatenate(outs, axis=-1).astype(o_ref.dtype)


def _mem_attn(q, kv, bsz, seq, mem_len, name):
    tq = min(ROW_TILE, seq)
    nq = seq // tq
    return pl.pallas_call(
        _mem_attn_kernel, grid=(bsz, nq),
        in_specs=[pl.BlockSpec((tq, MEM_WIDTH), lambda b, i: (b * nq + i, 0)),
                  pl.BlockSpec((mem_len, 2 * MEM_WIDTH), lambda b, i: (b, 0))],
        out_specs=pl.BlockSpec((tq, MEM_WIDTH), lambda b, i: (b * nq + i, 0)),
        out_shape=jax.ShapeDtypeStruct((bsz * seq, MEM_WIDTH), BF16),
        compiler_params=_params(), name=name,
    )(q, kv)


def _ssd_kernel(xbc_ref, z_ref, dt_ref, cw_ref, cb_ref, dtb_ref, alog_ref, dsk_ref, gnw_ref,
                expand_ref, o_ref, ext_ref, st_ref):
    c = pl.program_id(1)
    L = SSD_CHUNK
    gw = SSD_INNER // SSD_GROUPS
    hpg = SSD_HEADS // SSD_GROUPS

    @pl.when(c == 0)
    def _():
        ext_ref[0:SUBLANES, :] = jnp.zeros((SUBLANES, ext_ref.shape[1]), F32)
        st_ref[...] = jnp.zeros_like(st_ref)

    @pl.when(c > 0)
    def _():
        ext_ref[0:SUBLANES, :] = ext_ref[L:L + SUBLANES, :]

    ext_ref[SUBLANES:SUBLANES + L, :] = xbc_ref[...].astype(F32)
    conv = cb_ref[...]
    for k in range(CONV_WIDTH):
        conv = conv + cw_ref[k:k + 1, :] * ext_ref[pl.ds(SUBLANES - (CONV_WIDTH - 1) + k, L), :]
    xbc = conv * _sigmoid(conv)
    xs = xbc[:, :SSD_INNER]
    bm = xbc[:, SSD_INNER:SSD_INNER + SSD_GROUPS * SSD_STATE].astype(BF16)
    cm = xbc[:, SSD_INNER + SSD_GROUPS * SSD_STATE:].astype(BF16)

    pre = dt_ref[...] + dtb_ref[...]
    dt = jnp.maximum(pre, 0.0) + jnp.log(1.0 + jnp.exp(-jnp.abs(pre)))
    a = -jnp.exp(alog_ref[...])
    adt = dt * a
    row = lax.broadcasted_iota(jnp.int32, (L, L), 0)
    col = lax.broadcasted_iota(jnp.int32, (L, L), 1)
    causal = row >= col
    acum = jnp.dot(causal.astype(F32), adt, preferred_element_type=F32,
                   precision=lax.Precision.HIGHEST)
    acum_t = acum.T
    ea = jnp.exp(acum)
    te = jnp.exp(acum[L - 1:L, :] - acum)
    stacked = jnp.concatenate([dt, ea, te], axis=0).astype(BF16)
    wide = jnp.dot(stacked, expand_ref[...], preferred_element_type=F32)
    dtx, eax, tex = wide[0:L], wide[L:2 * L], wide[2 * L:3 * L]
    xdt = xs * dtx
    xdt_b = xdt.astype(BF16)
    xw = (xdt * tex).astype(BF16)

    ys = []
    for g in range(SSD_GROUPS):
        cg = cm[:, g * SSD_STATE:(g + 1) * SSD_STATE]
        bg = bm[:, g * SSD_STATE:(g + 1) * SSD_STATE]
        cb = lax.dot_general(cg, bg, (((1,), (1,)), ((), ())), preferred_element_type=F32)
        state = st_ref[g]
        y_off = jnp.dot(cg, state.astype(BF16), preferred_element_type=F32) * eax[:, g * gw:(g + 1) * gw]
        parts = []
        for r in range(hpg):
            h = g * hpg + r
            seg = acum[:, h:h + 1] - acum_t[h:h + 1, :]
            m = (cb * jnp.exp(jnp.where(causal, seg, -jnp.inf))).astype(BF16)
            parts.append(jnp.dot(m, xdt_b[:, h * SSD_HEAD_DIM:(h + 1) * SSD_HEAD_DIM],
                                 preferred_element_type=F32))
        ys.append(jnp.concatenate(parts, axis=-1) + y_off)
        upd = lax.dot_general(bg, xw[:, g * gw:(g + 1) * gw], (((0,), (0,)), ((), ())),
                              preferred_element_type=F32)
        st_ref[g] = state * eax[L - 1:L, g * gw:(g + 1) * gw] + upd
    y = jnp.concatenate(ys, axis=-1) + dsk_ref[...] * xs
    zf = z_ref[...].astype(F32)
    y = y * (zf * _sigmoid(zf))
    normed = []
    for g in range(SSD_GROUPS):
        yg = y[:, g * gw:(g + 1) * gw]
        normed.append(yg * lax.rsqrt(jnp.mean(yg * yg, axis=-1, keepdims=True) + NORM_EPS))
    o_ref[...] = (jnp.concatenate(normed, axis=-1) * gnw_ref[...]).astype(o_ref.dtype)


def _ssd(xbc, z, dt, conv_w, conv_b, dt_bias, a_log, d_skip, gnorm_w, bsz, seq):
    nc = seq // SSD_CHUNK
    conv_ch = xbc.shape[1]
    pad = LANES - SSD_HEADS
    dtb = jnp.pad(dt_bias, (0, pad)).reshape(1, LANES)
    alog = jnp.pad(a_log, (0, pad)).reshape(1, LANES)
    dsk = jnp.repeat(d_skip, SSD_HEAD_DIM).reshape(1, SSD_INNER)
    expand = (jnp.arange(LANES)[:, None] == (jnp.arange(SSD_INNER) // SSD_HEAD_DIM)[None, :]).astype(BF16)
    blk = lambda w: pl.BlockSpec((SSD_CHUNK, w), lambda b, c: (b * nc + c, 0))
    return pl.pallas_call(
        _ssd_kernel, grid=(bsz, nc),
        in_specs=[blk(conv_ch), blk(SSD_INNER), blk(LANES),
                  _resident((CONV_WIDTH, conv_ch)), _resident((1, conv_ch)),
                  _resident((1, LANES)), _resident((1, LANES)),
                  _resident((1, SSD_INNER)), _resident((1, SSD_INNER)),
                  _resident((LANES, SSD_INNER))],
        out_specs=blk(SSD_INNER),
        out_shape=jax.ShapeDtypeStruct((bsz * seq, SSD_INNER), BF16),
        scratch_shapes=[pltpu.VMEM((SSD_CHUNK + 2 * SUBLANES, conv_ch), F32),
                        pltpu.VMEM((SSD_GROUPS, SSD_STATE, SSD_INNER // SSD_GROUPS), F32)],
        compiler_params=_params(), name="ssd_mixer",
    )(xbc, z, dt, conv_w, conv_b.reshape(1, conv_ch), dtb, alog, dsk,
      gnorm_w.reshape(1, SSD_INNER), expand)


def _rope(x, cos, sin_up, sin_dn):
    w = x.shape[1]
    reps = w // LANES
    tile = lambda t: jnp.concatenate([t] * reps, axis=-1) if reps > 1 else t
    half = ROT_DIM // 2
    return (x * tile(cos) + pltpu.roll(x, half, axis=1) * tile(sin_up)
            + pltpu.roll(x, w - half, axis=1) * tile(sin_dn))


def _swa_kernel(sink_ref, q_ref, kvp_ref, kvc_ref, tc_ref, tp_ref, o_ref):
    i = pl.program_id(1)
    blk = ATTN_BLOCK
    kvw = KV_HEADS * HEAD_DIM
    g = Q_HEADS // KV_HEADS
    scale = HEAD_DIM ** -0.5
    q = _rope(q_ref[...].astype(F32), tc_ref[0], tc_ref[1], tc_ref[2]).astype(BF16)
    k_prev = _rope(kvp_ref[:, :kvw].astype(F32), tp_ref[0], tp_ref[1], tp_ref[2])
    k_cur = _rope(kvc_ref[:, :kvw].astype(F32), tc_ref[0], tc_ref[1], tc_ref[2])
    k = jnp.concatenate([k_prev, k_cur], axis=0).astype(BF16)
    v = jnp.concatenate([kvp_ref[:, kvw:], kvc_ref[:, kvw:]], axis=0)
    t_idx = lax.broadcasted_iota(jnp.int32, (blk, 2 * blk), 0)
    s_idx = lax.broadcasted_iota(jnp.int32, (blk, 2 * blk), 1)
    valid = (s_idx > t_idx) & (s_idx <= t_idx + blk) & ((i > 0) | (s_idx >= blk))
    outs = []
    for h in range(Q_HEADS):
        kh = h // g
        qh = q[:, h * HEAD_DIM:(h + 1) * HEAD_DIM]
        kk = k[:, kh * HEAD_DIM:(kh + 1) * HEAD_DIM]
        vv = v[:, kh * HEAD_DIM:(kh + 1) * HEAD_DIM]
        sc = lax.dot_general(qh, kk, (((1,), (1,)), ((), ())), preferred_element_type=F32) * scale
        sc = jnp.where(valid, sc, -jnp.inf)
        sink = sink_ref[h]
        mx = jnp.maximum(jnp.max(sc, axis=-1, keepdims=True), sink)
        p = jnp.exp(sc - mx)
        denom = jnp.sum(p, axis=-1, keepdims=True) + jnp.exp(sink - mx)
        outs.append(jnp.dot((p / denom).astype(BF16), vv, preferred_element_type=F32))
    o_ref[...] = jnp.concatenate(outs, axis=-1).astype(o_ref.dtype)


def _rope_tables(seq):
    half = ROT_DIM // 2
    inv_freq = ROPE_THETA ** (-2.0 * jnp.arange(half, dtype=F32) / ROT_DIM)
    ang = jnp.arange(seq, dtype=jnp.int32).astype(F32)[:, None] * inv_freq[None, :]
    cos, sin = jnp.cos(ang), jnp.sin(ang)
    ones = jnp.ones((seq, HEAD_DIM - ROT_DIM), F32)
    zeros = jnp.zeros((seq, HEAD_DIM - ROT_DIM), F32)
    zh = jnp.zeros((seq, half), F32)
    cos_h = jnp.concatenate([cos, cos, ones], axis=-1)
    up_h = jnp.concatenate([zh, sin, zeros], axis=-1)
    dn_h = jnp.concatenate([-sin, zh, zeros], axis=-1)
    two = lambda t: jnp.concatenate([t, t], axis=-1)
    return jnp.stack([two(cos_h), two(up_h), two(dn_h)], axis=0)


def _swa(q, kv, sinks, bsz, seq):
    nb = seq // ATTN_BLOCK
    tables = _rope_tables(seq)
    qw = Q_HEADS * HEAD_DIM
    kvw2 = 2 * KV_HEADS * HEAD_DIM
    prev = lambda i: jnp.maximum(i - 1, 0)
    return pl.pallas_call(
        _swa_kernel, grid=(bsz, nb),
        in_specs=[pl.BlockSpec(memory_space=pltpu.SMEM),
                  pl.BlockSpec((ATTN_BLOCK, qw), lambda b, i: (b * nb + i, 0)),
                  pl.BlockSpec((ATTN_BLOCK, kvw2), lambda b, i: (b * nb + prev(i), 0)),
                  pl.BlockSpec((ATTN_BLOCK, kvw2), lambda b, i: (b * nb + i, 0)),
                  pl.BlockSpec((3, ATTN_BLOCK, LANES), lambda b, i: (0, i, 0)),
                  pl.BlockSpec((3, ATTN_BLOCK, LANES), lambda b, i: (0, prev(i), 0))],
        out_specs=pl.BlockSpec((ATTN_BLOCK, qw), lambda b, i: (b * nb + i, 0)),
        out_shape=jax.ShapeDtypeStruct((bsz * seq, qw), BF16),
        compiler_params=_params(), name="swa_attention",
    )(sinks, q, kv, kv, tables, tables)


def _router_kernel(h_ref, nw_ref, wr_ref, br_ref, xt_ref, route_ref):
    tm = h_ref.shape[0]
    xn = _rms(h_ref[...], nw_ref[...])
    for j in range(xn.shape[1] // LANES):
        xt_ref[pl.ds(j, tm, stride=SUBLANES), :] = xn[:, j * LANES:(j + 1) * LANES]
    logits = jnp.dot(xn, wr_ref[...], preferred_element_type=F32,
                     precision=lax.Precision.HIGHEST) + br_ref[...]
    lane = lax.broadcasted_iota(jnp.int32, logits.shape, 1)
    neg = -jnp.inf
    gl = jnp.where(lane < N_GROUPS, logits, neg)
    gmax = jnp.max(gl, axis=-1, keepdims=True)
    g_w = 1.0 / jnp.sum(jnp.exp(gl - gmax), axis=-1, keepdims=True)
    g_idx = jnp.min(jnp.where(gl == gmax, lane, LANES), axis=-1, keepdims=True)
    lo = N_GROUPS + g_idx * EXPERTS_PER_GROUP
    el = jnp.where((lane >= lo) & (lane < lo + EXPERTS_PER_GROUP), logits, neg)
    m1 = jnp.max(el, axis=-1, keepdims=True)
    i1 = jnp.min(jnp.where(el == m1, lane, LANES), axis=-1, keepdims=True)
    el2 = jnp.where(lane == i1, neg, el)
    m2 = jnp.max(el2, axis=-1, keepdims=True)
    i2 = jnp.min(jnp.where(el2 == m2, lane, LANES), axis=-1, keepdims=True)
    d = jnp.exp(m2 - m1)
    w1 = g_w / (1.0 + d)
    w2 = g_w * d / (1.0 + d)
    e1 = (i1 - N_GROUPS).astype(F32)
    e2 = (i2 - N_GROUPS).astype(F32)
    packed = jnp.where(lane == 0, e1, jnp.where(lane == 1, e2, jnp.where(lane == 2, w1, jnp.where(lane == 3, w2, 0.0))))
    route_ref[...] = packed[:, :route_ref.shape[1]]


def _router(h, norm_w, rg_w, rg_b, re_w, re_b):
    t, d = h.shape
    tm = min(ROW_TILE, t)
    pad = LANES - N_GROUPS - N_EXPERTS
    wr = jnp.concatenate([rg_w, re_w, jnp.zeros((d, pad), F32)], axis=1)
    br = jnp.concatenate([rg_b, re_b, jnp.zeros((pad,), F32)]).reshape(1, LANES)
    return pl.pallas_call(
        _router_kernel, grid=(t // tm,),
        in_specs=[pl.BlockSpec((tm, d), lambda i: (i, 0)), _resident((1, d)),
                  _resident((d, LANES)), _resident((1, LANES))],
        out_specs=[pl.BlockSpec((tm * SUBLANES, LANES), lambda i: (i, 0)),
                   pl.BlockSpec((tm, SUBLANES), lambda i: (i, 0))],
        out_shape=[jax.ShapeDtypeStruct((t * SUBLANES, LANES), F32),
                   jax.ShapeDtypeStruct((t, SUBLANES), F32)],
        compiler_params=_params(), name="moe_router",
    )(h, norm_w.reshape(1, d), wr, br)


def _moe_kernel(st_ref, cnt_ref, tok_ref, wt_ref, xs_ref, wgu_ref, wd_ref, o_ref, tile_ref, ybuf_ref):
    b = pl.program_id(0)
    e = pl.program_id(1)
    rows, stride = MOE_ROWS, MOE_STRIDE
    nslab = tile_ref.shape[0] // stride
    unroll = SUBLANES

    @pl.when(e == 0)
    def _():
        o_ref[...] = jnp.zeros_like(o_ref)

    start = st_ref[b * N_EXPERTS + e]
    count = cnt_ref[b * N_EXPERTS + e]

    def tile_body(t, carry):
        base = start + t * rows

        def gather(i, c):
            for u in range(unroll):
                r = i * unroll + u
                tok = tok_ref[0, 0, base + r]
                tile_ref[pl.ds(r, nslab, stride=stride), :] = xs_ref[pl.ds(pl.multiple_of(tok * SUBLANES, SUBLANES), SUBLANES), :]
            return c

        lax.fori_loop(0, rows // unroll, gather, 0)
        x = jnp.concatenate([tile_ref[pl.ds(j * stride, rows), :] for j in range(nslab)], axis=-1).astype(BF16)
        gu = jnp.dot(x, wgu_ref[...], preferred_element_type=F32)
        gate, up = gu[:, :D_EXPERT], gu[:, D_EXPERT:]
        he = (gate * _sigmoid(gate) * up).astype(BF16)
        y = jnp.dot(he, wd_ref[...], preferred_element_type=F32)
        for j in range(nslab):
            ybuf_ref[pl.ds(j * stride, rows), :] = y[:, j * LANES:(j + 1) * LANES]
        nvalid = jnp.minimum(count - t * rows, rows)

        def contribution(r):
            tok = tok_ref[0, 0, base + r]
            off = pl.multiple_of(tok * SUBLANES, SUBLANES)
            slab = ybuf_ref[pl.ds(r, nslab, stride=stride), :] * wt_ref[0, 0, base + r]
            return off, o_ref[pl.ds(off, SUBLANES), :] + slab

        def scatter_group(i, c):
            done = [contribution(i * unroll + u) for u in range(unroll)]
            for off, val in done:
                o_ref[pl.ds(off, SUBLANES), :] = val
            return c

        def scatter_one(r, c):
            off, val = contribution(r)
            o_ref[pl.ds(off, SUBLANES), :] = val
            return c

        full = nvalid // unroll
        lax.fori_loop(0, full, scatter_group, 0)
        lax.fori_loop(full * unroll, nvalid, scatter_one, 0)
        return carry

    lax.fori_loop(0, (count + rows - 1) // rows, tile_body, 0)


def _moe(xt, route, w_gu, w_down, t, d):
    m = min(MOE_BLOCK, t)
    nb = t // m
    e_idx = route[:, 0:2].astype(jnp.int32).reshape(nb, 2 * m)
    wts = route[:, 2:4].reshape(nb, 2 * m)
    order = jnp.argsort(e_idx, axis=1, stable=True)
    tok_sorted = (order // 2).astype(jnp.int32)
    w_sorted = jnp.take_along_axis(wts, order, axis=1)
    counts = jnp.sum((e_idx[:, :, None] == jnp.arange(N_EXPERTS)[None, None, :]).astype(jnp.int32), axis=1)
    starts = jnp.cumsum(counts, axis=1) - counts
    length = 2 * m + MOE_ROWS
    tok_sorted = jnp.pad(tok_sorted, ((0, 0), (0, MOE_ROWS))).reshape(nb, 1, length)
    w_sorted = jnp.pad(w_sorted, ((0, 0), (0, MOE_ROWS))).reshape(nb, 1, length)
    nslab = d // LANES
    grid_spec = pltpu.PrefetchScalarGridSpec(
        num_scalar_prefetch=2, grid=(nb, N_EXPERTS),
        in_specs=[pl.BlockSpec((1, 1, length), lambda b, e, *_: (b, 0, 0), memory_space=pltpu.SMEM),
                  pl.BlockSpec((1, 1, length), lambda b, e, *_: (b, 0, 0), memory_space=pltpu.SMEM),
                  pl.BlockSpec((m * SUBLANES, LANES), lambda b, e, *_: (b, 0), pipeline_mode=pl.Buffered(1)),
                  pl.BlockSpec((None, d, 2 * D_EXPERT), lambda b, e, *_: (e, 0, 0)),
                  pl.BlockSpec((None, D_EXPERT, d), lambda b, e, *_: (e, 0, 0))],
        out_specs=pl.BlockSpec((m * SUBLANES, LANES), lambda b, e, *_: (b, 0)),
        scratch_shapes=[pltpu.VMEM((nslab * MOE_STRIDE, LANES), F32),
                        pltpu.VMEM((nslab * MOE_STRIDE, LANES), F32)])
    return pl.pallas_call(
        _moe_kernel, grid_spec=grid_spec,
        out_shape=jax.ShapeDtypeStruct((t * SUBLANES, LANES), F32),
        compiler_params=_params(), name="moe_experts",
    )(starts.reshape(-1), counts.reshape(-1), tok_sorted, w_sorted, xt, w_gu, w_down)


def _combine_kernel(h_ref, mt_ref, nw_ref, o_ref, *, final_norm):
    tm, d = h_ref.shape
    parts = [mt_ref[pl.ds(j, tm, stride=SUBLANES), :] for j in range(d // LANES)]
    out = h_ref[...] + jnp.concatenate(parts, axis=-1)
    if final_norm:
        out = _rms(out, nw_ref[...])
    o_ref[...] = out


def _combine(h, moe_t, norm_w, final_norm, name):
    t, d = h.shape
    tm = min(ROW_TILE, t)
    return pl.pallas_call(
        functools.partial(_combine_kernel, final_norm=final_norm), grid=(t // tm,),
        in_specs=[pl.BlockSpec((tm, d), lambda i: (i, 0)),
                  pl.BlockSpec((tm * SUBLANES, LANES), lambda i: (i, 0)),
                  _resident((1, d))],
        out_specs=pl.BlockSpec((tm, d), lambda i: (i, 0)),
        out_shape=jax.ShapeDtypeStruct((t, d), F32),
        compiler_params=_params(), name=name,
    )(h, moe_t, norm_w.reshape(1, d))


def _moe_layer(h, norm_w, rg_w, rg_b, re_w, re_b, w_gate, w_up, w_down, out_norm_w, final_norm, name):
    t, d = h.shape
    xt, route = _router(h, norm_w, rg_w, rg_b, re_w, re_b)
    w_gu = jnp.concatenate([w_gate, w_up], axis=-1).astype(BF16)
    moe_t = _moe(xt, route, w_gu, w_down.astype(BF16), t, d)
    return _combine(h, moe_t, out_norm_w, final_norm, name)


def kernel(x, mem, a_norm_w, a_in_w, a_conv_w, a_conv_b, a_dt_bias, a_a_log, a_d_skip, a_gnorm_w, a_out_w, kv_norm_w, w_kv, b_norm_w, b_in_w, b_sinks, b_out_w, mem_norm_w, mem_w_kv, ffn_norm_w, router_group_w, router_group_b, router_expert_w, router_expert_b, w_gate, w_up, w_down, final_norm_w):
    bsz, seq, d = x.shape
    mem_len = mem.shape[1]
    t = bsz * seq
    h = x.reshape(t, d)
    mem2 = mem.reshape(bsz * mem_len, d)
    conv_ch = a_conv_w.shape[-1]
    n_a = a_in_w.shape[0]
    n_layers = n_a + b_in_w.shape[0]
    kv_sh = None
    for layer in range(n_layers):
        (mem_kv,) = _norm_matmul(mem2, mem_norm_w[layer], [mem_w_kv[layer].astype(BF16)], [BF16],
                                 name=f"mem_kv_{layer}")
        if layer < n_a:
            i = layer
            w_in = a_in_w[i]
            o_x, o_dt, o_q = SSD_INNER, SSD_INNER + conv_ch, SSD_INNER + conv_ch + SSD_HEADS
            w_dt = jnp.pad(w_in[:, o_dt:o_q], ((0, 0), (0, LANES - SSD_HEADS)))
            z, xbc, q_mem, dt = _norm_matmul(
                h, a_norm_w[i],
                [w_in[:, :o_x].astype(BF16), w_in[:, o_x:o_dt].astype(BF16),
                 w_in[:, o_q:].astype(BF16), w_dt.astype(BF16)],
                [BF16, BF16, BF16, F32], name=f"ssd_in_proj_{i}")
            y = _ssd(xbc, z, dt, a_conv_w[i], a_conv_b[i], a_dt_bias[i], a_a_log[i], a_d_skip[i],
                     a_gnorm_w[i], bsz, seq)
            m_att = _mem_attn(q_mem, mem_kv, bsz, seq, mem_len, name=f"mem_attn_{layer}")
            w_out = a_out_w[i].astype(BF16)
            h = _mix_out(y, m_att, w_out[:SSD_INNER], w_out[SSD_INNER:], h, name=f"ssd_out_proj_{i}")
        else:
            j = layer - n_a
            if j == 0:
                (kv_sh,) = _norm_matmul(h, kv_norm_w, [w_kv.astype(BF16)], [BF16], name="shared_kv")
            qw = Q_HEADS * HEAD_DIM
            w_in = b_in_w[j].astype(BF16)
            q, q_mem = _norm_matmul(h, b_norm_w[j], [w_in[:, :qw], w_in[:, qw:]], [BF16, BF16],
                                    name=f"swa_in_proj_{j}")
            att = _swa(q, kv_sh, b_sinks[j], bsz, seq)
            m_att = _mem_attn(q_mem, mem_kv, bsz, seq, mem_len, name=f"mem_attn_{layer}")
            w_out = b_out_w[j].astype(BF16)
            h = _mix_out(att, m_att, w_out[:qw], w_out[qw:], h, name=f"swa_out_proj_{j}")
        last = layer == n_layers - 1
        h = _moe_layer(h, ffn_norm_w[layer], router_group_w[layer], router_group_b[layer],
                       router_expert_w[layer], router_expert_b[layer],
                       w_gate[layer], w_up[layer], w_down[layer],
                       final_norm_w, last, name=f"moe_combine_{layer}")
    return h.reshape(bsz, seq, d)
```

```python
import functools

import jax
import jax.numpy as jnp
from jax import lax
from jax.experimental import pallas as pl
from jax.experimental.pallas import tpu as pltpu

F32 = jnp.float32
BF16 = jnp.bfloat16
NORM_EPS = 1e-6

LANES = 128
SUBLANES = 8
VMEM_LIMIT = 56 * 1024 * 1024

MEM_HEADS = 4
MEM_HEAD_DIM = 128
MEM_WIDTH = MEM_HEADS * MEM_HEAD_DIM
SSD_HEAD_DIM = 64
SSD_HEADS = 24
SSD_INNER = SSD_HEADS * SSD_HEAD_DIM
SSD_GROUPS = 4
SSD_STATE = 128
SSD_CHUNK = 128
CONV_WIDTH = 4
HEAD_DIM = 64
Q_HEADS = 16
KV_HEADS = 4
ATTN_BLOCK = 128
ROT_DIM = 16
ROPE_THETA = 500000.0
N_GROUPS = 4
EXPERTS_PER_GROUP = 8
N_EXPERTS = 32
D_EXPERT = 512

ROW_TILE = 512
MOE_BLOCK = 4096
MOE_UNIT = 64
MOE_CLASSES = 8
MOE_ROWS = MOE_UNIT * MOE_CLASSES
MOE_STRIDE = MOE_ROWS + SUBLANES
PACK = 2
PACK_ROWS = SUBLANES // PACK


def _params():
    return pltpu.CompilerParams(vmem_limit_bytes=VMEM_LIMIT)


def _resident(shape):
    nd = len(shape)
    return pl.BlockSpec(shape, lambda *_: (0,) * nd, pipeline_mode=pl.Buffered(1))


def _rms(x, w):
    ms = jnp.mean(x * x, axis=-1, keepdims=True)
    return x * lax.rsqrt(ms + NORM_EPS) * w


def _sigmoid(x):
    return 1.0 / (1.0 + jnp.exp(-x))


def _norm_matmul_kernel(x_ref, nw_ref, *refs, n_w):
    w_refs, o_refs = refs[:n_w], refs[n_w:]
    xn = _rms(x_ref[...], nw_ref[...]).astype(BF16)
    for w_ref, o_ref in zip(w_refs, o_refs):
        n = w_ref.shape[1]
        for c0 in range(0, n, 512):
            cw = min(512, n - c0)
            acc = jnp.dot(xn, w_ref[:, c0:c0 + cw], preferred_element_type=F32)
            o_ref[:, c0:c0 + cw] = acc.astype(o_ref.dtype)


def _norm_matmul(x, norm_w, weights, out_dtypes, name):
    t, d = x.shape
    tm = min(ROW_TILE, t)
    in_specs = [pl.BlockSpec((tm, d), lambda i: (i, 0)), _resident((1, d))]
    in_specs += [_resident(w.shape) for w in weights]
    out_specs = [pl.BlockSpec((tm, w.shape[1]), lambda i: (i, 0)) for w in weights]
    out_shape = [jax.ShapeDtypeStruct((t, w.shape[1]), dt) for w, dt in zip(weights, out_dtypes)]
    return pl.pallas_call(
        functools.partial(_norm_matmul_kernel, n_w=len(weights)),
        grid=(t // tm,), in_specs=in_specs, out_specs=out_specs, out_shape=out_shape,
        compiler_params=_params(), name=name,
    )(x, norm_w.reshape(1, d), *weights)


def _mix_out_kernel(a1_ref, a2_ref, w1_ref, w2_ref, r_ref, o_ref):
    acc = jnp.dot(a1_ref[...], w1_ref[...], preferred_element_type=F32)
    acc = acc + jnp.dot(a2_ref[...], w2_ref[...], preferred_element_type=F32)
    o_ref[...] = r_ref[...] + acc


def _mix_out(a1, a2, w1, w2, res, name):
    t, d = res.shape
    tm = min(ROW_TILE, t)
    return pl.pallas_call(
        _mix_out_kernel, grid=(t // tm,),
        in_specs=[pl.BlockSpec((tm, a1.shape[1]), lambda i: (i, 0)),
                  pl.BlockSpec((tm, a2.shape[1]), lambda i: (i, 0)),
                  _resident(w1.shape), _resident(w2.shape),
                  pl.BlockSpec((tm, d), lambda i: (i, 0))],
        out_specs=pl.BlockSpec((tm, d), lambda i: (i, 0)),
        out_shape=jax.ShapeDtypeStruct((t, d), F32),
        compiler_params=_params(), name=name,
    )(a1, a2, w1, w2, res)


def _mem_attn_kernel(q_ref, kv_ref, o_ref):
    scale = MEM_HEAD_DIM ** -0.5
    outs = []
    for h in range(MEM_HEADS):
        lo = h * MEM_HEAD_DIM
        q = q_ref[:, lo:lo + MEM_HEAD_DIM]
        k = kv_ref[:, lo:lo + MEM_HEAD_DIM]
        v = kv_ref[:, MEM_WIDTH + lo:MEM_WIDTH + lo + MEM_HEAD_DIM]
        s = lax.dot_general(q, k, (((1,), (1,)), ((), ())), preferred_element_type=F32) * scale
        p = jnp.exp(s - jnp.max(s, axis=-1, keepdims=True))
        p = p / jnp.sum(p, axis=-1, keepdims=True)
        outs.append(jnp.dot(p.astype(BF16), v, preferred_element_type=F32))
    o_ref[...] = jnp.concatenate(outs, axis=-1).astype(o_ref.dtype)


def _mem_attn(q, kv, bsz, seq, mem_len, name):
    tq = min(ROW_TILE, seq)
    nq = seq // tq
    return pl.pallas_call(
        _mem_attn_kernel, grid=(bsz, nq),
        in_specs=[pl.BlockSpec((tq, MEM_WIDTH), lambda b, i: (b * nq + i, 0)),
                  pl.BlockSpec((mem_len, 2 * MEM_WIDTH), lambda b, i: (b, 0))],
        out_specs=pl.BlockSpec((tq, MEM_WIDTH), lambda b, i: (b * nq + i, 0)),
        out_shape=jax.ShapeDtypeStruct((bsz * seq, MEM_WIDTH), BF16),
        compiler_params=_params(), name=name,
    )(q, kv)


def _ssd_kernel(xbc_ref, z_ref, dt_ref, cw_ref, cb_ref, dtb_ref, alog_ref, dsk_ref, gnw_ref,
                expand_ref, o_ref, ext_ref, st_ref):
    c = pl.program_id(1)
    L = SSD_CHUNK
    gw = SSD_INNER // SSD_GROUPS
    hpg = SSD_HEADS // SSD_GROUPS

    @pl.when(c == 0)
    def _():
        ext_ref[0:SUBLANES, :] = jnp.zeros((SUBLANES, ext_ref.shape[1]), F32)
        st_ref[...] = jnp.zeros_like(st_ref)

    @pl.when(c > 0)
    def _():
        ext_ref[0:SUBLANES, :] = ext_ref[L:L + SUBLANES, :]

    ext_ref[SUBLANES:SUBLANES + L, :] = xbc_ref[...].astype(F32)
    conv = cb_ref[...]
    for k in range(CONV_WIDTH):
        conv = conv + cw_ref[k:k + 1, :] * ext_ref[pl.ds(SUBLANES - (CONV_WIDTH - 1) + k, L), :]
    xbc = conv * _sigmoid(conv)
    xs = xbc[:, :SSD_INNER]
    bm = xbc[:, SSD_INNER:SSD_INNER + SSD_GROUPS * SSD_STATE].astype(BF16)
    cm = xbc[:, SSD_INNER + SSD_GROUPS * SSD_STATE:].astype(BF16)

    pre = dt_ref[...] + dtb_ref[...]
    dt = jnp.maximum(pre, 0.0) + jnp.log(1.0 + jnp.exp(-jnp.abs(pre)))
    a = -jnp.exp(alog_ref[...])
    adt = dt * a
    row = lax.broadcasted_iota(jnp.int32, (L, L), 0)
    col = lax.broadcasted_iota(jnp.int32, (L, L), 1)
    causal = row >= col
    acum = jnp.dot(causal.astype(F32), adt, preferred_element_type=F32,
                   precision=lax.Precision.HIGHEST)
    acum_t = acum.T
    ea = jnp.exp(acum)
    te = jnp.exp(acum[L - 1:L, :] - acum)
    stacked = jnp.concatenate([dt, ea, te], axis=0).astype(BF16)
    wide = jnp.dot(stacked, expand_ref[...], preferred_element_type=F32)
    dtx, eax, tex = wide[0:L], wide[L:2 * L], wide[2 * L:3 * L]
    xdt = xs * dtx
    xdt_b = xdt.astype(BF16)
    xw = (xdt * tex).astype(BF16)

    ys = []
    for g in range(SSD_GROUPS):
        cg = cm[:, g * SSD_STATE:(g + 1) * SSD_STATE]
        bg = bm[:, g * SSD_STATE:(g + 1) * SSD_STATE]
        cb = lax.dot_general(cg, bg, (((1,), (1,)), ((), ())), preferred_element_type=F32)
        state = st_ref[g]
        y_off = jnp.dot(cg, state.astype(BF16), preferred_element_type=F32) * eax[:, g * gw:(g + 1) * gw]
        parts = []
        for r in range(hpg):
            h = g * hpg + r
            seg = acum[:, h:h + 1] - acum_t[h:h + 1, :]
            m = (cb * jnp.exp(jnp.where(causal, seg, -jnp.inf))).astype(BF16)
            parts.append(jnp.dot(m, xdt_b[:, h * SSD_HEAD_DIM:(h + 1) * SSD_HEAD_DIM],
                                 preferred_element_type=F32))
        ys.append(jnp.concatenate(parts, axis=-1) + y_off)
        upd = lax.dot_general(bg, xw[:, g * gw:(g + 1) * gw], (((0,), (0,)), ((), ())),
                              preferred_element_type=F32)
        st_ref[g] = state * eax[L - 1:L, g * gw:(g + 1) * gw] + upd
    y = jnp.concatenate(ys, axis=-1) + dsk_ref[...] * xs
    zf = z_ref[...].astype(F32)
    y = y * (zf * _sigmoid(zf))
    normed = []
    for g in range(SSD_GROUPS):
        yg = y[:, g * gw:(g + 1) * gw]
        normed.append(yg * lax.rsqrt(jnp.mean(yg * yg, axis=-1, keepdims=True) + NORM_EPS))
    o_ref[...] = (jnp.concatenate(normed, axis=-1) * gnw_ref[...]).astype(o_ref.dtype)


def _ssd(xbc, z, dt, conv_w, conv_b, dt_bias, a_log, d_skip, gnorm_w, bsz, seq):
    nc = seq // SSD_CHUNK
    conv_ch = xbc.shape[1]
    pad = LANES - SSD_HEADS
    dtb = jnp.pad(dt_bias, (0, pad)).reshape(1, LANES)
    alog = jnp.pad(a_log, (0, pad)).reshape(1, LANES)
    dsk = jnp.repeat(d_skip, SSD_HEAD_DIM).reshape(1, SSD_INNER)
    expand = (jnp.arange(LANES)[:, None] == (jnp.arange(SSD_INNER) // SSD_HEAD_DIM)[None, :]).astype(BF16)
    blk = lambda w: pl.BlockSpec((SSD_CHUNK, w), lambda b, c: (b * nc + c, 0))
    return pl.pallas_call(
        _ssd_kernel, grid=(bsz, nc),
        in_specs=[blk(conv_ch), blk(SSD_INNER), blk(LANES),
                  _resident((CONV_WIDTH, conv_ch)), _resident((1, conv_ch)),
                  _resident((1, LANES)), _resident((1, LANES)),
                  _resident((1, SSD_INNER)), _resident((1, SSD_INNER)),
                  _resident((LANES, SSD_INNER))],
        out_specs=blk(SSD_INNER),
        out_shape=jax.ShapeDtypeStruct((bsz * seq, SSD_INNER), BF16),
        scratch_shapes=[pltpu.VMEM((SSD_CHUNK + 2 * SUBLANES, conv_ch), F32),
                        pltpu.VMEM((SSD_GROUPS, SSD_STATE, SSD_INNER // SSD_GROUPS), F32)],
        compiler_params=_params(), name="ssd_mixer",
    )(xbc, z, dt, conv_w, conv_b.reshape(1, conv_ch), dtb, alog, dsk,
      gnorm_w.reshape(1, SSD_INNER), expand)


def _rope(x, cos, sin_up, sin_dn):
    w = x.shape[1]
    reps = w // LANES
    tile = lambda t: jnp.concatenate([t] * reps, axis=-1) if reps > 1 else t
    half = ROT_DIM // 2
    return (x * tile(cos) + pltpu.roll(x, half, axis=1) * tile(sin_up)
            + pltpu.roll(x, w - half, axis=1) * tile(sin_dn))


def _swa_kernel(sink_ref, q_ref, kvp_ref, kvc_ref, tc_ref, tp_ref, o_ref):
    i = pl.program_id(1)
    blk = ATTN_BLOCK
    kvw = KV_HEADS * HEAD_DIM
    g = Q_HEADS // KV_HEADS
    scale = HEAD_DIM ** -0.5
    q = _rope(q_ref[...].astype(F32), tc_ref[0], tc_ref[1], tc_ref[2]).astype(BF16)
    k_prev = _rope(kvp_ref[:, :kvw].astype(F32), tp_ref[0], tp_ref[1], tp_ref[2])
    k_cur = _rope(kvc_ref[:, :kvw].astype(F32), tc_ref[0], tc_ref[1], tc_ref[2])
    k = jnp.concatenate([k_prev, k_cur], axis=0).astype(BF16)
    v = jnp.concatenate([kvp_ref[:, kvw:], kvc_ref[:, kvw:]], axis=0)
    t_idx = lax.broadcasted_iota(jnp.int32, (blk, 2 * blk), 0)
    s_idx = lax.broadcasted_iota(jnp.int32, (blk, 2 * blk), 1)
    valid = (s_idx > t_idx) & (s_idx <= t_idx + blk) & ((i > 0) | (s_idx >= blk))
    outs = []
    for h in range(Q_HEADS):
        kh = h // g
        qh = q[:, h * HEAD_DIM:(h + 1) * HEAD_DIM]
        kk = k[:, kh * HEAD_DIM:(kh + 1) * HEAD_DIM]
        vv = v[:, kh * HEAD_DIM:(kh + 1) * HEAD_DIM]
        sc = lax.dot_general(qh, kk, (((1,), (1,)), ((), ())), preferred_element_type=F32) * scale
        sc = jnp.where(valid, sc, -jnp.inf)
        sink = sink_ref[h]
        mx = jnp.maximum(jnp.max(sc, axis=-1, keepdims=True), sink)
        p = jnp.exp(sc - mx)
        denom = jnp.sum(p, axis=-1, keepdims=True) + jnp.exp(sink - mx)
        outs.append(jnp.dot((p / denom).astype(BF16), vv, preferred_element_type=F32))
    o_ref[...] = jnp.concatenate(outs, axis=-1).astype(o_ref.dtype)


def _rope_tables(seq):
    half = ROT_DIM // 2
    inv_freq = ROPE_THETA ** (-2.0 * jnp.arange(half, dtype=F32) / ROT_DIM)
    ang = jnp.arange(seq, dtype=jnp.int32).astype(F32)[:, None] * inv_freq[None, :]
    cos, sin = jnp.cos(ang), jnp.sin(ang)
    ones = jnp.ones((seq, HEAD_DIM - ROT_DIM), F32)
    zeros = jnp.zeros((seq, HEAD_DIM - ROT_DIM), F32)
    zh = jnp.zeros((seq, half), F32)
    cos_h = jnp.concatenate([cos, cos, ones], axis=-1)
    up_h = jnp.concatenate([zh, sin, zeros], axis=-1)
    dn_h = jnp.concatenate([-sin, zh, zeros], axis=-1)
    two = lambda t: jnp.concatenate([t, t], axis=-1)
    return jnp.stack([two(cos_h), two(up_h), two(dn_h)], axis=0)


def _swa(q, kv, sinks, bsz, seq):
    nb = seq // ATTN_BLOCK
    tables = _rope_tables(seq)
    qw = Q_HEADS * HEAD_DIM
    kvw2 = 2 * KV_HEADS * HEAD_DIM
    prev = lambda i: jnp.maximum(i - 1, 0)
    return pl.pallas_call(
        _swa_kernel, grid=(bsz, nb),
        in_specs=[pl.BlockSpec(memory_space=pltpu.SMEM),
                  pl.BlockSpec((ATTN_BLOCK, qw), lambda b, i: (b * nb + i, 0)),
                  pl.BlockSpec((ATTN_BLOCK, kvw2), lambda b, i: (b * nb + prev(i), 0)),
                  pl.BlockSpec((ATTN_BLOCK, kvw2), lambda b, i: (b * nb + i, 0)),
                  pl.BlockSpec((3, ATTN_BLOCK, LANES), lambda b, i: (0, i, 0)),
                  pl.BlockSpec((3, ATTN_BLOCK, LANES), lambda b, i: (0, prev(i), 0))],
        out_specs=pl.BlockSpec((ATTN_BLOCK, qw), lambda b, i: (b * nb + i, 0)),
        out_shape=jax.ShapeDtypeStruct((bsz * seq, qw), BF16),
        compiler_params=_params(), name="swa_attention",
    )(sinks, q, kv, kv, tables, tables)


def _router_kernel(h_ref, nw_ref, wr_ref, br_ref, xt_ref, route_ref):
    tm = h_ref.shape[0]
    xn = _rms(h_ref[...], nw_ref[...])
    half = xn.shape[1] // PACK
    bits = lax.bitcast_convert_type(xn.astype(BF16).astype(F32), jnp.uint32)
    for j in range(PACK_ROWS):
        lo = bits[:, j * LANES:(j + 1) * LANES] >> 16
        hi = bits[:, half + j * LANES:half + (j + 1) * LANES] & jnp.uint32(0xFFFF0000)
        xt_ref[pl.ds(j, tm, stride=PACK_ROWS), :] = lo | hi
    logits = jnp.dot(xn, wr_ref[...], preferred_element_type=F32,
                     precision=lax.Precision.HIGHEST) + br_ref[...]
    lane = lax.broadcasted_iota(jnp.int32, logits.shape, 1)
    neg = -jnp.inf
    gl = jnp.where(lane < N_GROUPS, logits, neg)
    gmax = jnp.max(gl, axis=-1, keepdims=True)
    g_w = 1.0 / jnp.sum(jnp.exp(gl - gmax), axis=-1, keepdims=True)
    g_idx = jnp.min(jnp.where(gl == gmax, lane, LANES), axis=-1, keepdims=True)
    lo = N_GROUPS + g_idx * EXPERTS_PER_GROUP
    el = jnp.where((lane >= lo) & (lane < lo + EXPERTS_PER_GROUP), logits, neg)
    m1 = jnp.max(el, axis=-1, keepdims=True)
    i1 = jnp.min(jnp.where(el == m1, lane, LANES), axis=-1, keepdims=True)
    el2 = jnp.where(lane == i1, neg, el)
    m2 = jnp.max(el2, axis=-1, keepdims=True)
    i2 = jnp.min(jnp.where(el2 == m2, lane, LANES), axis=-1, keepdims=True)
    d = jnp.exp(m2 - m1)
    w1 = g_w / (1.0 + d)
    w2 = g_w * d / (1.0 + d)
    e1 = (i1 - N_GROUPS).astype(F32)
    e2 = (i2 - N_GROUPS).astype(F32)
    packed = jnp.where(lane == 0, e1, jnp.where(lane == 1, e2, jnp.where(lane == 2, w1, jnp.where(lane == 3, w2, 0.0))))
    route_ref[...] = packed[:, :route_ref.shape[1]]


def _router(h, norm_w, rg_w, rg_b, re_w, re_b):
    t, d = h.shape
    tm = min(ROW_TILE, t)
    pad = LANES - N_GROUPS - N_EXPERTS
    wr = jnp.concatenate([rg_w, re_w, jnp.zeros((d, pad), F32)], axis=1)
    br = jnp.concatenate([rg_b, re_b, jnp.zeros((pad,), F32)]).reshape(1, LANES)
    return pl.pallas_call(
        _router_kernel, grid=(t // tm,),
        in_specs=[pl.BlockSpec((tm, d), lambda i: (i, 0)), _resident((1, d)),
                  _resident((d, LANES)), _resident((1, LANES))],
        out_specs=[pl.BlockSpec((tm * PACK_ROWS, LANES), lambda i: (i, 0)),
                   pl.BlockSpec((tm, SUBLANES), lambda i: (i, 0))],
        out_shape=[jax.ShapeDtypeStruct((t * PACK_ROWS, LANES), jnp.uint32),
                   jax.ShapeDtypeStruct((t, SUBLANES), F32)],
        compiler_params=_params(), name="moe_router",
    )(h, norm_w.reshape(1, d), wr, br)


def _moe_kernel(st_ref, cnt_ref, tok_ref, wt_ref, xs_ref, wgu_ref, wd_ref, o_ref, tile_ref, ybuf_ref):
    b = pl.program_id(0)
    e = pl.program_id(1)
    stride = MOE_STRIDE
    nslab = ybuf_ref.shape[0] // stride
    unroll = SUBLANES

    @pl.when((b == 0) & (e == 0))
    def _():
        tile_ref[...] = jnp.zeros_like(tile_ref)

    @pl.when(e == 0)
    def _():
        o_ref[...] = jnp.zeros_like(o_ref)

    start = st_ref[b * N_EXPERTS + e]
    count = cnt_ref[b * N_EXPERTS + e]

    def expert_mlp(rows):
        def run():
            lo, hi = [], []
            for j in range(PACK_ROWS):
                word = tile_ref[pl.ds(j * stride, rows), :]
                lo.append(lax.bitcast_convert_type(word << 16, F32))
                hi.append(lax.bitcast_convert_type(word & jnp.uint32(0xFFFF0000), F32))
            x = jnp.concatenate(lo + hi, axis=-1).astype(BF16)
            gu = jnp.dot(x, wgu_ref[...], preferred_element_type=F32)
            gate, up = gu[:, :D_EXPERT], gu[:, D_EXPERT:]
            he = (gate * _sigmoid(gate) * up).astype(BF16)
            y = jnp.dot(he, wd_ref[...], preferred_element_type=F32)
            for j in range(nslab):
                ybuf_ref[pl.ds(j * stride, rows), :] = y[:, j * LANES:(j + 1) * LANES]
        return run

    def tile_body(t, carry):
        base = start + t * MOE_ROWS
        nvalid = jnp.minimum(count - t * MOE_ROWS, MOE_ROWS)

        def gather(i, c):
            for u in range(unroll):
                r = i * unroll + u
                tok = tok_ref[0, 0, base + r]
                src = pl.ds(pl.multiple_of(tok * PACK_ROWS, PACK_ROWS), PACK_ROWS)
                tile_ref[pl.ds(r, PACK_ROWS, stride=stride), :] = xs_ref[src, :]
            return c

        lax.fori_loop(0, (nvalid + unroll - 1) // unroll, gather, 0)
        size_class = (nvalid + MOE_UNIT - 1) // MOE_UNIT - 1
        lax.switch(size_class, [expert_mlp(MOE_UNIT * (k + 1)) for k in range(MOE_CLASSES)])

        def contribution(r):
            tok = tok_ref[0, 0, base + r]
            off = pl.multiple_of(tok * SUBLANES, SUBLANES)
            slab = ybuf_ref[pl.ds(r, nslab, stride=stride), :] * wt_ref[0, 0, base + r]
            return off, o_ref[pl.ds(off, SUBLANES), :] + slab

        def scatter_group(i, c):
            done = [contribution(i * unroll + u) for u in range(unroll)]
            for off, val in done:
                o_ref[pl.ds(off, SUBLANES), :] = val
            return c

        def scatter_one(r, c):
            off, val = contribution(r)
            o_ref[pl.ds(off, SUBLANES), :] = val
            return c

        full = nvalid // unroll
        lax.fori_loop(0, full, scatter_group, 0)
        lax.fori_loop(full * unroll, nvalid, scatter_one, 0)
        return carry

    lax.fori_loop(0, (count + MOE_ROWS - 1) // MOE_ROWS, tile_body, 0)


def _moe(xt, route, w_gu, w_down, t, d):
    m = min(MOE_BLOCK, t)
    nb = t // m
    e_idx = route[:, 0:2].astype(jnp.int32).reshape(nb, 2 * m)
    wts = route[:, 2:4].reshape(nb, 2 * m)
    order = jnp.argsort(e_idx, axis=1, stable=True)
    tok_sorted = (order // 2).astype(jnp.int32)
    w_sorted = jnp.take_along_axis(wts, order, axis=1)
    counts = jnp.sum((e_idx[:, :, None] == jnp.arange(N_EXPERTS)[None, None, :]).astype(jnp.int32), axis=1)
    starts = jnp.cumsum(counts, axis=1) - counts
    length = 2 * m + MOE_ROWS
    tok_sorted = jnp.pad(tok_sorted, ((0, 0), (0, MOE_ROWS))).reshape(nb, 1, length)
    w_sorted = jnp.pad(w_sorted, ((0, 0), (0, MOE_ROWS))).reshape(nb, 1, length)
    nslab = d // LANES
    grid_spec = pltpu.PrefetchScalarGridSpec(
        num_scalar_prefetch=2, grid=(nb, N_EXPERTS),
        in_specs=[pl.BlockSpec((1, 1, length), lambda b, e, *_: (b, 0, 0), memory_space=pltpu.SMEM),
                  pl.BlockSpec((1, 1, length), lambda b, e, *_: (b, 0, 0), memory_space=pltpu.SMEM),
                  pl.BlockSpec((m * PACK_ROWS, LANES), lambda b, e, *_: (b, 0), pipeline_mode=pl.Buffered(1)),
                  pl.BlockSpec((None, d, 2 * D_EXPERT), lambda b, e, *_: (e, 0, 0)),
                  pl.BlockSpec((None, D_EXPERT, d), lambda b, e, *_: (e, 0, 0))],
        out_specs=pl.BlockSpec((m * SUBLANES, LANES), lambda b, e, *_: (b, 0)),
        scratch_shapes=[pltpu.VMEM((PACK_ROWS * MOE_STRIDE, LANES), jnp.uint32),
                        pltpu.VMEM((nslab * MOE_STRIDE, LANES), F32)])
    return pl.pallas_call(
        _moe_kernel, grid_spec=grid_spec,
        out_shape=jax.ShapeDtypeStruct((t * SUBLANES, LANES), F32),
        compiler_params=_params(), name="moe_experts",
    )(starts.reshape(-1), counts.reshape(-1), tok_sorted, w_sorted, xt, w_gu, w_down)


def _combine_kernel(h_ref, mt_ref, nw_ref, o_ref, *, final_norm):
    tm, d = h_ref.shape
    parts = [mt_ref[pl.ds(j, tm, stride=SUBLANES), :] for j in range(d // LANES)]
    out = h_ref[...] + jnp.concatenate(parts, axis=-1)
    if final_norm:
        out = _rms(out, nw_ref[...])
    o_ref[...] = out


def _combine(h, moe_t, norm_w, final_norm, name):
    t, d = h.shape
    tm = min(ROW_TILE, t)
    return pl.pallas_call(
        functools.partial(_combine_kernel, final_norm=final_norm), grid=(t // tm,),
        in_specs=[pl.BlockSpec((tm, d), lambda i: (i, 0)),
                  pl.BlockSpec((tm * SUBLANES, LANES), lambda i: (i, 0)),
                  _resident((1, d))],
        out_specs=pl.BlockSpec((tm, d), lambda i: (i, 0)),
        out_shape=jax.ShapeDtypeStruct((t, d), F32),
        compiler_params=_params(), name=name,
    )(h, moe_t, norm_w.reshape(1, d))


def _moe_layer(h, norm_w, rg_w, rg_b, re_w, re_b, w_gate, w_up, w_down, out_norm_w, final_norm, name):
    t, d = h.shape
    xt, route = _router(h, norm_w, rg_w, rg_b, re_w, re_b)
    w_gu = jnp.concatenate([w_gate, w_up], axis=-1).astype(BF16)
    moe_t = _moe(xt, route, w_gu, w_down.astype(BF16), t, d)
    return _combine(h, moe_t, out_norm_w, final_norm, name)


def kernel(x, mem, a_norm_w, a_in_w, a_conv_w, a_conv_b, a_dt_bias, a_a_log, a_d_skip, a_gnorm_w, a_out_w, kv_norm_w, w_kv, b_norm_w, b_in_w, b_sinks, b_out_w, mem_norm_w, mem_w_kv, ffn_norm_w, router_group_w, router_group_b, router_expert_w, router_expert_b, w_gate, w_up, w_down, final_norm_w):
    bsz, seq, d = x.shape
    mem_len = mem.shape[1]
    t = bsz * seq
    h = x.reshape(t, d)
    mem2 = mem.reshape(bsz * mem_len, d)
    conv_ch = a_conv_w.shape[-1]
    n_a = a_in_w.shape[0]
    n_layers = n_a + b_in_w.shape[0]
    kv_sh = None
    for layer in range(n_layers):
        (mem_kv,) = _norm_matmul(mem2, mem_norm_w[layer], [mem_w_kv[layer].astype(BF16)], [BF16],
                                 name=f"mem_kv_{layer}")
        if layer < n_a:
            i = layer
            w_in = a_in_w[i]
            o_x, o_dt, o_q = SSD_INNER, SSD_INNER + conv_ch, SSD_INNER + conv_ch + SSD_HEADS
            w_dt = jnp.pad(w_in[:, o_dt:o_q], ((0, 0), (0, LANES - SSD_HEADS)))
            z, xbc, q_mem, dt = _norm_matmul(
                h, a_norm_w[i],
                [w_in[:, :o_x].astype(BF16), w_in[:, o_x:o_dt].astype(BF16),
                 w_in[:, o_q:].astype(BF16), w_dt.astype(BF16)],
                [BF16, BF16, BF16, F32], name=f"ssd_in_proj_{i}")
            y = _ssd(xbc, z, dt, a_conv_w[i], a_conv_b[i], a_dt_bias[i], a_a_log[i], a_d_skip[i],
                     a_gnorm_w[i], bsz, seq)
            m_att = _mem_attn(q_mem, mem_kv, bsz, seq, mem_len, name=f"mem_attn_{layer}")
            w_out = a_out_w[i].astype(BF16)
            h = _mix_out(y, m_att, w_out[:SSD_INNER], w_out[SSD_INNER:], h, name=f"ssd_out_proj_{i}")
        else:
            j = layer - n_a
            if j == 0:
                (kv_sh,) = _norm_matmul(h, kv_norm_w, [w_kv.astype(BF16)], [BF16], name="shared_kv")
            qw = Q_HEADS * HEAD_DIM
            w_in = b_in_w[j].astype(BF16)
            q, q_mem = _norm_matmul(h, b_norm_w[j], [w_in[:, :qw], w_in[:, qw:]], [BF16, BF16],
                                    name=f"swa_in_proj_{j}")
            att = _swa(q, kv_sh, b_sinks[j], bsz, seq)
            m_att = _mem_attn(q_mem, mem_kv, bsz, seq, mem_len, name=f"mem_attn_{layer}")
            w_out = b_out_w[j].astype(BF16)
            h = _mix_out(att, m_att, w_out[:qw], w_out[qw:], h, name=f"swa_out_proj_{j}")
        last = layer == n_layers - 1
        h = _moe_layer(h, ffn_norm_w[layer], router_group_w[layer], router_group_b[layer],
                       router_expert_w[layer], router_expert_b[layer],
                       w_gate[layer], w_up[layer], w_down[layer],
                       final_norm_w, last, name=f"moe_combine_{layer}")
    return h.reshape(bsz, seq, d)
```

```python
import functools

import jax
import jax.numpy as jnp
from jax import lax
from jax.experimental import pallas as pl
from jax.experimental.pallas import tpu as pltpu

F32 = jnp.float32
BF16 = jnp.bfloat16
NORM_EPS = 1e-6

LANES = 128
SUBLANES = 8
VMEM_LIMIT = 56 * 1024 * 1024

MEM_HEADS = 4
MEM_HEAD_DIM = 128
MEM_WIDTH = MEM_HEADS * MEM_HEAD_DIM
SSD_HEAD_DIM = 64
SSD_HEADS = 24
SSD_INNER = SSD_HEADS * SSD_HEAD_DIM
SSD_GROUPS = 4
SSD_STATE = 128
SSD_CHUNK = 128
CONV_WIDTH = 4
HEAD_DIM = 64
Q_HEADS = 16
KV_HEADS = 4
ATTN_BLOCK = 128
ROT_DIM = 16
ROPE_THETA = 500000.0
N_GROUPS = 4
EXPERTS_PER_GROUP = 8
N_EXPERTS = 32
D_EXPERT = 512

ROW_TILE = 512
MOE_BLOCK = 4096
MOE_UNIT = 64
MOE_CLASSES = 8
MOE_ROWS = MOE_UNIT * MOE_CLASSES
MOE_STRIDE = MOE_ROWS + SUBLANES
PACK = 2
PACK_ROWS = SUBLANES // PACK


def _params():
    return pltpu.CompilerParams(vmem_limit_bytes=VMEM_LIMIT)


def _resident(shape):
    nd = len(shape)
    return pl.BlockSpec(shape, lambda *_: (0,) * nd, pipeline_mode=pl.Buffered(1))


def _rms(x, w):
    ms = jnp.mean(x * x, axis=-1, keepdims=True)
    return x * lax.rsqrt(ms + NORM_EPS) * w


def _sigmoid(x):
    return 1.0 / (1.0 + jnp.exp(-x))


def _norm_matmul_kernel(x_ref, nw_ref, *refs, n_w):
    w_refs, o_refs = refs[:n_w], refs[n_w:]
    xn = _rms(x_ref[...], nw_ref[...]).astype(BF16)
    for w_ref, o_ref in zip(w_refs, o_refs):
        n = w_ref.shape[1]
        for c0 in range(0, n, 512):
            cw = min(512, n - c0)
            acc = jnp.dot(xn, w_ref[:, c0:c0 + cw], preferred_element_type=F32)
            o_ref[:, c0:c0 + cw] = acc.astype(o_ref.dtype)


def _norm_matmul(x, norm_w, weights, out_dtypes, name):
    t, d = x.shape
    tm = min(ROW_TILE, t)
    in_specs = [pl.BlockSpec((tm, d), lambda i: (i, 0)), _resident((1, d))]
    in_specs += [_resident(w.shape) for w in weights]
    out_specs = [pl.BlockSpec((tm, w.shape[1]), lambda i: (i, 0)) for w in weights]
    out_shape = [jax.ShapeDtypeStruct((t, w.shape[1]), dt) for w, dt in zip(weights, out_dtypes)]
    return pl.pallas_call(
        functools.partial(_norm_matmul_kernel, n_w=len(weights)),
        grid=(t // tm,), in_specs=in_specs, out_specs=out_specs, out_shape=out_shape,
        compiler_params=_params(), name=name,
    )(x, norm_w.reshape(1, d), *weights)


def _mix_out_kernel(a1_ref, a2_ref, w1_ref, w2_ref, r_ref, o_ref):
    acc = jnp.dot(a1_ref[...], w1_ref[...], preferred_element_type=F32)
    acc = acc + jnp.dot(a2_ref[...], w2_ref[...], preferred_element_type=F32)
    o_ref[...] = r_ref[...] + acc


def _mix_out(a1, a2, w1, w2, res, name):
    t, d = res.shape
    tm = min(ROW_TILE, t)
    return pl.pallas_call(
        _mix_out_kernel, grid=(t // tm,),
        in_specs=[pl.BlockSpec((tm, a1.shape[1]), lambda i: (i, 0)),
                  pl.BlockSpec((tm, a2.shape[1]), lambda i: (i, 0)),
                  _resident(w1.shape), _resident(w2.shape),
                  pl.BlockSpec((tm, d), lambda i: (i, 0))],
        out_specs=pl.BlockSpec((tm, d), lambda i: (i, 0)),
        out_shape=jax.ShapeDtypeStruct((t, d), F32),
        compiler_params=_params(), name=name,
    )(a1, a2, w1, w2, res)


def _mem_attn_kernel(q_ref, kv_ref, o_ref):
    scale = MEM_HEAD_DIM ** -0.5
    outs = []
    for h in range(MEM_HEADS):
        lo = h * MEM_HEAD_DIM
        q = q_ref[:, lo:lo + MEM_HEAD_DIM]
        k = kv_ref[:, lo:lo + MEM_HEAD_DIM]
        v = kv_ref[:, MEM_WIDTH + lo:MEM_WIDTH + lo + MEM_HEAD_DIM]
        s = lax.dot_general(q, k, (((1,), (1,)), ((), ())), preferred_element_type=F32) * scale
        p = jnp.exp(s - jnp.max(s, axis=-1, keepdims=True))
        p = p / jnp.sum(p, axis=-1, keepdims=True)
        outs.append(jnp.dot(p.astype(BF16), v, preferred_element_type=F32))
    o_ref[...] = jnp.concatenate(outs, axis=-1).astype(o_ref.dtype)


def _mem_attn(q, kv, bsz, seq, mem_len, name):
    tq = min(ROW_TILE, seq)
    nq = seq // tq
    return pl.pallas_call(
        _mem_attn_kernel, grid=(bsz, nq),
        in_specs=[pl.BlockSpec((tq, MEM_WIDTH), lambda b, i: (b * nq + i, 0)),
                  pl.BlockSpec((mem_len, 2 * MEM_WIDTH), lambda b, i: (b, 0))],
        out_specs=pl.BlockSpec((tq, MEM_WIDTH), lambda b, i: (b * nq + i, 0)),
        out_shape=jax.ShapeDtypeStruct((bsz * seq, MEM_WIDTH), BF16),
        compiler_params=_params(), name=name,
    )(q, kv)


def _ssd_kernel(xbc_ref, z_ref, dt_ref, cw_ref, cb_ref, dtb_ref, alog_ref, dsk_ref, gnw_ref,
                expand_ref, o_ref, ext_ref, st_ref):
    c = pl.program_id(1)
    L = SSD_CHUNK
    gw = SSD_INNER // SSD_GROUPS
    hpg = SSD_HEADS // SSD_GROUPS

    @pl.when(c == 0)
    def _():
        ext_ref[0:SUBLANES, :] = jnp.zeros((SUBLANES, ext_ref.shape[1]), F32)
        st_ref[...] = jnp.zeros_like(st_ref)

    @pl.when(c > 0)
    def _():
        ext_ref[0:SUBLANES, :] = ext_ref[L:L + SUBLANES, :]

    ext_ref[SUBLANES:SUBLANES + L, :] = xbc_ref[...].astype(F32)
    conv = cb_ref[...]
    for k in range(CONV_WIDTH):
        conv = conv + cw_ref[k:k + 1, :] * ext_ref[pl.ds(SUBLANES - (CONV_WIDTH - 1) + k, L), :]
    xbc = conv * _sigmoid(conv)
    xs = xbc[:, :SSD_INNER]
    bm = xbc[:, SSD_INNER:SSD_INNER + SSD_GROUPS * SSD_STATE]
    cm = xbc[:, SSD_INNER + SSD_GROUPS * SSD_STATE:].astype(BF16)

    pre = dt_ref[...] + dtb_ref[...]
    dt = jnp.maximum(pre, 0.0) + jnp.log(1.0 + jnp.exp(-jnp.abs(pre)))
    a = -jnp.exp(alog_ref[...])
    adt = dt * a
    row = lax.broadcasted_iota(jnp.int32, (L, L), 0)
    col = lax.broadcasted_iota(jnp.int32, (L, L), 1)
    causal = row >= col
    acum = jnp.dot(causal.astype(F32), adt, preferred_element_type=F32,
                   precision=lax.Precision.HIGHEST)
    acum_t = acum.T
    ea = jnp.exp(acum)
    te = jnp.exp(acum[L - 1:L, :] - acum)
    widen = lambda cols: jnp.dot(cols.astype(BF16), expand_ref[...], preferred_element_type=F32)
    dtx, eax, tex = widen(dt), widen(ea), widen(te)
    xdt = xs * dtx
    xdt_b = xdt.astype(BF16)
    xw = (xdt * tex).astype(BF16)

    ys = []
    for g in range(SSD_GROUPS):
        cg = cm[:, g * SSD_STATE:(g + 1) * SSD_STATE]
        bg_t = bm[:, g * SSD_STATE:(g + 1) * SSD_STATE].T.astype(BF16)
        cb = jnp.dot(cg, bg_t, preferred_element_type=F32)
        state = st_ref[g]
        y_off = jnp.dot(cg, state.astype(BF16), preferred_element_type=F32) * eax[:, g * gw:(g + 1) * gw]
        parts = []
        for r in range(hpg):
            h = g * hpg + r
            seg = acum[:, h:h + 1] - acum_t[h:h + 1, :]
            m = (cb * jnp.exp(jnp.where(causal, seg, -jnp.inf))).astype(BF16)
            parts.append(jnp.dot(m, xdt_b[:, h * SSD_HEAD_DIM:(h + 1) * SSD_HEAD_DIM],
                                 preferred_element_type=F32))
        ys.append(jnp.concatenate(parts, axis=-1) + y_off)
        upd = jnp.dot(bg_t, xw[:, g * gw:(g + 1) * gw], preferred_element_type=F32)
        st_ref[g] = state * eax[L - 1:L, g * gw:(g + 1) * gw] + upd
    y = jnp.concatenate(ys, axis=-1) + dsk_ref[...] * xs
    zf = z_ref[...].astype(F32)
    y = y * (zf * _sigmoid(zf))
    normed = []
    for g in range(SSD_GROUPS):
        yg = y[:, g * gw:(g + 1) * gw]
        normed.append(yg * lax.rsqrt(jnp.mean(yg * yg, axis=-1, keepdims=True) + NORM_EPS))
    o_ref[...] = (jnp.concatenate(normed, axis=-1) * gnw_ref[...]).astype(o_ref.dtype)


def _ssd(xbc, z, dt, conv_w, conv_b, dt_bias, a_log, d_skip, gnorm_w, bsz, seq):
    nc = seq // SSD_CHUNK
    conv_ch = xbc.shape[1]
    pad = LANES - SSD_HEADS
    dtb = jnp.pad(dt_bias, (0, pad)).reshape(1, LANES)
    alog = jnp.pad(a_log, (0, pad)).reshape(1, LANES)
    dsk = jnp.repeat(d_skip, SSD_HEAD_DIM).reshape(1, SSD_INNER)
    expand = (jnp.arange(LANES)[:, None] == (jnp.arange(SSD_INNER) // SSD_HEAD_DIM)[None, :]).astype(BF16)
    blk = lambda w: pl.BlockSpec((SSD_CHUNK, w), lambda b, c: (b * nc + c, 0))
    return pl.pallas_call(
        _ssd_kernel, grid=(bsz, nc),
        in_specs=[blk(conv_ch), blk(SSD_INNER), blk(LANES),
                  _resident((CONV_WIDTH, conv_ch)), _resident((1, conv_ch)),
                  _resident((1, LANES)), _resident((1, LANES)),
                  _resident((1, SSD_INNER)), _resident((1, SSD_INNER)),
                  _resident((LANES, SSD_INNER))],
        out_specs=blk(SSD_INNER),
        out_shape=jax.ShapeDtypeStruct((bsz * seq, SSD_INNER), BF16),
        scratch_shapes=[pltpu.VMEM((SSD_CHUNK + 2 * SUBLANES, conv_ch), F32),
                        pltpu.VMEM((SSD_GROUPS, SSD_STATE, SSD_INNER // SSD_GROUPS), F32)],
        compiler_params=_params(), name="ssd_mixer",
    )(xbc, z, dt, conv_w, conv_b.reshape(1, conv_ch), dtb, alog, dsk,
      gnorm_w.reshape(1, SSD_INNER), expand)


def _rope(x, cos, sin_up, sin_dn):
    w = x.shape[1]
    reps = w // LANES
    tile = lambda t: jnp.concatenate([t] * reps, axis=-1) if reps > 1 else t
    half = ROT_DIM // 2
    return (x * tile(cos) + pltpu.roll(x, half, axis=1) * tile(sin_up)
            + pltpu.roll(x, w - half, axis=1) * tile(sin_dn))


def _swa_kernel(sink_ref, q_ref, kvp_ref, kvc_ref, tc_ref, tp_ref, bias_ref, o_ref):
    kvw = KV_HEADS * HEAD_DIM
    g = Q_HEADS // KV_HEADS
    scale = HEAD_DIM ** -0.5
    q = _rope(q_ref[...].astype(F32), tc_ref[0], tc_ref[1], tc_ref[2]) * scale
    q_t = q.T.astype(BF16)
    k_prev = _rope(kvp_ref[:, :kvw].astype(F32), tp_ref[0], tp_ref[1], tp_ref[2])
    k_cur = _rope(kvc_ref[:, :kvw].astype(F32), tc_ref[0], tc_ref[1], tc_ref[2])
    k = jnp.concatenate([k_prev, k_cur], axis=0).astype(BF16)
    v = jnp.concatenate([kvp_ref[:, kvw:], kvc_ref[:, kvw:]], axis=0).astype(F32)
    v_t = v.T.astype(BF16)
    bias = bias_ref[0]
    outs = []
    for kh in range(KV_HEADS):
        kk = k[:, kh * HEAD_DIM:(kh + 1) * HEAD_DIM]
        sc = jnp.concatenate(
            [jnp.dot(kk, q_t[h * HEAD_DIM:(h + 1) * HEAD_DIM, :], preferred_element_type=F32)
             for h in range(kh * g, (kh + 1) * g)], axis=-1) + bias
        sink = jnp.concatenate([jnp.full((1, ATTN_BLOCK), sink_ref[h], F32)
                                for h in range(kh * g, (kh + 1) * g)], axis=-1)
        mx = jnp.maximum(jnp.max(sc, axis=0, keepdims=True), sink)
        p = jnp.exp(sc - mx)
        denom = jnp.sum(p, axis=0, keepdims=True) + jnp.exp(sink - mx)
        o_t = jnp.dot(v_t[kh * HEAD_DIM:(kh + 1) * HEAD_DIM, :], p.astype(BF16),
                      preferred_element_type=F32) / denom
        outs += [o_t[:, j * ATTN_BLOCK:(j + 1) * ATTN_BLOCK] for j in range(g)]
    o_ref[...] = jnp.concatenate(outs, axis=0).T.astype(o_ref.dtype)


def _swa_bias(g):
    blk = ATTN_BLOCK
    s_idx = jnp.arange(2 * blk)[:, None]
    t_idx = jnp.arange(blk)[None, :]
    band = (s_idx > t_idx) & (s_idx <= t_idx + blk)
    first = band & (s_idx >= blk)
    both = jnp.stack([first, band], axis=0)
    return jnp.tile(jnp.where(both, 0.0, -jnp.inf).astype(F32), (1, 1, g))


def _rope_tables(seq):
    half = ROT_DIM // 2
    inv_freq = ROPE_THETA ** (-2.0 * jnp.arange(half, dtype=F32) / ROT_DIM)
    ang = jnp.arange(seq, dtype=jnp.int32).astype(F32)[:, None] * inv_freq[None, :]
    cos, sin = jnp.cos(ang), jnp.sin(ang)
    ones = jnp.ones((seq, HEAD_DIM - ROT_DIM), F32)
    zeros = jnp.zeros((seq, HEAD_DIM - ROT_DIM), F32)
    zh = jnp.zeros((seq, half), F32)
    cos_h = jnp.concatenate([cos, cos, ones], axis=-1)
    up_h = jnp.concatenate([zh, sin, zeros], axis=-1)
    dn_h = jnp.concatenate([-sin, zh, zeros], axis=-1)
    two = lambda t: jnp.concatenate([t, t], axis=-1)
    return jnp.stack([two(cos_h), two(up_h), two(dn_h)], axis=0)


def _swa(q, kv, sinks, bsz, seq):
    nb = seq // ATTN_BLOCK
    tables = _rope_tables(seq)
    qw = Q_HEADS * HEAD_DIM
    kvw2 = 2 * KV_HEADS * HEAD_DIM
    prev = lambda i: jnp.maximum(i - 1, 0)
    g = Q_HEADS // KV_HEADS
    return pl.pallas_call(
        _swa_kernel, grid=(bsz, nb),
        in_specs=[pl.BlockSpec(memory_space=pltpu.SMEM),
                  pl.BlockSpec((ATTN_BLOCK, qw), lambda b, i: (b * nb + i, 0)),
                  pl.BlockSpec((ATTN_BLOCK, kvw2), lambda b, i: (b * nb + prev(i), 0)),
                  pl.BlockSpec((ATTN_BLOCK, kvw2), lambda b, i: (b * nb + i, 0)),
                  pl.BlockSpec((3, ATTN_BLOCK, LANES), lambda b, i: (0, i, 0)),
                  pl.BlockSpec((3, ATTN_BLOCK, LANES), lambda b, i: (0, prev(i), 0)),
                  pl.BlockSpec((1, 2 * ATTN_BLOCK, g * ATTN_BLOCK), lambda b, i: (jnp.minimum(i, 1), 0, 0))],
        out_specs=pl.BlockSpec((ATTN_BLOCK, qw), lambda b, i: (b * nb + i, 0)),
        out_shape=jax.ShapeDtypeStruct((bsz * seq, qw), BF16),
        compiler_params=_params(), name="swa_attention",
    )(sinks, q, kv, kv, tables, tables, _swa_bias(g))


def _router_kernel(h_ref, nw_ref, wr_ref, br_ref, xt_ref, route_ref, cnt_ref):
    tm = h_ref.shape[0]
    xn = _rms(h_ref[...], nw_ref[...])
    x_hi = xn.astype(BF16)
    x_hi32 = x_hi.astype(F32)
    half = xn.shape[1] // PACK
    bits = lax.bitcast_convert_type(x_hi32, jnp.uint32)
    for j in range(PACK_ROWS):
        lo = bits[:, j * LANES:(j + 1) * LANES] >> 16
        hi = bits[:, half + j * LANES:half + (j + 1) * LANES] & jnp.uint32(0xFFFF0000)
        xt_ref[pl.ds(j, tm, stride=PACK_ROWS), :] = lo | hi
    x_lo = (xn - x_hi32).astype(BF16)
    both = jnp.dot(x_hi, wr_ref[...], preferred_element_type=F32)
    logits = (both[:, :LANES] + both[:, LANES:]
              + jnp.dot(x_lo, wr_ref[:, :LANES], preferred_element_type=F32) + br_ref[...])
    lt = logits.T
    neg = -jnp.inf
    grow = lax.broadcasted_iota(jnp.int32, (SUBLANES, tm), 0)
    gl = jnp.where(grow < N_GROUPS, lt[:SUBLANES], neg)
    gmax = jnp.max(gl, axis=0, keepdims=True)
    g_w = 1.0 / jnp.sum(jnp.exp(gl - gmax), axis=0, keepdims=True)
    g_idx = jnp.min(jnp.where(gl == gmax, grow, SUBLANES), axis=0, keepdims=True)
    erow = lax.broadcasted_iota(jnp.int32, (N_EXPERTS, tm), 0)
    el = jnp.where(erow // EXPERTS_PER_GROUP == g_idx, lt[SUBLANES:SUBLANES + N_EXPERTS], neg)
    m1 = jnp.max(el, axis=0, keepdims=True)
    i1 = jnp.min(jnp.where(el == m1, erow, N_EXPERTS), axis=0, keepdims=True)
    el2 = jnp.where(erow == i1, neg, el)
    m2 = jnp.max(el2, axis=0, keepdims=True)
    i2 = jnp.min(jnp.where(el2 == m2, erow, N_EXPERTS), axis=0, keepdims=True)
    d = jnp.exp(m2 - m1)
    w1 = g_w / (1.0 + d)
    w2 = g_w * d / (1.0 + d)
    zero = jnp.zeros((SUBLANES - 4, tm), F32)
    route_ref[...] = jnp.concatenate([i1.astype(F32), i2.astype(F32), w1, w2, zero], axis=0)
    chosen = ((erow == i1) | (erow == i2)).astype(F32)
    cnt_ref[...] = jnp.broadcast_to(jnp.sum(chosen, axis=1, keepdims=True), cnt_ref.shape)


def _router(h, norm_w, rg_w, rg_b, re_w, re_b):
    t, d = h.shape
    tm = min(ROW_TILE, t)
    gap = jnp.zeros((d, SUBLANES - N_GROUPS), F32)
    tail = jnp.zeros((d, LANES - SUBLANES - N_EXPERTS), F32)
    wr = jnp.concatenate([rg_w, gap, re_w, tail], axis=1)
    w_hi = wr.astype(BF16)
    w_lo = (wr - w_hi.astype(F32)).astype(BF16)
    br = jnp.concatenate([rg_b, gap[0], re_b, tail[0]]).reshape(1, LANES)
    return pl.pallas_call(
        _router_kernel, grid=(t // tm,),
        in_specs=[pl.BlockSpec((tm, d), lambda i: (i, 0)), _resident((1, d)),
                  _resident((d, 2 * LANES)), _resident((1, LANES))],
        out_specs=[pl.BlockSpec((tm * PACK_ROWS, LANES), lambda i: (i, 0)),
                   pl.BlockSpec((SUBLANES, tm), lambda i: (0, i)),
                   pl.BlockSpec((N_EXPERTS, LANES), lambda i: (i, 0))],
        out_shape=[jax.ShapeDtypeStruct((t * PACK_ROWS, LANES), jnp.uint32),
                   jax.ShapeDtypeStruct((SUBLANES, t), F32),
                   jax.ShapeDtypeStruct((t // tm * N_EXPERTS, LANES), F32)],
        compiler_params=_params(), name="moe_router",
    )(h, norm_w.reshape(1, d), jnp.concatenate([w_hi, w_lo], axis=1), br)


def _moe_kernel(st_ref, cnt_ref, tok_ref, wt_ref, xs_ref, wgu_ref, wd_ref, o_ref, tile_ref, ybuf_ref):
    b = pl.program_id(0)
    e = pl.program_id(1)
    stride = MOE_STRIDE
    nslab = ybuf_ref.shape[0] // stride
    unroll = SUBLANES

    @pl.when((b == 0) & (e == 0))
    def _():
        tile_ref[...] = jnp.zeros_like(tile_ref)

    @pl.when(e == 0)
    def _():
        o_ref[...] = jnp.zeros_like(o_ref)

    start = st_ref[b * N_EXPERTS + e]
    count = cnt_ref[b * N_EXPERTS + e]

    def expert_mlp(rows):
        def run():
            lo, hi = [], []
            for j in range(PACK_ROWS):
                word = tile_ref[pl.ds(j * stride, rows), :]
                lo.append(lax.bitcast_convert_type(word << 16, F32))
                hi.append(lax.bitcast_convert_type(word & jnp.uint32(0xFFFF0000), F32))
            x = jnp.concatenate(lo + hi, axis=-1).astype(BF16)
            gu = jnp.dot(x, wgu_ref[...], preferred_element_type=F32)
            gate, up = gu[:, :D_EXPERT], gu[:, D_EXPERT:]
            he = (gate * _sigmoid(gate) * up).astype(BF16)
            y = jnp.dot(he, wd_ref[...], preferred_element_type=F32)
            for j in range(nslab):
                ybuf_ref[pl.ds(j * stride, rows), :] = y[:, j * LANES:(j + 1) * LANES]
        return run

    def tile_body(t, carry):
        base = start + t * MOE_ROWS
        nvalid = jnp.minimum(count - t * MOE_ROWS, MOE_ROWS)

        def gather(i, c):
            for u in range(unroll):
                r = i * unroll + u
                tok = tok_ref[0, 0, base + r]
                src = pl.ds(pl.multiple_of(tok * PACK_ROWS, PACK_ROWS), PACK_ROWS)
                tile_ref[pl.ds(r, PACK_ROWS, stride=stride), :] = xs_ref[src, :]
            return c

        lax.fori_loop(0, (nvalid + unroll - 1) // unroll, gather, 0)
        size_class = (nvalid + MOE_UNIT - 1) // MOE_UNIT - 1
        lax.switch(size_class, [expert_mlp(MOE_UNIT * (k + 1)) for k in range(MOE_CLASSES)])

        def contribution(r):
            tok = tok_ref[0, 0, base + r]
            off = pl.multiple_of(tok * SUBLANES, SUBLANES)
            slab = ybuf_ref[pl.ds(r, nslab, stride=stride), :] * wt_ref[0, 0, base + r]
            return off, o_ref[pl.ds(off, SUBLANES), :] + slab

        def scatter_group(i, c):
            done = [contribution(i * unroll + u) for u in range(unroll)]
            for off, val in done:
                o_ref[pl.ds(off, SUBLANES), :] = val
            return c

        def scatter_one(r, c):
            off, val = contribution(r)
            o_ref[pl.ds(off, SUBLANES), :] = val
            return c

        full = nvalid // unroll
        lax.fori_loop(0, full, scatter_group, 0)
        lax.fori_loop(full * unroll, nvalid, scatter_one, 0)
        return carry

    lax.fori_loop(0, (count + MOE_ROWS - 1) // MOE_ROWS, tile_body, 0)


def _moe(xt, route, tile_counts, w_gu, w_down, t, d):
    m = min(MOE_BLOCK, t)
    nb = t // m
    pick = lambda r: route[r].reshape(nb, m)
    e_idx = jnp.concatenate([pick(0), pick(1)], axis=1).astype(jnp.int32)
    wts = jnp.concatenate([pick(2), pick(3)], axis=1)
    order = jnp.argsort(e_idx, axis=1, stable=True)
    tok_sorted = (order % m).astype(jnp.int32)
    w_sorted = jnp.take_along_axis(wts, order, axis=1)
    per_tile = tile_counts[:, 0].reshape(nb, -1, N_EXPERTS)
    counts = per_tile.sum(axis=1).astype(jnp.int32)
    starts = jnp.cumsum(counts, axis=1) - counts
    length = 2 * m + MOE_ROWS
    tok_sorted = jnp.pad(tok_sorted, ((0, 0), (0, MOE_ROWS))).reshape(nb, 1, length)
    w_sorted = jnp.pad(w_sorted, ((0, 0), (0, MOE_ROWS))).reshape(nb, 1, length)
    nslab = d // LANES
    grid_spec = pltpu.PrefetchScalarGridSpec(
        num_scalar_prefetch=2, grid=(nb, N_EXPERTS),
        in_specs=[pl.BlockSpec((1, 1, length), lambda b, e, *_: (b, 0, 0), memory_space=pltpu.SMEM),
                  pl.BlockSpec((1, 1, length), lambda b, e, *_: (b, 0, 0), memory_space=pltpu.SMEM),
                  pl.BlockSpec((m * PACK_ROWS, LANES), lambda b, e, *_: (b, 0), pipeline_mode=pl.Buffered(1)),
                  pl.BlockSpec((None, d, 2 * D_EXPERT), lambda b, e, *_: (e, 0, 0)),
                  pl.BlockSpec((None, D_EXPERT, d), lambda b, e, *_: (e, 0, 0))],
        out_specs=pl.BlockSpec((m * SUBLANES, LANES), lambda b, e, *_: (b, 0)),
        scratch_shapes=[pltpu.VMEM((PACK_ROWS * MOE_STRIDE, LANES), jnp.uint32),
                        pltpu.VMEM((nslab * MOE_STRIDE, LANES), F32)])
    return pl.pallas_call(
        _moe_kernel, grid_spec=grid_spec,
        out_shape=jax.ShapeDtypeStruct((t * SUBLANES, LANES), F32),
        compiler_params=_params(), name="moe_experts",
    )(starts.reshape(-1), counts.reshape(-1), tok_sorted, w_sorted, xt, w_gu, w_down)


def _combine_kernel(h_ref, mt_ref, nw_ref, o_ref, *, final_norm):
    tm, d = h_ref.shape
    parts = [mt_ref[pl.ds(j, tm, stride=SUBLANES), :] for j in range(d // LANES)]
    out = h_ref[...] + jnp.concatenate(parts, axis=-1)
    if final_norm:
        out = _rms(out, nw_ref[...])
    o_ref[...] = out


def _combine(h, moe_t, norm_w, final_norm, name):
    t, d = h.shape
    tm = min(ROW_TILE, t)
    return pl.pallas_call(
        functools.partial(_combine_kernel, final_norm=final_norm), grid=(t // tm,),
        in_specs=[pl.BlockSpec((tm, d), lambda i: (i, 0)),
                  pl.BlockSpec((tm * SUBLANES, LANES), lambda i: (i, 0)),
                  _resident((1, d))],
        out_specs=pl.BlockSpec((tm, d), lambda i: (i, 0)),
        out_shape=jax.ShapeDtypeStruct((t, d), F32),
        compiler_params=_params(), name=name,
    )(h, moe_t, norm_w.reshape(1, d))


def _moe_layer(h, norm_w, rg_w, rg_b, re_w, re_b, w_gate, w_up, w_down, out_norm_w, final_norm, name):
    t, d = h.shape
    xt, route, tile_counts = _router(h, norm_w, rg_w, rg_b, re_w, re_b)
    w_gu = jnp.concatenate([w_gate, w_up], axis=-1).astype(BF16)
    moe_t = _moe(xt, route, tile_counts, w_gu, w_down.astype(BF16), t, d)
    return _combine(h, moe_t, out_norm_w, final_norm, name)


def kernel(x, mem, a_norm_w, a_in_w, a_conv_w, a_conv_b, a_dt_bias, a_a_log, a_d_skip, a_gnorm_w, a_out_w, kv_norm_w, w_kv, b_norm_w, b_in_w, b_sinks, b_out_w, mem_norm_w, mem_w_kv, ffn_norm_w, router_group_w, router_group_b, router_expert_w, router_expert_b, w_gate, w_up, w_down, final_norm_w):
    bsz, seq, d = x.shape
    mem_len = mem.shape[1]
    t = bsz * seq
    h = x.reshape(t, d)
    mem2 = mem.reshape(bsz * mem_len, d)
    conv_ch = a_conv_w.shape[-1]
    n_a = a_in_w.shape[0]
    n_layers = n_a + b_in_w.shape[0]
    kv_sh = None
    for layer in range(n_layers):
        (mem_kv,) = _norm_matmul(mem2, mem_norm_w[layer], [mem_w_kv[layer].astype(BF16)], [BF16],
                                 name=f"mem_kv_{layer}")
        if layer < n_a:
            i = layer
            w_in = a_in_w[i]
            o_x, o_dt, o_q = SSD_INNER, SSD_INNER + conv_ch, SSD_INNER + conv_ch + SSD_HEADS
            w_dt = jnp.pad(w_in[:, o_dt:o_q], ((0, 0), (0, LANES - SSD_HEADS)))
            z, xbc, q_mem, dt = _norm_matmul(
                h, a_norm_w[i],
                [w_in[:, :o_x].astype(BF16), w_in[:, o_x:o_dt].astype(BF16),
                 w_in[:, o_q:].astype(BF16), w_dt.astype(BF16)],
                [BF16, BF16, BF16, F32], name=f"ssd_in_proj_{i}")
            y = _ssd(xbc, z, dt, a_conv_w[i], a_conv_b[i], a_dt_bias[i], a_a_log[i], a_d_skip[i],
                     a_gnorm_w[i], bsz, seq)
            m_att = _mem_attn(q_mem, mem_kv, bsz, seq, mem_len, name=f"mem_attn_{layer}")
            w_out = a_out_w[i].astype(BF16)
            h = _mix_out(y, m_att, w_out[:SSD_INNER], w_out[SSD_INNER:], h, name=f"ssd_out_proj_{i}")
        else:
            j = layer - n_a
            if j == 0:
                (kv_sh,) = _norm_matmul(h, kv_norm_w, [w_kv.astype(BF16)], [BF16], name="shared_kv")
            qw = Q_HEADS * HEAD_DIM
            w_in = b_in_w[j].astype(BF16)
            q, q_mem = _norm_matmul(h, b_norm_w[j], [w_in[:, :qw], w_in[:, qw:]], [BF16, BF16],
                                    name=f"swa_in_proj_{j}")
            att = _swa(q, kv_sh, b_sinks[j], bsz, seq)
            m_att = _mem_attn(q_mem, mem_kv, bsz, seq, mem_len, name=f"mem_attn_{layer}")
            w_out = b_out_w[j].astype(BF16)
            h = _mix_out(att, m_att, w_out[:qw], w_out[qw:], h, name=f"swa_out_proj_{j}")
        last = layer == n_layers - 1
        h = _moe_layer(h, ffn_norm_w[layer], router_group_w[layer], router_group_b[layer],
                       router_expert_w[layer], router_expert_b[layer],
                       w_gate[layer], w_up[layer], w_down[layer],
                       final_norm_w, last, name=f"moe_combine_{layer}")
    return h.reshape(bsz, seq, d)
```

```python
import functools

import jax
import jax.numpy as jnp
from jax import lax
from jax.experimental import pallas as pl
from jax.experimental.pallas import tpu as pltpu

F32 = jnp.float32
BF16 = jnp.bfloat16
NORM_EPS = 1e-6

LANES = 128
SUBLANES = 8
VMEM_LIMIT = 56 * 1024 * 1024

MEM_HEADS = 4
MEM_HEAD_DIM = 128
MEM_WIDTH = MEM_HEADS * MEM_HEAD_DIM
SSD_HEAD_DIM = 64
SSD_HEADS = 24
SSD_INNER = SSD_HEADS * SSD_HEAD_DIM
SSD_GROUPS = 4
SSD_STATE = 128
SSD_CHUNK = 128
CONV_WIDTH = 4
HEAD_DIM = 64
Q_HEADS = 16
KV_HEADS = 4
ATTN_BLOCK = 128
ROT_DIM = 16
ROPE_THETA = 500000.0
N_GROUPS = 4
EXPERTS_PER_GROUP = 8
N_EXPERTS = 32
D_EXPERT = 512

ROW_TILE = 512
MOE_BLOCK = 4096
MOE_UNIT = 64
MOE_CLASSES = 8
MOE_ROWS = MOE_UNIT * MOE_CLASSES
MOE_STRIDE = MOE_ROWS + SUBLANES
PACK = 2
PACK_ROWS = SUBLANES // PACK
HIGH_HALF = 0xFFFF0000


def _params():
    return pltpu.CompilerParams(vmem_limit_bytes=VMEM_LIMIT)


def _resident(shape):
    nd = len(shape)
    return pl.BlockSpec(shape, lambda *_: (0,) * nd, pipeline_mode=pl.Buffered(1))


def _rms(x, w):
    ms = jnp.mean(x * x, axis=-1, keepdims=True)
    return x * lax.rsqrt(ms + NORM_EPS) * w


def _sigmoid(x):
    return 1.0 / (1.0 + jnp.exp(-x))


def _pack_pairs(x):
    half = x.shape[1] // PACK
    bits = lax.bitcast_convert_type(x.astype(BF16).astype(F32), jnp.uint32)
    return [(bits[:, j * LANES:(j + 1) * LANES] >> 16)
            | (bits[:, half + j * LANES:half + (j + 1) * LANES] & jnp.uint32(HIGH_HALF))
            for j in range(half // LANES)]


def _unpack_pairs(words):
    lo = [lax.bitcast_convert_type(w << 16, F32) for w in words]
    hi = [lax.bitcast_convert_type(w & jnp.uint32(HIGH_HALF), F32) for w in words]
    return jnp.concatenate(lo + hi, axis=-1)


def _norm_matmul_kernel(x_ref, nw_ref, *refs, n_w):
    w_refs, o_refs = refs[:n_w], refs[n_w:]
    xn = _rms(x_ref[...], nw_ref[...]).astype(BF16)
    for w_ref, o_ref in zip(w_refs, o_refs):
        n = w_ref.shape[1]
        for c0 in range(0, n, 512):
            cw = min(512, n - c0)
            acc = jnp.dot(xn, w_ref[:, c0:c0 + cw], preferred_element_type=F32)
            o_ref[:, c0:c0 + cw] = acc.astype(o_ref.dtype)


def _norm_matmul(x, norm_w, weights, out_dtypes, name):
    t, d = x.shape
    tm = min(ROW_TILE, t)
    in_specs = [pl.BlockSpec((tm, d), lambda i: (i, 0)), _resident((1, d))]
    in_specs += [_resident(w.shape) for w in weights]
    out_specs = [pl.BlockSpec((tm, w.shape[1]), lambda i: (i, 0)) for w in weights]
    out_shape = [jax.ShapeDtypeStruct((t, w.shape[1]), dt) for w, dt in zip(weights, out_dtypes)]
    return pl.pallas_call(
        functools.partial(_norm_matmul_kernel, n_w=len(weights)),
        grid=(t // tm,), in_specs=in_specs, out_specs=out_specs, out_shape=out_shape,
        compiler_params=_params(), name=name,
    )(x, norm_w.reshape(1, d), *weights)


def _mix_out_kernel(a1_ref, a2_ref, w1_ref, w2_ref, r_ref, o_ref):
    acc = jnp.dot(a1_ref[...], w1_ref[...], preferred_element_type=F32)
    acc = acc + jnp.dot(a2_ref[...], w2_ref[...], preferred_element_type=F32)
    o_ref[...] = r_ref[...] + acc


def _mix_out(a1, a2, w1, w2, res, name):
    t, d = res.shape
    tm = min(ROW_TILE, t)
    return pl.pallas_call(
        _mix_out_kernel, grid=(t // tm,),
        in_specs=[pl.BlockSpec((tm, a1.shape[1]), lambda i: (i, 0)),
                  pl.BlockSpec((tm, a2.shape[1]), lambda i: (i, 0)),
                  _resident(w1.shape), _resident(w2.shape),
                  pl.BlockSpec((tm, d), lambda i: (i, 0))],
        out_specs=pl.BlockSpec((tm, d), lambda i: (i, 0)),
        out_shape=jax.ShapeDtypeStruct((t, d), F32),
        compiler_params=_params(), name=name,
    )(a1, a2, w1, w2, res)


def _mem_attn_kernel(q_ref, kv_ref, o_ref):
    scale = MEM_HEAD_DIM ** -0.5
    outs = []
    for h in range(MEM_HEADS):
        lo = h * MEM_HEAD_DIM
        q = q_ref[:, lo:lo + MEM_HEAD_DIM]
        k = kv_ref[:, lo:lo + MEM_HEAD_DIM]
        v = kv_ref[:, MEM_WIDTH + lo:MEM_WIDTH + lo + MEM_HEAD_DIM]
        s = lax.dot_general(q, k, (((1,), (1,)), ((), ())), preferred_element_type=F32) * scale
        p = jnp.exp(s - jnp.max(s, axis=-1, keepdims=True))
        p = p / jnp.sum(p, axis=-1, keepdims=True)
        outs.append(jnp.dot(p.astype(BF16), v, preferred_element_type=F32))
    o_ref[...] = jnp.concatenate(outs, axis=-1).astype(o_ref.dtype)


def _mem_attn(q, kv, bsz, seq, mem_len, name):
    tq = min(ROW_TILE, seq)
    nq = seq // tq
    return pl.pallas_call(
        _mem_attn_kernel, grid=(bsz, nq),
        in_specs=[pl.BlockSpec((tq, MEM_WIDTH), lambda b, i: (b * nq + i, 0)),
                  pl.BlockSpec((mem_len, 2 * MEM_WIDTH), lambda b, i: (b, 0))],
        out_specs=pl.BlockSpec((tq, MEM_WIDTH), lambda b, i: (b * nq + i, 0)),
        out_shape=jax.ShapeDtypeStruct((bsz * seq, MEM_WIDTH), BF16),
        compiler_params=_params(), name=name,
    )(q, kv)


def _ssd_kernel(xbc_ref, z_ref, dt_ref, cw_ref, cb_ref, dtb_ref, alog_ref, dsk_ref, gnw_ref,
                expand_ref, o_ref, ext_ref, st_ref):
    c = pl.program_id(1)
    L = SSD_CHUNK
    gw = SSD_INNER // SSD_GROUPS
    hpg = SSD_HEADS // SSD_GROUPS

    @pl.when(c == 0)
    def _():
        ext_ref[0:SUBLANES, :] = jnp.zeros((SUBLANES, ext_ref.shape[1]), F32)
        st_ref[...] = jnp.zeros_like(st_ref)

    @pl.when(c > 0)
    def _():
        ext_ref[0:SUBLANES, :] = ext_ref[L:L + SUBLANES, :]

    ext_ref[SUBLANES:SUBLANES + L, :] = xbc_ref[...].astype(F32)
    conv = cb_ref[...]
    for k in range(CONV_WIDTH):
        conv = conv + cw_ref[k:k + 1, :] * ext_ref[pl.ds(SUBLANES - (CONV_WIDTH - 1) + k, L), :]
    xbc = conv * _sigmoid(conv)
    xs = xbc[:, :SSD_INNER]
    bm = xbc[:, SSD_INNER:SSD_INNER + SSD_GROUPS * SSD_STATE]
    cm = xbc[:, SSD_INNER + SSD_GROUPS * SSD_STATE:].astype(BF16)

    pre = dt_ref[...] + dtb_ref[...]
    dt = jnp.maximum(pre, 0.0) + jnp.log(1.0 + jnp.exp(-jnp.abs(pre)))
    a = -jnp.exp(alog_ref[...])
    adt = dt * a
    row = lax.broadcasted_iota(jnp.int32, (L, L), 0)
    col = lax.broadcasted_iota(jnp.int32, (L, L), 1)
    causal = row >= col
    acum = jnp.dot(causal.astype(F32), adt, preferred_element_type=F32,
                   precision=lax.Precision.HIGHEST)
    acum_t = acum.T
    ea = jnp.exp(acum)
    te = jnp.exp(acum[L - 1:L, :] - acum)
    widen = lambda cols: jnp.dot(cols.astype(BF16), expand_ref[...], preferred_element_type=F32)
    dtx, eax, tex = widen(dt), widen(ea), widen(te)
    xdt = xs * dtx
    xdt_b = xdt.astype(BF16)
    xw = (xdt * tex).astype(BF16)

    first_half = col < SSD_HEAD_DIM
    ys = []
    for g in range(SSD_GROUPS):
        cg = cm[:, g * SSD_STATE:(g + 1) * SSD_STATE]
        bg_t = bm[:, g * SSD_STATE:(g + 1) * SSD_STATE].T.astype(BF16)
        cb = jnp.dot(cg, bg_t, preferred_element_type=F32)
        state = st_ref[g]
        y_off = jnp.dot(cg, state.astype(BF16), preferred_element_type=F32) * eax[:, g * gw:(g + 1) * gw]
        parts = []
        for r in range(0, hpg, 2):
            both = []
            for h in (g * hpg + r, g * hpg + r + 1):
                seg = acum[:, h:h + 1] - acum_t[h:h + 1, :]
                m = (cb * jnp.exp(jnp.where(causal, seg, -jnp.inf))).astype(BF16)
                lo = (g * hpg + r) * SSD_HEAD_DIM
                both.append(jnp.dot(m, xdt_b[:, lo:lo + LANES], preferred_element_type=F32))
            parts.append(jnp.where(first_half, both[0], both[1]))
        ys.append(jnp.concatenate(parts, axis=-1) + y_off)
        upd = jnp.dot(bg_t, xw[:, g * gw:(g + 1) * gw], preferred_element_type=F32)
        st_ref[g] = state * eax[L - 1:L, g * gw:(g + 1) * gw] + upd
    y = jnp.concatenate(ys, axis=-1) + dsk_ref[...] * xs
    zf = z_ref[...].astype(F32)
    y = y * (zf * _sigmoid(zf))
    normed = []
    for g in range(SSD_GROUPS):
        yg = y[:, g * gw:(g + 1) * gw]
        normed.append(yg * lax.rsqrt(jnp.mean(yg * yg, axis=-1, keepdims=True) + NORM_EPS))
    o_ref[...] = (jnp.concatenate(normed, axis=-1) * gnw_ref[...]).astype(o_ref.dtype)


def _ssd(xbc, z, dt, conv_w, conv_b, dt_bias, a_log, d_skip, gnorm_w, bsz, seq):
    nc = seq // SSD_CHUNK
    conv_ch = xbc.shape[1]
    pad = LANES - SSD_HEADS
    dtb = jnp.pad(dt_bias, (0, pad)).reshape(1, LANES)
    alog = jnp.pad(a_log, (0, pad)).reshape(1, LANES)
    dsk = jnp.repeat(d_skip, SSD_HEAD_DIM).reshape(1, SSD_INNER)
    expand = (jnp.arange(LANES)[:, None] == (jnp.arange(SSD_INNER) // SSD_HEAD_DIM)[None, :]).astype(BF16)
    blk = lambda w: pl.BlockSpec((SSD_CHUNK, w), lambda b, c: (b * nc + c, 0))
    return pl.pallas_call(
        _ssd_kernel, grid=(bsz, nc),
        in_specs=[blk(conv_ch), blk(SSD_INNER), blk(LANES),
                  _resident((CONV_WIDTH, conv_ch)), _resident((1, conv_ch)),
                  _resident((1, LANES)), _resident((1, LANES)),
                  _resident((1, SSD_INNER)), _resident((1, SSD_INNER)),
                  _resident((LANES, SSD_INNER))],
        out_specs=blk(SSD_INNER),
        out_shape=jax.ShapeDtypeStruct((bsz * seq, SSD_INNER), BF16),
        scratch_shapes=[pltpu.VMEM((SSD_CHUNK + 2 * SUBLANES, conv_ch), F32),
                        pltpu.VMEM((SSD_GROUPS, SSD_STATE, SSD_INNER // SSD_GROUPS), F32)],
        compiler_params=_params(), name="ssd_mixer",
    )(xbc, z, dt, conv_w, conv_b.reshape(1, conv_ch), dtb, alog, dsk,
      gnorm_w.reshape(1, SSD_INNER), expand)


def _rope(x, cos, sin_up, sin_dn):
    w = x.shape[1]
    reps = w // LANES
    tile = lambda t: jnp.concatenate([t] * reps, axis=-1) if reps > 1 else t
    half = ROT_DIM // 2
    return (x * tile(cos) + pltpu.roll(x, half, axis=1) * tile(sin_up)
            + pltpu.roll(x, w - half, axis=1) * tile(sin_dn))


def _swa_kernel(sink_ref, q_ref, kvp_ref, kvc_ref, tc_ref, tp_ref, bias_ref, o_ref):
    kvw = KV_HEADS * HEAD_DIM
    g = Q_HEADS // KV_HEADS
    scale = HEAD_DIM ** -0.5
    q = _rope(q_ref[...].astype(F32), tc_ref[0], tc_ref[1], tc_ref[2]) * scale
    q_t = q.T.astype(BF16)
    k_prev = _rope(kvp_ref[:, :kvw].astype(F32), tp_ref[0], tp_ref[1], tp_ref[2])
    k_cur = _rope(kvc_ref[:, :kvw].astype(F32), tc_ref[0], tc_ref[1], tc_ref[2])
    k = jnp.concatenate([k_prev, k_cur], axis=0).astype(BF16)
    v = jnp.concatenate([kvp_ref[:, kvw:], kvc_ref[:, kvw:]], axis=0).astype(F32)
    v_t = v.T.astype(BF16)
    bias = bias_ref[0]
    outs = []
    for kh in range(KV_HEADS):
        kk = k[:, kh * HEAD_DIM:(kh + 1) * HEAD_DIM]
        sc = jnp.concatenate(
            [jnp.dot(kk, q_t[h * HEAD_DIM:(h + 1) * HEAD_DIM, :], preferred_element_type=F32)
             for h in range(kh * g, (kh + 1) * g)], axis=-1) + bias
        sink = jnp.concatenate([jnp.full((1, ATTN_BLOCK), sink_ref[h], F32)
                                for h in range(kh * g, (kh + 1) * g)], axis=-1)
        mx = jnp.maximum(jnp.max(sc, axis=0, keepdims=True), sink)
        p = jnp.exp(sc - mx)
        denom = jnp.sum(p, axis=0, keepdims=True) + jnp.exp(sink - mx)
        o_t = jnp.dot(v_t[kh * HEAD_DIM:(kh + 1) * HEAD_DIM, :], p.astype(BF16),
                      preferred_element_type=F32) / denom
        outs += [o_t[:, j * ATTN_BLOCK:(j + 1) * ATTN_BLOCK] for j in range(g)]
    o_ref[...] = jnp.concatenate(outs, axis=0).T.astype(o_ref.dtype)


def _swa_bias(g):
    blk = ATTN_BLOCK
    s_idx = jnp.arange(2 * blk)[:, None]
    t_idx = jnp.arange(blk)[None, :]
    band = (s_idx > t_idx) & (s_idx <= t_idx + blk)
    first = band & (s_idx >= blk)
    both = jnp.stack([first, band], axis=0)
    return jnp.tile(jnp.where(both, 0.0, -jnp.inf).astype(F32), (1, 1, g))


def _rope_tables(seq):
    half = ROT_DIM // 2
    inv_freq = ROPE_THETA ** (-2.0 * jnp.arange(half, dtype=F32) / ROT_DIM)
    ang = jnp.arange(seq, dtype=jnp.int32).astype(F32)[:, None] * inv_freq[None, :]
    cos, sin = jnp.cos(ang), jnp.sin(ang)
    ones = jnp.ones((seq, HEAD_DIM - ROT_DIM), F32)
    zeros = jnp.zeros((seq, HEAD_DIM - ROT_DIM), F32)
    zh = jnp.zeros((seq, half), F32)
    cos_h = jnp.concatenate([cos, cos, ones], axis=-1)
    up_h = jnp.concatenate([zh, sin, zeros], axis=-1)
    dn_h = jnp.concatenate([-sin, zh, zeros], axis=-1)
    two = lambda t: jnp.concatenate([t, t], axis=-1)
    return jnp.stack([two(cos_h), two(up_h), two(dn_h)], axis=0)


def _swa(q, kv, sinks, bsz, seq):
    nb = seq // ATTN_BLOCK
    tables = _rope_tables(seq)
    qw = Q_HEADS * HEAD_DIM
    kvw2 = 2 * KV_HEADS * HEAD_DIM
    prev = lambda i: jnp.maximum(i - 1, 0)
    g = Q_HEADS // KV_HEADS
    return pl.pallas_call(
        _swa_kernel, grid=(bsz, nb),
        in_specs=[pl.BlockSpec(memory_space=pltpu.SMEM),
                  pl.BlockSpec((ATTN_BLOCK, qw), lambda b, i: (b * nb + i, 0)),
                  pl.BlockSpec((ATTN_BLOCK, kvw2), lambda b, i: (b * nb + prev(i), 0)),
                  pl.BlockSpec((ATTN_BLOCK, kvw2), lambda b, i: (b * nb + i, 0)),
                  pl.BlockSpec((3, ATTN_BLOCK, LANES), lambda b, i: (0, i, 0)),
                  pl.BlockSpec((3, ATTN_BLOCK, LANES), lambda b, i: (0, prev(i), 0)),
                  pl.BlockSpec((1, 2 * ATTN_BLOCK, g * ATTN_BLOCK), lambda b, i: (jnp.minimum(i, 1), 0, 0))],
        out_specs=pl.BlockSpec((ATTN_BLOCK, qw), lambda b, i: (b * nb + i, 0)),
        out_shape=jax.ShapeDtypeStruct((bsz * seq, qw), BF16),
        compiler_params=_params(), name="swa_attention",
    )(sinks, q, kv, kv, tables, tables, _swa_bias(g))


def _router_kernel(h_ref, nw_ref, wr_ref, br_ref, xt_ref, route_ref, cnt_ref):
    tm = h_ref.shape[0]
    xn = _rms(h_ref[...], nw_ref[...])
    x_hi = xn.astype(BF16)
    for j, word in enumerate(_pack_pairs(xn)):
        xt_ref[pl.ds(j, tm, stride=PACK_ROWS), :] = word
    x_lo = (xn - x_hi.astype(F32)).astype(BF16)
    both = jnp.dot(x_hi, wr_ref[...], preferred_element_type=F32)
    logits = (both[:, :LANES] + both[:, LANES:]
              + jnp.dot(x_lo, wr_ref[:, :LANES], preferred_element_type=F32) + br_ref[...])
    lt = logits.T
    neg = -jnp.inf
    grow = lax.broadcasted_iota(jnp.int32, (SUBLANES, tm), 0)
    gl = jnp.where(grow < N_GROUPS, lt[:SUBLANES], neg)
    gmax = jnp.max(gl, axis=0, keepdims=True)
    g_w = 1.0 / jnp.sum(jnp.exp(gl - gmax), axis=0, keepdims=True)
    g_idx = jnp.min(jnp.where(gl == gmax, grow, SUBLANES), axis=0, keepdims=True)
    erow = lax.broadcasted_iota(jnp.int32, (N_EXPERTS, tm), 0)
    el = jnp.where(erow // EXPERTS_PER_GROUP == g_idx, lt[SUBLANES:SUBLANES + N_EXPERTS], neg)
    m1 = jnp.max(el, axis=0, keepdims=True)
    i1 = jnp.min(jnp.where(el == m1, erow, N_EXPERTS), axis=0, keepdims=True)
    el2 = jnp.where(erow == i1, neg, el)
    m2 = jnp.max(el2, axis=0, keepdims=True)
    i2 = jnp.min(jnp.where(el2 == m2, erow, N_EXPERTS), axis=0, keepdims=True)
    d = jnp.exp(m2 - m1)
    w1 = g_w / (1.0 + d)
    w2 = g_w * d / (1.0 + d)
    zero = jnp.zeros((SUBLANES - 4, tm), F32)
    route_ref[...] = jnp.concatenate([i1.astype(F32), i2.astype(F32), w1, w2, zero], axis=0)
    chosen = ((erow == i1) | (erow == i2)).astype(F32)
    cnt_ref[...] = jnp.broadcast_to(jnp.sum(chosen, axis=1, keepdims=True), cnt_ref.shape)


def _router(h, norm_w, rg_w, rg_b, re_w, re_b):
    t, d = h.shape
    tm = min(ROW_TILE, t)
    gap = jnp.zeros((d, SUBLANES - N_GROUPS), F32)
    tail = jnp.zeros((d, LANES - SUBLANES - N_EXPERTS), F32)
    wr = jnp.concatenate([rg_w, gap, re_w, tail], axis=1)
    w_hi = wr.astype(BF16)
    w_lo = (wr - w_hi.astype(F32)).astype(BF16)
    br = jnp.concatenate([rg_b, gap[0], re_b, tail[0]]).reshape(1, LANES)
    return pl.pallas_call(
        _router_kernel, grid=(t // tm,),
        in_specs=[pl.BlockSpec((tm, d), lambda i: (i, 0)), _resident((1, d)),
                  _resident((d, 2 * LANES)), _resident((1, LANES))],
        out_specs=[pl.BlockSpec((tm * PACK_ROWS, LANES), lambda i: (i, 0)),
                   pl.BlockSpec((SUBLANES, tm), lambda i: (0, i)),
                   pl.BlockSpec((N_EXPERTS, LANES), lambda i: (i, 0))],
        out_shape=[jax.ShapeDtypeStruct((t * PACK_ROWS, LANES), jnp.uint32),
                   jax.ShapeDtypeStruct((SUBLANES, t), F32),
                   jax.ShapeDtypeStruct((t // tm * N_EXPERTS, LANES), F32)],
        compiler_params=_params(), name="moe_router",
    )(h, norm_w.reshape(1, d), jnp.concatenate([w_hi, w_lo], axis=1), br)


def _moe_kernel(st_ref, cnt_ref, tok_ref, dst_ref, xs_ref, wgu_ref, wd_ref, o_ref,
                tile_ref, xb_ref, ybuf_ref, ready_ref):
    b = pl.program_id(0)
    e = pl.program_id(1)
    stride = MOE_STRIDE
    unroll = SUBLANES

    @pl.when((b == 0) & (e == 0))
    def _():
        tile_ref[...] = jnp.zeros_like(tile_ref)

    @pl.when(e == 0)
    def _():
        ready_ref[0] = 0

    start = st_ref[b * N_EXPERTS + e]
    count = cnt_ref[b * N_EXPERTS + e]

    def gather_row(base, r):
        tok = tok_ref[0, 0, base + r]
        src = pl.ds(pl.multiple_of(tok * PACK_ROWS, PACK_ROWS), PACK_ROWS)
        tile_ref[pl.ds(r, PACK_ROWS, stride=stride), :] = xs_ref[src, :]

    def scatter_row(base, r):
        dst = dst_ref[0, 0, base + r]
        o_ref[pl.ds(pl.multiple_of(dst * PACK_ROWS, PACK_ROWS), PACK_ROWS), :] = (
            ybuf_ref[pl.ds(r, PACK_ROWS, stride=stride), :])

    def expert_mlp(rows, next_base):
        def run():
            words = [tile_ref[pl.ds(j * stride, rows), :] for j in range(PACK_ROWS)]
            xb_ref[0:rows, :] = _unpack_pairs(words).astype(BF16)
            for r in range(rows):
                gather_row(next_base, r)
            gu = jnp.dot(xb_ref[0:rows, :], wgu_ref[...], preferred_element_type=F32)
            gate, up = gu[:, :D_EXPERT], gu[:, D_EXPERT:]
            he = (gate * _sigmoid(gate) * up).astype(BF16)
            y = jnp.dot(he, wd_ref[...], preferred_element_type=F32)
            for j, word in enumerate(_pack_pairs(y)):
                ybuf_ref[pl.ds(j * stride, rows), :] = word
        return run

    def tile_body(t, carry):
        base = start + t * MOE_ROWS
        nvalid = jnp.minimum(count - t * MOE_ROWS, MOE_ROWS)
        have = jnp.where(t == 0, ready_ref[0], 0)

        def gather(i, c):
            for u in range(unroll):
                gather_row(base, i * unroll + u)
            return c

        lax.fori_loop(have // unroll, (nvalid + unroll - 1) // unroll, gather, 0)
        size_class = (nvalid + MOE_UNIT - 1) // MOE_UNIT - 1
        next_base = start + count
        lax.switch(size_class, [expert_mlp(MOE_UNIT * (k + 1), next_base) for k in range(MOE_CLASSES)])
        ready_ref[0] = (size_class + 1) * MOE_UNIT

        def scatter(i, c):
            for u in range(unroll):
                scatter_row(base, i * unroll + u)
            return c

        def scatter_one(r, c):
            scatter_row(base, r)
            return c

        full = nvalid // unroll
        lax.fori_loop(0, full, scatter, 0)
        lax.fori_loop(full * unroll, nvalid, scatter_one, 0)
        return carry

    ntiles = (count + MOE_ROWS - 1) // MOE_ROWS
    lax.fori_loop(0, ntiles, tile_body, 0)

    @pl.when(ntiles == 0)
    def _():
        ready_ref[0] = 0


def _moe(xt, route, tile_counts, w_gu, w_down, t, d):
    m = min(MOE_BLOCK, t)
    nb = t // m
    pick = lambda r: route[r].reshape(nb, m)
    e_idx = jnp.concatenate([pick(0), pick(1)], axis=1).astype(jnp.int32)
    order = jnp.argsort(e_idx, axis=1, stable=True).astype(jnp.int32)
    per_tile = tile_counts[:, 0].reshape(nb, -1, N_EXPERTS)
    counts = per_tile.sum(axis=1).astype(jnp.int32)
    starts = jnp.cumsum(counts, axis=1) - counts
    length = 2 * m + MOE_ROWS
    pad = lambda a: jnp.pad(a, ((0, 0), (0, MOE_ROWS))).reshape(nb, 1, length)
    grid_spec = pltpu.PrefetchScalarGridSpec(
        num_scalar_prefetch=2, grid=(nb, N_EXPERTS),
        in_specs=[pl.BlockSpec((1, 1, length), lambda b, e, *_: (b, 0, 0), memory_space=pltpu.SMEM),
                  pl.BlockSpec((1, 1, length), lambda b, e, *_: (b, 0, 0), memory_space=pltpu.SMEM),
                  pl.BlockSpec((m * PACK_ROWS, LANES), lambda b, e, *_: (b, 0), pipeline_mode=pl.Buffered(1)),
                  pl.BlockSpec((None, d, 2 * D_EXPERT), lambda b, e, *_: (e, 0, 0)),
                  pl.BlockSpec((None, D_EXPERT, d), lambda b, e, *_: (e, 0, 0))],
        out_specs=pl.BlockSpec((2 * m * PACK_ROWS, LANES), lambda b, e, *_: (b, 0)),
        scratch_shapes=[pltpu.VMEM((PACK_ROWS * MOE_STRIDE, LANES), jnp.uint32),
                        pltpu.VMEM((MOE_ROWS, d), BF16),
                        pltpu.VMEM((PACK_ROWS * MOE_STRIDE, LANES), jnp.uint32),
                        pltpu.SMEM((1,), jnp.int32)])
    return pl.pallas_call(
        _moe_kernel, grid_spec=grid_spec,
        out_shape=jax.ShapeDtypeStruct((nb * 2 * m * PACK_ROWS, LANES), jnp.uint32),
        compiler_params=_params(), name="moe_experts",
    )(starts.reshape(-1), counts.reshape(-1), pad(order % m), pad(order), xt, w_gu, w_down)


def _combine_kernel(h_ref, y0_ref, y1_ref, route_ref, nw_ref, o_ref, *, final_norm):
    tm = h_ref.shape[0]
    token_rows = lambda ref: _unpack_pairs([ref[pl.ds(j, tm, stride=PACK_ROWS), :] for j in range(PACK_ROWS)])
    gates = route_ref[...].T
    out = h_ref[...] + gates[:, 2:3] * token_rows(y0_ref) + gates[:, 3:4] * token_rows(y1_ref)
    if final_norm:
        out = _rms(out, nw_ref[...])
    o_ref[...] = out


def _combine(h, y, route, norm_w, final_norm, name):
    t, d = h.shape
    tm = min(ROW_TILE, t)
    m = min(MOE_BLOCK, t)
    per_block = m // tm

    def slot(s):
        return lambda i: ((i // per_block) * 2 * per_block + s * per_block + i % per_block, 0)

    return pl.pallas_call(
        functools.partial(_combine_kernel, final_norm=final_norm), grid=(t // tm,),
        in_specs=[pl.BlockSpec((tm, d), lambda i: (i, 0)),
                  pl.BlockSpec((tm * PACK_ROWS, LANES), slot(0)),
                  pl.BlockSpec((tm * PACK_ROWS, LANES), slot(1)),
                  pl.BlockSpec((SUBLANES, tm), lambda i: (0, i)),
                  _resident((1, d))],
        out_specs=pl.BlockSpec((tm, d), lambda i: (i, 0)),
        out_shape=jax.ShapeDtypeStruct((t, d), F32),
        compiler_params=_params(), name=name,
    )(h, y, y, route, norm_w.reshape(1, d))


def _moe_layer(h, norm_w, rg_w, rg_b, re_w, re_b, w_gate, w_up, w_down, out_norm_w, final_norm, name):
    t, d = h.shape
    xt, route, tile_counts = _router(h, norm_w, rg_w, rg_b, re_w, re_b)
    w_gu = jnp.concatenate([w_gate, w_up], axis=-1).astype(BF16)
    y = _moe(xt, route, tile_counts, w_gu, w_down.astype(BF16), t, d)
    return _combine(h, y, route, out_norm_w, final_norm, name)


def kernel(x, mem, a_norm_w, a_in_w, a_conv_w, a_conv_b, a_dt_bias, a_a_log, a_d_skip, a_gnorm_w, a_out_w, kv_norm_w, w_kv, b_norm_w, b_in_w, b_sinks, b_out_w, mem_norm_w, mem_w_kv, ffn_norm_w, router_group_w, router_group_b, router_expert_w, router_expert_b, w_gate, w_up, w_down, final_norm_w):
    bsz, seq, d = x.shape
    mem_len = mem.shape[1]
    t = bsz * seq
    h = x.reshape(t, d)
    mem2 = mem.reshape(bsz * mem_len, d)
    conv_ch = a_conv_w.shape[-1]
    n_a = a_in_w.shape[0]
    n_layers = n_a + b_in_w.shape[0]
    kv_sh = None
    for layer in range(n_layers):
        (mem_kv,) = _norm_matmul(mem2, mem_norm_w[layer], [mem_w_kv[layer].astype(BF16)], [BF16],
                                 name=f"mem_kv_{layer}")
        if layer < n_a:
            i = layer
            w_in = a_in_w[i]
            o_x, o_dt, o_q = SSD_INNER, SSD_INNER + conv_ch, SSD_INNER + conv_ch + SSD_HEADS
            w_dt = jnp.pad(w_in[:, o_dt:o_q], ((0, 0), (0, LANES - SSD_HEADS)))
            z, xbc, q_mem, dt = _norm_matmul(
                h, a_norm_w[i],
                [w_in[:, :o_x].astype(BF16), w_in[:, o_x:o_dt].astype(BF16),
                 w_in[:, o_q:].astype(BF16), w_dt.astype(BF16)],
                [BF16, BF16, BF16, F32], name=f"ssd_in_proj_{i}")
            y = _ssd(xbc, z, dt, a_conv_w[i], a_conv_b[i], a_dt_bias[i], a_a_log[i], a_d_skip[i],
                     a_gnorm_w[i], bsz, seq)
            m_att = _mem_attn(q_mem, mem_kv, bsz, seq, mem_len, name=f"mem_attn_{layer}")
            w_out = a_out_w[i].astype(BF16)
            h = _mix_out(y, m_att, w_out[:SSD_INNER], w_out[SSD_INNER:], h, name=f"ssd_out_proj_{i}")
        else:
            j = layer - n_a
            if j == 0:
                (kv_sh,) = _norm_matmul(h, kv_norm_w, [w_kv.astype(BF16)], [BF16], name="shared_kv")
            qw = Q_HEADS * HEAD_DIM
            w_in = b_in_w[j].astype(BF16)
            q, q_mem = _norm_matmul(h, b_norm_w[j], [w_in[:, :qw], w_in[:, qw:]], [BF16, BF16],
                                    name=f"swa_in_proj_{j}")
            att = _swa(q, kv_sh, b_sinks[j], bsz, seq)
            m_att = _mem_attn(q_mem, mem_kv, bsz, seq, mem_len, name=f"mem_attn_{layer}")
            w_out = b_out_w[j].astype(BF16)
            h = _mix_out(att, m_att, w_out[:qw], w_out[qw:], h, name=f"swa_out_proj_{j}")
        last = layer == n_layers - 1
        h = _moe_layer(h, ffn_norm_w[layer], router_group_w[layer], router_group_b[layer],
                       router_expert_w[layer], router_expert_b[layer],
                       w_gate[layer], w_up[layer], w_down[layer],
                       final_norm_w, last, name=f"moe_combine_{layer}")
    return h.reshape(bsz, seq, d)
```

```python
import functools

import jax
import jax.numpy as jnp
from jax import lax
from jax.experimental import pallas as pl
from jax.experimental.pallas import tpu as pltpu

F32 = jnp.float32
BF16 = jnp.bfloat16
NORM_EPS = 1e-6
LOG2_E = 1.4426950408889634

LANES = 128
SUBLANES = 8
VMEM_LIMIT = 56 * 1024 * 1024

MEM_HEADS = 4
MEM_HEAD_DIM = 128
MEM_WIDTH = MEM_HEADS * MEM_HEAD_DIM
SSD_HEAD_DIM = 64
SSD_HEADS = 24
SSD_INNER = SSD_HEADS * SSD_HEAD_DIM
SSD_GROUPS = 4
SSD_STATE = 128
SSD_CHUNK = 128
CONV_WIDTH = 4
CONV_HIST = 16
HEAD_DIM = 64
Q_HEADS = 16
KV_HEADS = 4
ATTN_BLOCK = 128
ROT_DIM = 16
ROPE_THETA = 500000.0
N_GROUPS = 4
EXPERTS_PER_GROUP = 8
N_EXPERTS = 32
D_EXPERT = 512

ROW_TILE = 512
MOE_BLOCK = 4096
MOE_UNIT = 64
MOE_CLASSES = 8
MOE_ROWS = MOE_UNIT * MOE_CLASSES
MOE_STRIDE = MOE_ROWS + SUBLANES
PACK = 2
PACK_ROWS = SUBLANES // PACK
HIGH_HALF = 0xFFFF0000


def _params():
    return pltpu.CompilerParams(vmem_limit_bytes=VMEM_LIMIT)


def _resident(shape):
    nd = len(shape)
    return pl.BlockSpec(shape, lambda *_: (0,) * nd, pipeline_mode=pl.Buffered(1))


def _rms(x, w):
    ms = jnp.mean(x * x, axis=-1, keepdims=True)
    return x * lax.rsqrt(ms + NORM_EPS) * w


def _silu(x):
    return x * (1.0 / (1.0 + jnp.exp(-x)))


def _pack_pairs(x):
    half = x.shape[1] // PACK
    bits = lax.bitcast_convert_type(x.astype(BF16).astype(F32), jnp.uint32)
    return [(bits[:, j * LANES:(j + 1) * LANES] >> 16)
            | (bits[:, half + j * LANES:half + (j + 1) * LANES] & jnp.uint32(HIGH_HALF))
            for j in range(half // LANES)]


def _unpack_pairs(words):
    lo = [lax.bitcast_convert_type(w << 16, F32) for w in words]
    hi = [lax.bitcast_convert_type(w & jnp.uint32(HIGH_HALF), F32) for w in words]
    return jnp.concatenate(lo + hi, axis=-1)


def _norm_matmul_kernel(x_ref, nw_ref, *refs, n_w):
    w_refs, o_refs = refs[:n_w], refs[n_w:]
    xn = _rms(x_ref[...], nw_ref[...]).astype(BF16)
    for w_ref, o_ref in zip(w_refs, o_refs):
        n = w_ref.shape[1]
        for c0 in range(0, n, 512):
            cw = min(512, n - c0)
            acc = jnp.dot(xn, w_ref[:, c0:c0 + cw], preferred_element_type=F32)
            o_ref[:, c0:c0 + cw] = acc.astype(o_ref.dtype)


def _norm_matmul(x, norm_w, weights, out_dtypes, name):
    t, d = x.shape
    tm = min(ROW_TILE, t)
    in_specs = [pl.BlockSpec((tm, d), lambda i: (i, 0)), _resident((1, d))]
    in_specs += [_resident(w.shape) for w in weights]
    out_specs = [pl.BlockSpec((tm, w.shape[1]), lambda i: (i, 0)) for w in weights]
    out_shape = [jax.ShapeDtypeStruct((t, w.shape[1]), dt) for w, dt in zip(weights, out_dtypes)]
    return pl.pallas_call(
        functools.partial(_norm_matmul_kernel, n_w=len(weights)),
        grid=(t // tm,), in_specs=in_specs, out_specs=out_specs, out_shape=out_shape,
        compiler_params=_params(), name=name,
    )(x, norm_w.reshape(1, d), *weights)


def _mix_out_kernel(a1_ref, a2_ref, w1_ref, w2_ref, r_ref, o_ref):
    acc = jnp.dot(a1_ref[...], w1_ref[...], preferred_element_type=F32)
    acc = acc + jnp.dot(a2_ref[...], w2_ref[...], preferred_element_type=F32)
    o_ref[...] = r_ref[...] + acc


def _mix_out(a1, a2, w1, w2, res, name):
    t, d = res.shape
    tm = min(ROW_TILE, t)
    return pl.pallas_call(
        _mix_out_kernel, grid=(t // tm,),
        in_specs=[pl.BlockSpec((tm, a1.shape[1]), lambda i: (i, 0)),
                  pl.BlockSpec((tm, a2.shape[1]), lambda i: (i, 0)),
                  _resident(w1.shape), _resident(w2.shape),
                  pl.BlockSpec((tm, d), lambda i: (i, 0))],
        out_specs=pl.BlockSpec((tm, d), lambda i: (i, 0)),
        out_shape=jax.ShapeDtypeStruct((t, d), F32),
        compiler_params=_params(), name=name,
    )(a1, a2, w1, w2, res)


def _mem_attn_kernel(q_ref, kv_ref, o_ref):
    scale = MEM_HEAD_DIM ** -0.5
    outs = []
    for h in range(MEM_HEADS):
        lo = h * MEM_HEAD_DIM
        q = q_ref[:, lo:lo + MEM_HEAD_DIM]
        k = kv_ref[:, lo:lo + MEM_HEAD_DIM]
        v = kv_ref[:, MEM_WIDTH + lo:MEM_WIDTH + lo + MEM_HEAD_DIM]
        s = lax.dot_general(q, k, (((1,), (1,)), ((), ())), preferred_element_type=F32) * scale
        p = jnp.exp(s - jnp.max(s, axis=-1, keepdims=True))
        p = p / jnp.sum(p, axis=-1, keepdims=True)
        outs.append(jnp.dot(p.astype(BF16), v, preferred_element_type=F32))
    o_ref[...] = jnp.concatenate(outs, axis=-1).astype(o_ref.dtype)


def _mem_attn(q, kv, bsz, seq, mem_len, name):
    tq = min(ROW_TILE, seq)
    nq = seq // tq
    return pl.pallas_call(
        _mem_attn_kernel, grid=(bsz, nq),
        in_specs=[pl.BlockSpec((tq, MEM_WIDTH), lambda b, i: (b * nq + i, 0)),
                  pl.BlockSpec((mem_len, 2 * MEM_WIDTH), lambda b, i: (b, 0))],
        out_specs=pl.BlockSpec((tq, MEM_WIDTH), lambda b, i: (b * nq + i, 0)),
        out_shape=jax.ShapeDtypeStruct((bsz * seq, MEM_WIDTH), BF16),
        compiler_params=_params(), name=name,
    )(q, kv)


def _ssd_kernel(xbc_ref, z_ref, dt_ref, cw_ref, cb_ref, dtb_ref, alog_ref, dsk_ref, gnw_ref,
                expand_ref, o_ref, ext_ref, st_ref):
    c = pl.program_id(1)
    L = SSD_CHUNK
    gw = SSD_INNER // SSD_GROUPS
    hpg = SSD_HEADS // SSD_GROUPS

    @pl.when(c == 0)
    def _():
        ext_ref[0:SUBLANES, :] = jnp.zeros((SUBLANES, ext_ref.shape[1]), F32)
        st_ref[...] = jnp.zeros_like(st_ref)

    @pl.when(c > 0)
    def _():
        ext_ref[0:SUBLANES, :] = ext_ref[L:L + SUBLANES, :]

    ext_ref[SUBLANES:SUBLANES + L, :] = xbc_ref[...].astype(F32)
    conv = cb_ref[...]
    for k in range(CONV_WIDTH):
        conv = conv + cw_ref[k:k + 1, :] * ext_ref[pl.ds(SUBLANES - (CONV_WIDTH - 1) + k, L), :]
    xbc = _silu(conv)
    xs = xbc[:, :SSD_INNER]
    bm = xbc[:, SSD_INNER:SSD_INNER + SSD_GROUPS * SSD_STATE]
    cm = xbc[:, SSD_INNER + SSD_GROUPS * SSD_STATE:].astype(BF16)

    pre = dt_ref[...] + dtb_ref[...]
    dt = jnp.maximum(pre, 0.0) + jnp.log(1.0 + jnp.exp(-jnp.abs(pre)))
    a = -jnp.exp(alog_ref[...])
    adt = dt * a
    row = lax.broadcasted_iota(jnp.int32, (L, L), 0)
    col = lax.broadcasted_iota(jnp.int32, (L, L), 1)
    causal = row >= col
    acum = jnp.dot(causal.astype(F32), adt, preferred_element_type=F32,
                   precision=lax.Precision.HIGHEST)
    acum_t = acum.T
    ea = jnp.exp(acum)
    te = jnp.exp(acum[L - 1:L, :] - acum)
    widen = lambda cols: jnp.dot(cols.astype(BF16), expand_ref[...], preferred_element_type=F32)
    dtx, eax, tex = widen(dt), widen(ea), widen(te)
    xdt = xs * dtx
    xdt_b = xdt.astype(BF16)
    xw = (xdt * tex).astype(BF16)

    first_half = col < SSD_HEAD_DIM
    ys = []
    for g in range(SSD_GROUPS):
        cg = cm[:, g * SSD_STATE:(g + 1) * SSD_STATE]
        bg_t = bm[:, g * SSD_STATE:(g + 1) * SSD_STATE].T.astype(BF16)
        cb = jnp.dot(cg, bg_t, preferred_element_type=F32)
        state = st_ref[g]
        y_off = jnp.dot(cg, state.astype(BF16), preferred_element_type=F32) * eax[:, g * gw:(g + 1) * gw]
        parts = []
        for r in range(0, hpg, 2):
            both = []
            for h in (g * hpg + r, g * hpg + r + 1):
                seg = acum[:, h:h + 1] - acum_t[h:h + 1, :]
                m = (cb * jnp.exp(jnp.where(causal, seg, -jnp.inf))).astype(BF16)
                lo = (g * hpg + r) * SSD_HEAD_DIM
                both.append(jnp.dot(m, xdt_b[:, lo:lo + LANES], preferred_element_type=F32))
            parts.append(jnp.where(first_half, both[0], both[1]))
        ys.append(jnp.concatenate(parts, axis=-1) + y_off)
        upd = jnp.dot(bg_t, xw[:, g * gw:(g + 1) * gw], preferred_element_type=F32)
        st_ref[g] = state * eax[L - 1:L, g * gw:(g + 1) * gw] + upd
    y = jnp.concatenate(ys, axis=-1) + dsk_ref[...] * xs
    zf = z_ref[...].astype(F32)
    y = y * _silu(zf)
    normed = []
    for g in range(SSD_GROUPS):
        yg = y[:, g * gw:(g + 1) * gw]
        normed.append(yg * lax.rsqrt(jnp.mean(yg * yg, axis=-1, keepdims=True) + NORM_EPS))
    o_ref[...] = (jnp.concatenate(normed, axis=-1) * gnw_ref[...]).astype(o_ref.dtype)


def _ssd(xbc, z, dt, conv_w, conv_b, dt_bias, a_log, d_skip, gnorm_w, bsz, seq):
    nc = seq // SSD_CHUNK
    conv_ch = xbc.shape[1]
    pad = LANES - SSD_HEADS
    dtb = jnp.pad(dt_bias, (0, pad)).reshape(1, LANES)
    alog = jnp.pad(a_log, (0, pad)).reshape(1, LANES)
    dsk = jnp.repeat(d_skip, SSD_HEAD_DIM).reshape(1, SSD_INNER)
    expand = (jnp.arange(LANES)[:, None] == (jnp.arange(SSD_INNER) // SSD_HEAD_DIM)[None, :]).astype(BF16)
    blk =lambda w: pl.BlockSpec((SSD_CHUNK, w), lambda b, c: (b * nc + c, 0))
    return pl.pallas_call(
        _ssd_kernel, grid=(bsz, nc),
        in_specs=[blk(conv_ch), blk(SSD_INNER), blk(LANES),
                  _resident((CONV_WIDTH, conv_ch)), _resident((1, conv_ch)),
                  _resident((1, LANES)), _resident((1, LANES)),
                  _resident((1, SSD_INNER)), _resident((1, SSD_INNER)),
                  _resident((LANES, SSD_INNER))],
        out_specs=blk(SSD_INNER),
        out_shape=jax.ShapeDtypeStruct((bsz * seq, SSD_INNER), BF16),
        scratch_shapes=[pltpu.VMEM((SSD_CHUNK + 2 * SUBLANES, conv_ch), F32),
                        pltpu.VMEM((SSD_GROUPS, SSD_STATE, SSD_INNER // SSD_GROUPS), F32)],
        compiler_params=_params(), name="ssd_mixer",
    )(xbc, z, dt, conv_w, conv_b.reshape(1, conv_ch), dtb, alog, dsk,
      gnorm_w.reshape(1, SSD_INNER), expand)


def _rope(x, cos, sin_up, sin_dn):
    w = x.shape[1]
    reps = w // LANES
    tile = lambda t: jnp.concatenate([t] * reps, axis=-1) if reps > 1 else t
    half = ROT_DIM // 2
    return (x * tile(cos) + pltpu.roll(x, half, axis=1) * tile(sin_up)
            + pltpu.roll(x, w - half, axis=1) * tile(sin_dn))


def _swa_kernel(sink_ref, q_ref, kvp_ref, kvc_ref, tc_ref, bias_ref, o_ref):
    kvw = KV_HEADS * HEAD_DIM
    g = Q_HEADS // KV_HEADS
    scale = HEAD_DIM ** -0.5 * LOG2_E
    q = _rope(q_ref[...].astype(F32), tc_ref[0], tc_ref[1], tc_ref[2]) * scale
    q_t = q.T.astype(BF16)
    k = jnp.concatenate([kvp_ref[:, :kvw], kvc_ref[:, :kvw]], axis=0)
    v = jnp.concatenate([kvp_ref[:, kvw:], kvc_ref[:, kvw:]], axis=0).astype(F32)
    v_t = v.T.astype(BF16)
    bias = bias_ref[0]
    outs = []
    for kh in range(KV_HEADS):
        kk = k[:, kh * HEAD_DIM:(kh + 1) * HEAD_DIM]
        sc = jnp.concatenate(
            [jnp.dot(kk, q_t[h * HEAD_DIM:(h + 1) * HEAD_DIM, :], preferred_element_type=F32)
             for h in range(kh * g, (kh + 1) * g)], axis=-1) + bias
        sink = jnp.concatenate([jnp.full((1, ATTN_BLOCK), sink_ref[h] * LOG2_E, F32)
                                for h in range(kh * g, (kh + 1) * g)], axis=-1)
        mx = jnp.maximum(jnp.max(sc, axis=0, keepdims=True), sink)
        p = jnp.exp2(sc - mx)
        denom = jnp.sum(p, axis=0, keepdims=True) + jnp.exp2(sink - mx)
        o_t = jnp.dot(v_t[kh * HEAD_DIM:(kh + 1) * HEAD_DIM, :], p.astype(BF16),
                      preferred_element_type=F32) / denom
        outs += [o_t[:, j * ATTN_BLOCK:(j + 1) * ATTN_BLOCK] for j in range(g)]
    o_ref[...] = jnp.concatenate(outs, axis=0).T.astype(o_ref.dtype)


def _swa_bias(g):
    blk = ATTN_BLOCK
    s_idx = jnp.arange(2 * blk)[:, None]
    t_idx = jnp.arange(blk)[None, :]
    band = (s_idx > t_idx) & (s_idx <= t_idx + blk)
    first = band & (s_idx >= blk)
    both = jnp.stack([first, band], axis=0)
    return jnp.tile(jnp.where(both, 0.0, -jnp.inf).astype(F32), (1, 1, g))


def _rope_tables(seq):
    half = ROT_DIM // 2
    inv_freq = ROPE_THETA ** (-2.0 * jnp.arange(half, dtype=F32) / ROT_DIM)
    ang = jnp.arange(seq, dtype=jnp.int32).astype(F32)[:, None] * inv_freq[None, :]
    cos, sin = jnp.cos(ang), jnp.sin(ang)
    ones = jnp.ones((seq, HEAD_DIM - ROT_DIM), F32)
    zeros = jnp.zeros((seq, HEAD_DIM - ROT_DIM), F32)
    zh = jnp.zeros((seq, half), F32)
    cos_h = jnp.concatenate([cos, cos, ones], axis=-1)
    up_h = jnp.concatenate([zh, sin, zeros], axis=-1)
    dn_h = jnp.concatenate([-sin, zh, zeros], axis=-1)
    two = lambda t: jnp.concatenate([t, t], axis=-1)
    return jnp.stack([two(cos_h), two(up_h), two(dn_h)], axis=0)


def _shared_kv_kernel(x_ref, nw_ref, w_ref, t_ref, o_ref):
    kvw = KV_HEADS * HEAD_DIM
    xn = _rms(x_ref[...], nw_ref[...]).astype(BF16)
    kv = jnp.dot(xn, w_ref[...], preferred_element_type=F32)
    k = _rope(kv[:, :kvw], t_ref[0], t_ref[1], t_ref[2])
    o_ref[...] = jnp.concatenate([k, kv[:, kvw:]], axis=-1).astype(o_ref.dtype)


def _shared_kv(h, norm_w, w, tables, seq):
    t, d = h.shape
    tm = min(ROW_TILE, seq)
    per_seq = seq // tm
    return pl.pallas_call(
        _shared_kv_kernel, grid=(t // tm,),
        in_specs=[pl.BlockSpec((tm, d), lambda i: (i, 0)), _resident((1, d)), _resident(w.shape),
                  pl.BlockSpec((3, tm, LANES), lambda i: (0, i % per_seq, 0))],
        out_specs=pl.BlockSpec((tm, w.shape[1]), lambda i: (i, 0)),
        out_shape=jax.ShapeDtypeStruct((t, w.shape[1]), BF16),
        compiler_params=_params(), name="shared_kv",
    )(h, norm_w.reshape(1, d), w, tables)


def _swa(q, kv, sinks, tables, bsz, seq):
    nb = seq // ATTN_BLOCK
    qw = Q_HEADS * HEAD_DIM
    kvw2 = 2 * KV_HEADS * HEAD_DIM
    prev = lambda i: jnp.maximum(i - 1, 0)
    g = Q_HEADS // KV_HEADS
    return pl.pallas_call(
        _swa_kernel, grid=(bsz, nb),
        in_specs=[pl.BlockSpec(memory_space=pltpu.SMEM),
                  pl.BlockSpec((ATTN_BLOCK, qw), lambda b, i: (b * nb + i, 0)),
                  pl.BlockSpec((ATTN_BLOCK, kvw2), lambda b, i: (b * nb + prev(i), 0)),
                  pl.BlockSpec((ATTN_BLOCK, kvw2), lambda b, i: (b * nb + i, 0)),
                  pl.BlockSpec((3, ATTN_BLOCK, LANES), lambda b, i: (0, i, 0)),
                  pl.BlockSpec((1, 2 * ATTN_BLOCK, g * ATTN_BLOCK), lambda b, i: (jnp.minimum(i, 1), 0, 0))],
        out_specs=pl.BlockSpec((ATTN_BLOCK, qw), lambda b, i: (b * nb + i, 0)),
        out_shape=jax.ShapeDtypeStruct((bsz * seq, qw), BF16),
        compiler_params=_params(), name="swa_attention",
    )(sinks, q, kv, kv, tables, _swa_bias(g))


def _router_kernel(h_ref, nw_ref, wr_ref, br_ref, xt_ref, route_ref, cnt_ref):
    tm = h_ref.shape[0]
    xn = _rms(h_ref[...], nw_ref[...])
    x_hi = xn.astype(BF16)
    for j, word in enumerate(_pack_pairs(xn)):
        xt_ref[pl.ds(j, tm, stride=PACK_ROWS), :] = word
    x_lo = (xn - x_hi.astype(F32)).astype(BF16)
    both = jnp.dot(x_hi, wr_ref[...], preferred_element_type=F32)
    logits = (both[:, :LANES] + both[:, LANES:]
              + jnp.dot(x_lo, wr_ref[:, :LANES], preferred_element_type=F32) + br_ref[...])
    lt = logits.T
    neg = -jnp.inf
    grow = lax.broadcasted_iota(jnp.int32, (SUBLANES, tm), 0)
    gl = jnp.where(grow < N_GROUPS, lt[:SUBLANES], neg)
    gmax = jnp.max(gl, axis=0, keepdims=True)
    g_w = 1.0 / jnp.sum(jnp.exp(gl - gmax), axis=0, keepdims=True)
    g_idx = jnp.min(jnp.where(gl == gmax, grow, SUBLANES), axis=0, keepdims=True)
    erow = lax.broadcasted_iota(jnp.int32, (N_EXPERTS, tm), 0)
    el = jnp.where(erow // EXPERTS_PER_GROUP == g_idx, lt[SUBLANES:SUBLANES + N_EXPERTS], neg)
    m1 = jnp.max(el, axis=0, keepdims=True)
    i1 = jnp.min(jnp.where(el == m1, erow, N_EXPERTS), axis=0, keepdims=True)
    el2 = jnp.where(erow == i1, neg, el)
    m2 = jnp.max(el2, axis=0, keepdims=True)
    i2 = jnp.min(jnp.where(el2 == m2, erow, N_EXPERTS), axis=0, keepdims=True)
    d = jnp.exp(m2 - m1)
    w1 = g_w / (1.0 + d)
    w2 = g_w * d / (1.0 + d)
    zero = jnp.zeros((SUBLANES - 4, tm), F32)
    route_ref[...] = jnp.concatenate([i1.astype(F32), i2.astype(F32), w1, w2, zero], axis=0)
    chosen = ((erow == i1) | (erow == i2)).astype(F32)
    cnt_ref[...] = jnp.broadcast_to(jnp.sum(chosen, axis=1, keepdims=True), cnt_ref.shape)


def _router(h, norm_w, rg_w, rg_b, re_w, re_b):
    t, d = h.shape
    tm = min(ROW_TILE, t)
    gap = jnp.zeros((d, SUBLANES - N_GROUPS), F32)
    tail = jnp.zeros((d, LANES - SUBLANES - N_EXPERTS), F32)
    wr = jnp.concatenate([rg_w, gap, re_w, tail], axis=1)
    w_hi = wr.astype(BF16)
    w_lo = (wr - w_hi.astype(F32)).astype(BF16)
    br = jnp.concatenate([rg_b, gap[0], re_b, tail[0]]).reshape(1, LANES)
    return pl.pallas_call(
        _router_kernel, grid=(t // tm,),
        in_specs=[pl.BlockSpec((tm, d), lambda i: (i, 0)), _resident((1, d)),
                  _resident((d, 2 * LANES)), _resident((1, LANES))],
        out_specs=[pl.BlockSpec((tm * PACK_ROWS, LANES), lambda i: (i, 0)),
                   pl.BlockSpec((SUBLANES, tm), lambda i: (0, i)),
                   pl.BlockSpec((N_EXPERTS, LANES), lambda i: (i, 0))],
        out_shape=[jax.ShapeDtypeStruct((t * PACK_ROWS, LANES), jnp.uint32),
                   jax.ShapeDtypeStruct((SUBLANES, t), F32),
                   jax.ShapeDtypeStruct((t // tm * N_EXPERTS, LANES), F32)],
        compiler_params=_params(), name="moe_router",
    )(h, norm_w.reshape(1, d), jnp.concatenate([w_hi, w_lo], axis=1), br)


def _moe_kernel(st_ref, cnt_ref, tok_ref, dst_ref, xs_ref, wgu_ref, wd_ref, o_ref,
                tile_ref, xb_ref, ybuf_ref, state_ref):
    b = pl.program_id(0)
    e = pl.program_id(1)
    stride = MOE_STRIDE
    unroll = SUBLANES
    READY, PEND_BASE, PEND_ROWS = 0, 1, 2

    @pl.when((b == 0) & (e == 0))
    def _():
        ybuf_ref[...] = jnp.zeros_like(ybuf_ref)

    @pl.when(e == 0)
    def _():
        state_ref[READY] = 0
        state_ref[PEND_BASE] = 0
        state_ref[PEND_ROWS] = 0

    start = st_ref[b * N_EXPERTS + e]
    count = cnt_ref[b * N_EXPERTS + e]

    def gather_row(base, r):
        tok = tok_ref[0, 0, base + r]
        src = pl.ds(pl.multiple_of(tok * PACK_ROWS, PACK_ROWS), PACK_ROWS)
        tile_ref[pl.ds(r, PACK_ROWS, stride=stride), :] = xs_ref[src, :]

    def scatter_row(base, r):
        dst = dst_ref[0, 0, base + r]
        o_ref[pl.ds(pl.multiple_of(dst * PACK_ROWS, PACK_ROWS), PACK_ROWS), :] = (
            ybuf_ref[pl.ds(r, PACK_ROWS, stride=stride), :])

    def copy_rows(row_fn, base, lo, hi):
        def group(i, c):
            for u in range(unroll):
                row_fn(base, i * unroll + u)
            return c
        lax.fori_loop(lo // unroll, hi // unroll, group, 0)

    def expert_mlp(rows, next_base, pend_base):
        def run():
            words = [tile_ref[pl.ds(j * stride, rows), :] for j in range(PACK_ROWS)]
            xb_ref[0:rows, :] = _unpack_pairs(words).astype(BF16)
            for r in range(rows):
                gather_row(next_base, r)
                scatter_row(pend_base, r)
            gu = jnp.dot(xb_ref[0:rows, :], wgu_ref[...], preferred_element_type=F32)
            gate, up = gu[:, :D_EXPERT], gu[:, D_EXPERT:]
            he = (_silu(gate) * up).astype(BF16)
            y = jnp.dot(he, wd_ref[...], preferred_element_type=F32)
            for j, word in enumerate(_pack_pairs(y)):
                ybuf_ref[pl.ds(j * stride, rows), :] = word
        return run

    def tile_body(t, carry):
        base = start + t * MOE_ROWS
        rows = jnp.minimum(count - t * MOE_ROWS, MOE_ROWS)
        copy_rows(gather_row, base, state_ref[READY], rows)
        pend_rows = state_ref[PEND_ROWS]
        pend_base = jnp.where(pend_rows > 0, state_ref[PEND_BASE], base)
        copy_rows(scatter_row, pend_base, jnp.minimum(rows, pend_rows), pend_rows)
        next_base = base + rows
        lax.switch(rows // MOE_UNIT - 1, [expert_mlp(MOE_UNIT * (k + 1), next_base, pend_base)
                                          for k in range(MOE_CLASSES)])
        state_ref[READY] = rows
        state_ref[PEND_BASE] = base
        state_ref[PEND_ROWS] = rows
        return carry

    lax.fori_loop(0, (count + MOE_ROWS - 1) // MOE_ROWS, tile_body, 0)

    @pl.when(e == N_EXPERTS - 1)
    def _():
        copy_rows(scatter_row, state_ref[PEND_BASE], 0, state_ref[PEND_ROWS])


def _moe_plan(route, tile_counts, nb, m):
    pick = lambda r: route[r].reshape(nb, m)
    e_idx = jnp.concatenate([pick(0), pick(1)], axis=1).astype(jnp.int32)
    order = jnp.argsort(e_idx, axis=1, stable=True).astype(jnp.int32)
    counts = tile_counts[:, 0].reshape(nb, -1, N_EXPERTS).sum(axis=1).astype(jnp.int32)
    starts = jnp.cumsum(counts, axis=1) - counts
    padded = (counts + MOE_UNIT - 1) // MOE_UNIT * MOE_UNIT
    ends = jnp.cumsum(padded, axis=1)
    length = 2 * m + N_EXPERTS * MOE_UNIT + MOE_ROWS
    pos = jnp.arange(length, dtype=jnp.int32)[None, :]
    run = jnp.sum((pos[:, :, None] >= ends[:, None, :]).astype(jnp.int32), axis=-1)
    inside = run < N_EXPERTS
    run = jnp.minimum(run, N_EXPERTS - 1)
    look = lambda table: jnp.take_along_axis(table, run, axis=1)
    rank = jnp.minimum(pos - look(ends - padded), look(counts) - 1)
    src = jnp.clip(look(starts) + rank, 0, 2 * m - 1)
    assignment = jnp.where(inside, jnp.take_along_axis(order, src, axis=1), 0)
    as_list = lambda a: a.reshape(nb, 1, length)
    return (ends - padded).reshape(-1), padded.reshape(-1), as_list(assignment % m), as_list(assignment)


def _moe(xt, route, tile_counts, w_gu, w_down, t, d):
    m = min(MOE_BLOCK, t)
    nb = t // m
    starts, padded, tok_list, dst_list = _moe_plan(route, tile_counts, nb, m)
    length = tok_list.shape[-1]
    grid_spec = pltpu.PrefetchScalarGridSpec(
        num_scalar_prefetch=2, grid=(nb, N_EXPERTS),
        in_specs=[pl.BlockSpec((1, 1, length), lambda b, e, *_: (b, 0, 0), memory_space=pltpu.SMEM),
                  pl.BlockSpec((1, 1, length), lambda b, e, *_: (b, 0, 0), memory_space=pltpu.SMEM),
                  pl.BlockSpec((m * PACK_ROWS, LANES), lambda b, e, *_: (b, 0), pipeline_mode=pl.Buffered(1)),
                  pl.BlockSpec((None, d, 2 * D_EXPERT), lambda b, e, *_: (e, 0, 0)),
                  pl.BlockSpec((None, D_EXPERT, d), lambda b, e, *_: (e, 0, 0))],
        out_specs=pl.BlockSpec((2 * m * PACK_ROWS, LANES), lambda b, e, *_: (b, 0)),
        scratch_shapes=[pltpu.VMEM((PACK_ROWS * MOE_STRIDE, LANES), jnp.uint32),
                        pltpu.VMEM((MOE_ROWS, d), BF16),
                        pltpu.VMEM((PACK_ROWS * MOE_STRIDE, LANES), jnp.uint32),
                        pltpu.SMEM((3,), jnp.int32)])
    return pl.pallas_call(
        _moe_kernel, grid_spec=grid_spec,
        out_shape=jax.ShapeDtypeStruct((nb * 2 * m * PACK_ROWS, LANES), jnp.uint32),
        compiler_params=_params(), name="moe_experts",
    )(starts, padded, tok_list, dst_list, xt, w_gu, w_down)


def _combine_kernel(h_ref, y0_ref, y1_ref, route_ref, nw_ref, o_ref, *, final_norm):
    tm = h_ref.shape[0]
    token_rows = lambda ref: _unpack_pairs([ref[pl.ds(j, tm, stride=PACK_ROWS), :] for j in range(PACK_ROWS)])
    gates = route_ref[...].T
    out = h_ref[...] + gates[:, 2:3] * token_rows(y0_ref) + gates[:, 3:4] * token_rows(y1_ref)
    if final_norm:
        out = _rms(out, nw_ref[...])
    o_ref[...] = out


def _combine(h, y, route, norm_w, final_norm, name):
    t, d = h.shape
    tm = min(ROW_TILE, t)
    m = min(MOE_BLOCK, t)
    per_block = m // tm

    def slot(s):
        return lambda i: ((i // per_block) * 2 * per_block + s * per_block + i % per_block, 0)

    return pl.pallas_call(
        functools.partial(_combine_kernel, final_norm=final_norm), grid=(t // tm,),
        in_specs=[pl.BlockSpec((tm, d), lambda i: (i, 0)),
                  pl.BlockSpec((tm * PACK_ROWS, LANES), slot(0)),
                  pl.BlockSpec((tm * PACK_ROWS, LANES), slot(1)),
                  pl.BlockSpec((SUBLANES, tm), lambda i: (0, i)),
                  _resident((1, d))],
        out_specs=pl.BlockSpec((tm, d), lambda i: (i, 0)),
        out_shape=jax.ShapeDtypeStruct((t, d), F32),
        compiler_params=_params(), name=name,
    )(h, y, y, route, norm_w.reshape(1, d))


def _moe_layer(h, norm_w, rg_w, rg_b, re_w, re_b, w_gate, w_up, w_down, out_norm_w, final_norm, name):
    t, d = h.shape
    xt, route, tile_counts = _router(h, norm_w, rg_w, rg_b, re_w, re_b)
    w_gu = jnp.concatenate([w_gate, w_up], axis=-1).astype(BF16)
    y = _moe(xt, route, tile_counts, w_gu, w_down.astype(BF16), t, d)
    return _combine(h, y, route, out_norm_w, final_norm, name)


def kernel(x, mem, a_norm_w, a_in_w, a_conv_w, a_conv_b, a_dt_bias, a_a_log, a_d_skip, a_gnorm_w, a_out_w, kv_norm_w, w_kv, b_norm_w, b_in_w, b_sinks, b_out_w, mem_norm_w, mem_w_kv, ffn_norm_w, router_group_w, router_group_b, router_expert_w, router_expert_b, w_gate, w_up, w_down, final_norm_w):
    bsz, seq, d = x.shape
    mem_len = mem.shape[1]
    t = bsz * seq
    h = x.reshape(t, d)
    mem2 = mem.reshape(bsz * mem_len, d)
    conv_ch = a_conv_w.shape[-1]
    n_a = a_in_w.shape[0]
    n_layers = n_a + b_in_w.shape[0]
    kv_sh = None
    for layer in range(n_layers):
        (mem_kv,) = _norm_matmul(mem2, mem_norm_w[layer], [mem_w_kv[layer].astype(BF16)], [BF16],
                                 name=f"mem_kv_{layer}")
        if layer < n_a:
            i = layer
            w_in = a_in_w[i]
            o_x, o_dt, o_q = SSD_INNER, SSD_INNER + conv_ch, SSD_INNER + conv_ch + SSD_HEADS
            w_dt = jnp.pad(w_in[:, o_dt:o_q], ((0, 0), (0, LANES - SSD_HEADS)))
            z, xbc, q_mem, dt = _norm_matmul(
                h, a_norm_w[i],
                [w_in[:, :o_x].astype(BF16), w_in[:, o_x:o_dt].astype(BF16),
                 w_in[:, o_q:].astype(BF16), w_dt.astype(BF16)],
                [BF16, BF16, BF16, F32], name=f"ssd_in_proj_{i}")
            y = _ssd(xbc, z, dt, a_conv_w[i], a_conv_b[i], a_dt_bias[i], a_a_log[i], a_d_skip[i],
                     a_gnorm_w[i], bsz, seq)
            m_att = _mem_attn(q_mem, mem_kv, bsz, seq, mem_len, name=f"mem_attn_{layer}")
            w_out = a_out_w[i].astype(BF16)
            h = _mix_out(y, m_att, w_out[:SSD_INNER], w_out[SSD_INNER:], h, name=f"ssd_out_proj_{i}")
        else:
            j = layer - n_a
            if j == 0:
                tables = _rope_tables(seq)
                kv_sh = _shared_kv(h, kv_norm_w, w_kv.astype(BF16), tables, seq)
            qw = Q_HEADS * HEAD_DIM
            w_in = b_in_w[j].astype(BF16)
            q, q_mem = _norm_matmul(h, b_norm_w[j], [w_in[:, :qw], w_in[:, qw:]], [BF16, BF16],
                                    name=f"swa_in_proj_{j}")
            att = _swa(q, kv_sh, b_sinks[j], tables, bsz, seq)
            m_att = _mem_attn(q_mem, mem_kv, bsz, seq, mem_len, name=f"mem_attn_{layer}")
            w_out = b_out_w[j].astype(BF16)
            h = _mix_out(att, m_att, w_out[:qw], w_out[qw:], h, name=f"swa_out_proj_{j}")
        last = layer == n_layers - 1
        h = _moe_layer(h, ffn_norm_w[layer], router_group_w[layer], router_group_b[layer],
                       router_expert_w[layer], router_expert_b[layer],
                       w_gate[layer], w_up[layer], w_down[layer],
                       final_norm_w, last, name=f"moe_combine_{layer}")
    return h.reshape(bsz, seq, d)
```

```python
import functools

import jax
import jax.numpy as jnp
from jax import lax
from jax.experimental import pallas as pl
from jax.experimental.pallas import tpu as pltpu

F32 = jnp.float32
BF16 = jnp.bfloat16
NORM_EPS = 1e-6
LOG2_E = 1.4426950408889634

LANES = 128
SUBLANES = 8
VMEM_LIMIT = 56 * 1024 * 1024

MEM_HEADS = 4
MEM_HEAD_DIM = 128
MEM_WIDTH = MEM_HEADS * MEM_HEAD_DIM
SSD_HEAD_DIM = 64
SSD_HEADS = 24
SSD_INNER = SSD_HEADS * SSD_HEAD_DIM
SSD_GROUPS = 4
SSD_STATE = 128
SSD_CHUNK = 128
CONV_WIDTH = 4
CONV_HIST = 16
HEAD_DIM = 64
Q_HEADS = 16
KV_HEADS = 4
ATTN_BLOCK = 128
ROT_DIM = 16
ROPE_THETA = 500000.0
N_GROUPS = 4
EXPERTS_PER_GROUP = 8
N_EXPERTS = 32
D_EXPERT = 512

ROW_TILE = 512
MOE_BLOCK = 4096
MOE_UNIT = 64
MOE_CLASSES = 8
MOE_ROWS = MOE_UNIT * MOE_CLASSES
MOE_STRIDE = MOE_ROWS + SUBLANES
PACK = 2
PACK_ROWS = SUBLANES // PACK
HIGH_HALF = 0xFFFF0000


def _params():
    return pltpu.CompilerParams(vmem_limit_bytes=VMEM_LIMIT)


def _resident(shape):
    nd = len(shape)
    return pl.BlockSpec(shape, lambda *_: (0,) * nd, pipeline_mode=pl.Buffered(1))


def _rms(x, w):
    ms = jnp.mean(x * x, axis=-1, keepdims=True)
    return x * lax.rsqrt(ms + NORM_EPS) * w


def _silu(x):
    return x * (1.0 / (1.0 + jnp.exp(-x)))


def _pack_pairs(x):
    half = x.shape[1] // PACK
    bits = lax.bitcast_convert_type(x.astype(BF16).astype(F32), jnp.uint32)
    return [(bits[:, j * LANES:(j + 1) * LANES] >> 16)
            | (bits[:, half + j * LANES:half + (j + 1) * LANES] & jnp.uint32(HIGH_HALF))
            for j in range(half // LANES)]


def _unpack_pairs(words):
    lo = [lax.bitcast_convert_type(w << 16, F32) for w in words]
    hi = [lax.bitcast_convert_type(w & jnp.uint32(HIGH_HALF), F32) for w in words]
    return jnp.concatenate(lo + hi, axis=-1)


def _norm_matmul_kernel(x_ref, nw_ref, *refs, n_w):
    w_refs, o_refs = refs[:n_w], refs[n_w:]
    xn = _rms(x_ref[...], nw_ref[...]).astype(BF16)
    for w_ref, o_ref in zip(w_refs, o_refs):
        n = w_ref.shape[1]
        for c0 in range(0, n, 512):
            cw = min(512, n - c0)
            acc = jnp.dot(xn, w_ref[:, c0:c0 + cw], preferred_element_type=F32)
            o_ref[:, c0:c0 + cw] = acc.astype(o_ref.dtype)


def _norm_matmul(x, norm_w, weights, out_dtypes, name):
    t, d = x.shape
    tm = min(ROW_TILE, t)
    in_specs = [pl.BlockSpec((tm, d), lambda i: (i, 0)), _resident((1, d))]
    in_specs += [_resident(w.shape) for w in weights]
    out_specs = [pl.BlockSpec((tm, w.shape[1]), lambda i: (i, 0)) for w in weights]
    out_shape = [jax.ShapeDtypeStruct((t, w.shape[1]), dt) for w, dt in zip(weights, out_dtypes)]
    return pl.pallas_call(
        functools.partial(_norm_matmul_kernel, n_w=len(weights)),
        grid=(t // tm,), in_specs=in_specs, out_specs=out_specs, out_shape=out_shape,
        compiler_params=_params(), name=name,
    )(x, norm_w.reshape(1, d), *weights)


def _mix_out_kernel(a1_ref, a2_ref, w1_ref, w2_ref, r_ref, o_ref):
    acc = jnp.dot(a1_ref[...], w1_ref[...], preferred_element_type=F32)
    acc = acc + jnp.dot(a2_ref[...], w2_ref[...], preferred_element_type=F32)
    o_ref[...] = r_ref[...] + acc


def _mix_out(a1, a2, w1, w2, res, name):
    t, d = res.shape
    tm = min(ROW_TILE, t)
    return pl.pallas_call(
        _mix_out_kernel, grid=(t // tm,),
        in_specs=[pl.BlockSpec((tm, a1.shape[1]), lambda i: (i, 0)),
                  pl.BlockSpec((tm, a2.shape[1]), lambda i: (i, 0)),
                  _resident(w1.shape), _resident(w2.shape),
                  pl.BlockSpec((tm, d), lambda i: (i, 0))],
        out_specs=pl.BlockSpec((tm, d), lambda i: (i, 0)),
        out_shape=jax.ShapeDtypeStruct((t, d), F32),
        compiler_params=_params(), name=name,
    )(a1, a2, w1, w2, res)


def _mem_attn_kernel(q_ref, kv_ref, o_ref):
    scale = MEM_HEAD_DIM ** -0.5
    outs = []
    for h in range(MEM_HEADS):
        lo = h * MEM_HEAD_DIM
        q = q_ref[:, lo:lo + MEM_HEAD_DIM]
        k = kv_ref[:, lo:lo + MEM_HEAD_DIM]
        v = kv_ref[:, MEM_WIDTH + lo:MEM_WIDTH + lo + MEM_HEAD_DIM]
        s = lax.dot_general(q, k, (((1,), (1,)), ((), ())), preferred_element_type=F32) * scale
        p = jnp.exp(s - jnp.max(s, axis=-1, keepdims=True))
        p = p / jnp.sum(p, axis=-1, keepdims=True)
        outs.append(jnp.dot(p.astype(BF16), v, preferred_element_type=F32))
    o_ref[...] = jnp.concatenate(outs, axis=-1).astype(o_ref.dtype)


def _mem_attn(q, kv, bsz, seq, mem_len, name):
    tq = min(ROW_TILE, seq)
    nq = seq // tq
    return pl.pallas_call(
        _mem_attn_kernel, grid=(bsz, nq),
        in_specs=[pl.BlockSpec((tq, MEM_WIDTH), lambda b, i: (b * nq + i, 0)),
                  pl.BlockSpec((mem_len, 2 * MEM_WIDTH), lambda b, i: (b, 0))],
        out_specs=pl.BlockSpec((tq, MEM_WIDTH), lambda b, i: (b * nq + i, 0)),
        out_shape=jax.ShapeDtypeStruct((bsz * seq, MEM_WIDTH), BF16),
        compiler_params=_params(), name=name,
    )(q, kv)


def _ssd_kernel(xbc_ref, z_ref, dt_ref, cw_ref, cb_ref, dtb_ref, alog_ref, dsk_ref, gnw_ref,
                expand_ref, o_ref, ext_ref, st_ref):
    c = pl.program_id(1)
    L = SSD_CHUNK
    gw = SSD_INNER // SSD_GROUPS
    hpg = SSD_HEADS // SSD_GROUPS

    @pl.when(c == 0)
    def _():
        ext_ref[0:SUBLANES, :] = jnp.zeros((SUBLANES, ext_ref.shape[1]), F32)
        st_ref[...] = jnp.zeros_like(st_ref)

    @pl.when(c > 0)
    def _():
        ext_ref[0:SUBLANES, :] = ext_ref[L:L + SUBLANES, :]

    ext_ref[SUBLANES:SUBLANES + L, :] = xbc_ref[...].astype(F32)
    conv = cb_ref[...]
    for k in range(CONV_WIDTH):
        conv = conv + cw_ref[k:k + 1, :] * ext_ref[pl.ds(SUBLANES - (CONV_WIDTH - 1) + k, L), :]
    xbc = _silu(conv)
    xs = xbc[:, :SSD_INNER]
    bm = xbc[:, SSD_INNER:SSD_INNER + SSD_GROUPS * SSD_STATE]
    cm = xbc[:, SSD_INNER + SSD_GROUPS * SSD_STATE:].astype(BF16)

    pre = dt_ref[...] + dtb_ref[...]
    dt = jnp.maximum(pre, 0.0) + jnp.log(1.0 + jnp.exp(-jnp.abs(pre)))
    a = -jnp.exp(alog_ref[...])
    adt = dt * a
    row = lax.broadcasted_iota(jnp.int32, (L, L), 0)
    col = lax.broadcasted_iota(jnp.int32, (L, L), 1)
    causal = row >= col
    acum = jnp.dot(causal.astype(F32), adt, preferred_element_type=F32,
                   precision=lax.Precision.HIGHEST)
    acum_t = acum.T
    ea = jnp.exp(acum)
    te = jnp.exp(acum[L - 1:L, :] - acum)
    widen = lambda cols: jnp.dot(cols.astype(BF16), expand_ref[...], preferred_element_type=F32)
    dtx, eax, tex = widen(dt), widen(ea), widen(te)
    xdt = xs * dtx
    xdt_b = xdt.astype(BF16)
    xw = (xdt * tex).astype(BF16)

    first_half = col < SSD_HEAD_DIM
    ys = []
    for g in range(SSD_GROUPS):
        cg = cm[:, g * SSD_STATE:(g + 1) * SSD_STATE]
        bg_t = bm[:, g * SSD_STATE:(g + 1) * SSD_STATE].T.astype(BF16)
        cb = jnp.dot(cg, bg_t, preferred_element_type=F32)
        state = st_ref[g]
        y_off = jnp.dot(cg, state.astype(BF16), preferred_element_type=F32) * eax[:, g * gw:(g + 1) * gw]
        parts = []
        for r in range(0, hpg, 2):
            both = []
            for h in (g * hpg + r, g * hpg + r + 1):
                seg = acum[:, h:h + 1] - acum_t[h:h + 1, :]
                m = (cb * jnp.exp(jnp.where(causal, seg, -jnp.inf))).astype(BF16)
                lo = (g * hpg + r) * SSD_HEAD_DIM
                both.append(jnp.dot(m, xdt_b[:, lo:lo + LANES], preferred_element_type=F32))
            parts.append(jnp.where(first_half, both[0], both[1]))
        ys.append(jnp.concatenate(parts, axis=-1) + y_off)
        upd = jnp.dot(bg_t, xw[:, g * gw:(g + 1) * gw], preferred_element_type=F32)
        st_ref[g] = state * eax[L - 1:L, g * gw:(g + 1) * gw] + upd
    y = jnp.concatenate(ys, axis=-1) + dsk_ref[...] * xs
    zf = z_ref[...].astype(F32)
    y = y * _silu(zf)
    normed = []
    for g in range(SSD_GROUPS):
        yg = y[:, g * gw:(g + 1) * gw]
        normed.append(yg * lax.rsqrt(jnp.mean(yg * yg, axis=-1, keepdims=True) + NORM_EPS))
    o_ref[...] = (jnp.concatenate(normed, axis=-1) * gnw_ref[...]).astype(o_ref.dtype)


def _ssd(xbc, z, dt, conv_w, conv_b, dt_bias, a_log, d_skip, gnorm_w, bsz, seq):
    nc = seq // SSD_CHUNK
    conv_ch = xbc.shape[1]
    pad = LANES - SSD_HEADS
    dtb = jnp.pad(dt_bias, (0, pad)).reshape(1, LANES)
    alog = jnp.pad(a_log, (0, pad)).reshape(1, LANES)
    dsk = jnp.repeat(d_skip, SSD_HEAD_DIM).reshape(1, SSD_INNER)
    expand = (jnp.arange(LANES)[:, None] == (jnp.arange(SSD_INNER) // SSD_HEAD_DIM)[None, :]).astype(BF16)
    blk =lambda w: pl.BlockSpec((SSD_CHUNK, w), lambda b, c: (b * nc + c, 0))
    return pl.pallas_call(
        _ssd_kernel, grid=(bsz, nc),
        in_specs=[blk(conv_ch), blk(SSD_INNER), blk(LANES),
                  _resident((CONV_WIDTH, conv_ch)), _resident((1, conv_ch)),
                  _resident((1, LANES)), _resident((1, LANES)),
                  _resident((1, SSD_INNER)), _resident((1, SSD_INNER)),
                  _resident((LANES, SSD_INNER))],
        out_specs=blk(SSD_INNER),
        out_shape=jax.ShapeDtypeStruct((bsz * seq, SSD_INNER), BF16),
        scratch_shapes=[pltpu.VMEM((SSD_CHUNK + 2 * SUBLANES, conv_ch), F32),
                        pltpu.VMEM((SSD_GROUPS, SSD_STATE, SSD_INNER // SSD_GROUPS), F32)],
        compiler_params=_params(), name="ssd_mixer",
    )(xbc, z, dt, conv_w, conv_b.reshape(1, conv_ch), dtb, alog, dsk,
      gnorm_w.reshape(1, SSD_INNER), expand)


def _rope(x, cos, sin_up, sin_dn):
    w = x.shape[1]
    reps = w // LANES
    tile = lambda t: jnp.concatenate([t] * reps, axis=-1) if reps > 1 else t
    half = ROT_DIM // 2
    return (x * tile(cos) + pltpu.roll(x, half, axis=1) * tile(sin_up)
            + pltpu.roll(x, w - half, axis=1) * tile(sin_dn))


def _swa_kernel(sink_ref, q_ref, kvp_ref, kvc_ref, tc_ref, bias_ref, o_ref):
    kvw = KV_HEADS * HEAD_DIM
    g = Q_HEADS // KV_HEADS
    scale = HEAD_DIM ** -0.5 * LOG2_E
    q = _rope(q_ref[...].astype(F32), tc_ref[0], tc_ref[1], tc_ref[2]) * scale
    q_t = q.T.astype(BF16)
    k = jnp.concatenate([kvp_ref[:, :kvw], kvc_ref[:, :kvw]], axis=0)
    v = jnp.concatenate([kvp_ref[:, kvw:], kvc_ref[:, kvw:]], axis=0).astype(F32)
    v_t = v.T.astype(BF16)
    bias = bias_ref[0]
    outs = []
    for kh in range(KV_HEADS):
        kk = k[:, kh * HEAD_DIM:(kh + 1) * HEAD_DIM]
        sc = jnp.concatenate(
            [jnp.dot(kk, q_t[h * HEAD_DIM:(h + 1) * HEAD_DIM, :], preferred_element_type=F32)
             for h in range(kh * g, (kh + 1) * g)], axis=-1) + bias
        sink = jnp.concatenate([jnp.full((1, ATTN_BLOCK), sink_ref[h] * LOG2_E, F32)
                                for h in range(kh * g, (kh + 1) * g)], axis=-1)
        mx = jnp.maximum(jnp.max(sc, axis=0, keepdims=True), sink)
        p = jnp.exp2(sc - mx)
        denom = jnp.sum(p, axis=0, keepdims=True) + jnp.exp2(sink - mx)
        o_t = jnp.dot(v_t[kh * HEAD_DIM:(kh + 1) * HEAD_DIM, :], p.astype(BF16),
                      preferred_element_type=F32) / denom
        outs += [o_t[:, j * ATTN_BLOCK:(j + 1) * ATTN_BLOCK] for j in range(g)]
    o_ref[...] = jnp.concatenate(outs, axis=0).T.astype(o_ref.dtype)


def _swa_bias(g):
    blk = ATTN_BLOCK
    s_idx = jnp.arange(2 * blk)[:, None]
    t_idx = jnp.arange(blk)[None, :]
    band = (s_idx > t_idx) & (s_idx <= t_idx + blk)
    first = band & (s_idx >= blk)
    both = jnp.stack([first, band], axis=0)
    return jnp.tile(jnp.where(both, 0.0, -jnp.inf).astype(F32), (1, 1, g))


def _rope_tables(seq):
    half = ROT_DIM // 2
    inv_freq = ROPE_THETA ** (-2.0 * jnp.arange(half, dtype=F32) / ROT_DIM)
    ang = jnp.arange(seq, dtype=jnp.int32).astype(F32)[:, None] * inv_freq[None, :]
    cos, sin = jnp.cos(ang), jnp.sin(ang)
    ones = jnp.ones((seq, HEAD_DIM - ROT_DIM), F32)
    zeros = jnp.zeros((seq, HEAD_DIM - ROT_DIM), F32)
    zh = jnp.zeros((seq, half), F32)
    cos_h = jnp.concatenate([cos, cos, ones], axis=-1)
    up_h = jnp.concatenate([zh, sin, zeros], axis=-1)
    dn_h = jnp.concatenate([-sin, zh, zeros], axis=-1)
    two = lambda t: jnp.concatenate([t, t], axis=-1)
    return jnp.stack([two(cos_h), two(up_h), two(dn_h)], axis=0)


def _shared_kv_kernel(x_ref, nw_ref, w_ref, t_ref, o_ref):
    kvw = KV_HEADS * HEAD_DIM
    xn = _rms(x_ref[...], nw_ref[...]).astype(BF16)
    kv = jnp.dot(xn, w_ref[...], preferred_element_type=F32)
    k = _rope(kv[:, :kvw], t_ref[0], t_ref[1], t_ref[2])
    o_ref[...] = jnp.concatenate([k, kv[:, kvw:]], axis=-1).astype(o_ref.dtype)


def _shared_kv(h, norm_w, w, tables, seq):
    t, d = h.shape
    tm = min(ROW_TILE, seq)
    per_seq = seq // tm
    return pl.pallas_call(
        _shared_kv_kernel, grid=(t // tm,),
        in_specs=[pl.BlockSpec((tm, d), lambda i: (i, 0)), _resident((1, d)), _resident(w.shape),
                  pl.BlockSpec((3, tm, LANES), lambda i: (0, i % per_seq, 0))],
        out_specs=pl.BlockSpec((tm, w.shape[1]), lambda i: (i, 0)),
        out_shape=jax.ShapeDtypeStruct((t, w.shape[1]), BF16),
        compiler_params=_params(), name="shared_kv",
    )(h, norm_w.reshape(1, d), w, tables)


def _swa(q, kv, sinks, tables, bsz, seq):
    nb = seq // ATTN_BLOCK
    qw = Q_HEADS * HEAD_DIM
    kvw2 = 2 * KV_HEADS * HEAD_DIM
    prev = lambda i: jnp.maximum(i - 1, 0)
    g = Q_HEADS // KV_HEADS
    return pl.pallas_call(
        _swa_kernel, grid=(bsz, nb),
        in_specs=[pl.BlockSpec(memory_space=pltpu.SMEM),
                  pl.BlockSpec((ATTN_BLOCK, qw), lambda b, i: (b * nb + i, 0)),
                  pl.BlockSpec((ATTN_BLOCK, kvw2), lambda b, i: (b * nb + prev(i), 0)),
                  pl.BlockSpec((ATTN_BLOCK, kvw2), lambda b, i: (b * nb + i, 0)),
                  pl.BlockSpec((3, ATTN_BLOCK, LANES), lambda b, i: (0, i, 0)),
                  pl.BlockSpec((1, 2 * ATTN_BLOCK, g * ATTN_BLOCK), lambda b, i: (jnp.minimum(i, 1), 0, 0))],
        out_specs=pl.BlockSpec((ATTN_BLOCK, qw), lambda b, i: (b * nb + i, 0)),
        out_shape=jax.ShapeDtypeStruct((bsz * seq, qw), BF16),
        compiler_params=_params(), name="swa_attention",
    )(sinks, q, kv, kv, tables, _swa_bias(g))


def _router_kernel(h_ref, nw_ref, wr_ref, br_ref, xt_ref, route_ref, cnt_ref):
    tm = h_ref.shape[0]
    xn = _rms(h_ref[...], nw_ref[...])
    x_hi = xn.astype(BF16)
    for j, word in enumerate(_pack_pairs(xn)):
        xt_ref[pl.ds(j, tm, stride=PACK_ROWS), :] = word
    x_lo = (xn - x_hi.astype(F32)).astype(BF16)
    both = jnp.dot(x_hi, wr_ref[...], preferred_element_type=F32)
    logits = (both[:, :LANES] + both[:, LANES:]
              + jnp.dot(x_lo, wr_ref[:, :LANES], preferred_element_type=F32) + br_ref[...])
    lt = logits.T
    neg = -jnp.inf
    grow = lax.broadcasted_iota(jnp.int32, (SUBLANES, tm), 0)
    gl = jnp.where(grow < N_GROUPS, lt[:SUBLANES], neg)
    gmax = jnp.max(gl, axis=0, keepdims=True)
    g_w = 1.0 / jnp.sum(jnp.exp(gl - gmax), axis=0, keepdims=True)
    g_idx = jnp.min(jnp.where(gl == gmax, grow, SUBLANES), axis=0, keepdims=True)
    erow = lax.broadcasted_iota(jnp.int32, (N_EXPERTS, tm), 0)
    el = jnp.where(erow // EXPERTS_PER_GROUP == g_idx, lt[SUBLANES:SUBLANES + N_EXPERTS], neg)
    m1 = jnp.max(el, axis=0, keepdims=True)
    i1 = jnp.min(jnp.where(el == m1, erow, N_EXPERTS), axis=0, keepdims=True)
    el2 = jnp.where(erow == i1, neg, el)
    m2 = jnp.max(el2, axis=0, keepdims=True)
    i2 = jnp.min(jnp.where(el2 == m2, erow, N_EXPERTS), axis=0, keepdims=True)
    d = jnp.exp(m2 - m1)
    w1 = g_w / (1.0 + d)
    w2 = g_w * d / (1.0 + d)
    zero = jnp.zeros((SUBLANES - 4, tm), F32)
    route_ref[...] = jnp.concatenate([i1.astype(F32), i2.astype(F32), w1, w2, zero], axis=0)
    chosen = ((erow == i1) | (erow == i2)).astype(F32)
    cnt_ref[...] = jnp.broadcast_to(jnp.sum(chosen, axis=1, keepdims=True), cnt_ref.shape)


def _router(h, norm_w, rg_w, rg_b, re_w, re_b):
    t, d = h.shape
    tm = min(ROW_TILE, t)
    gap = jnp.zeros((d, SUBLANES - N_GROUPS), F32)
    tail = jnp.zeros((d, LANES - SUBLANES - N_EXPERTS), F32)
    wr = jnp.concatenate([rg_w, gap, re_w, tail], axis=1)
    w_hi = wr.astype(BF16)
    w_lo = (wr - w_hi.astype(F32)).astype(BF16)
    br = jnp.concatenate([rg_b, gap[0], re_b, tail[0]]).reshape(1, LANES)
    return pl.pallas_call(
        _router_kernel, grid=(t // tm,),
        in_specs=[pl.BlockSpec((tm, d), lambda i: (i, 0)), _resident((1, d)),
                  _resident((d, 2 * LANES)), _resident((1, LANES))],
        out_specs=[pl.BlockSpec((tm * PACK_ROWS, LANES), lambda i: (i, 0)),
                   pl.BlockSpec((SUBLANES, tm), lambda i: (0, i)),
                   pl.BlockSpec((N_EXPERTS, LANES), lambda i: (i, 0))],
        out_shape=[jax.ShapeDtypeStruct((t * PACK_ROWS, LANES), jnp.uint32),
                   jax.ShapeDtypeStruct((SUBLANES, t), F32),
                   jax.ShapeDtypeStruct((t // tm * N_EXPERTS, LANES), F32)],
        compiler_params=_params(), name="moe_router",
    )(h, norm_w.reshape(1, d), jnp.concatenate([w_hi, w_lo], axis=1), br)


def _moe_kernel(st_ref, cnt_ref, tok_ref, dst_ref, xs_ref, wgu_ref, wd_ref, o_ref,
                tile_ref, xb_ref, ybuf_ref, state_ref):
    b = pl.program_id(0)
    e = pl.program_id(1)
    stride = MOE_STRIDE
    unroll = SUBLANES
    READY, PEND_BASE, PEND_ROWS = 0, 1, 2

    @pl.when((b == 0) & (e == 0))
    def _():
        ybuf_ref[...] = jnp.zeros_like(ybuf_ref)

    @pl.when(e == 0)
    def _():
        state_ref[READY] = 0
        state_ref[PEND_BASE] = 0
        state_ref[PEND_ROWS] = 0

    start = st_ref[b * N_EXPERTS + e]
    count = cnt_ref[b * N_EXPERTS + e]

    def gather_row(base, r):
        tok = tok_ref[0, 0, base + r]
        src = pl.ds(pl.multiple_of(tok * PACK_ROWS, PACK_ROWS), PACK_ROWS)
        tile_ref[pl.ds(r, PACK_ROWS, stride=stride), :] = xs_ref[src, :]

    def scatter_row(base, r):
        dst = dst_ref[0, 0, base + r]
        o_ref[pl.ds(pl.multiple_of(dst * PACK_ROWS, PACK_ROWS), PACK_ROWS), :] = (
            ybuf_ref[pl.ds(r, PACK_ROWS, stride=stride), :])

    def copy_rows(row_fn, base, lo, hi):
        def group(i, c):
            for u in range(unroll):
                row_fn(base, i * unroll + u)
            return c
        lax.fori_loop(lo // unroll, hi // unroll, group, 0)

    def expert_mlp(rows, next_base, pend_base):
        def run():
            words = [tile_ref[pl.ds(j * stride, rows), :] for j in range(PACK_ROWS)]
            xb_ref[0:rows, :] = _unpack_pairs(words).astype(BF16)
            for r in range(rows):
                gather_row(next_base, r)
                scatter_row(pend_base, r)
            gu = jnp.dot(xb_ref[0:rows, :], wgu_ref[...], preferred_element_type=F32)
            gate, up = gu[:, :D_EXPERT], gu[:, D_EXPERT:]
            he = (_silu(gate) * up).astype(BF16)
            y = jnp.dot(he, wd_ref[...], preferred_element_type=F32)
            for j, word in enumerate(_pack_pairs(y)):
                ybuf_ref[pl.ds(j * stride, rows), :] = word
        return run

    def tile_body(t, carry):
        base = start + t * MOE_ROWS
        rows = jnp.minimum(count - t * MOE_ROWS, MOE_ROWS)
        copy_rows(gather_row, base, state_ref[READY], rows)
        pend_rows = state_ref[PEND_ROWS]
        pend_base = jnp.where(pend_rows > 0, state_ref[PEND_BASE], base)
        copy_rows(scatter_row, pend_base, jnp.minimum(rows, pend_rows), pend_rows)
        next_base = base + rows
        lax.switch(rows // MOE_UNIT - 1, [expert_mlp(MOE_UNIT * (k + 1), next_base, pend_base)
                                          for k in range(MOE_CLASSES)])
        state_ref[READY] = rows
        state_ref[PEND_BASE] = base
        state_ref[PEND_ROWS] = rows
        return carry

    lax.fori_loop(0, (count + MOE_ROWS - 1) // MOE_ROWS, tile_body, 0)

    @pl.when(e == N_EXPERTS - 1)
    def _():
        copy_rows(scatter_row, state_ref[PEND_BASE], 0, state_ref[PEND_ROWS])


def _moe_plan(route, tile_counts, nb, m):
    pick = lambda r: route[r].reshape(nb, m)
    e_idx = jnp.concatenate([pick(0), pick(1)], axis=1).astype(jnp.int32)
    counts = tile_counts[:, 0].reshape(nb, -1, N_EXPERTS).sum(axis=1).astype(jnp.int32)
    padded = (counts + MOE_UNIT - 1) // MOE_UNIT * MOE_UNIT
    starts = jnp.cumsum(padded, axis=1) - padded
    spare = MOE_UNIT - 1
    needed = jnp.arange(spare, dtype=jnp.int32)[None, None, :] < (padded - counts)[:, :, None]
    expert = jnp.arange(N_EXPERTS, dtype=jnp.int32)[None, :, None]
    dummy_keys = jnp.where(needed, 2 * expert + 1, 2 * N_EXPERTS).reshape(nb, N_EXPERTS * spare)
    order = jnp.argsort(jnp.concatenate([2 * e_idx, dummy_keys], axis=1), axis=1, stable=True).astype(jnp.int32)
    real = order < 2 * m
    tail = ((0, 0), (0, MOE_ROWS + N_EXPERTS))
    tok_list = jnp.pad(jnp.where(real, order % m, 0), tail)
    dst_list = jnp.pad(jnp.where(real, order, 2 * m), tail, constant_values=2 * m)
    as_list = lambda a: a.reshape(nb, 1, a.shape[-1])
    return starts.reshape(-1), padded.reshape(-1), as_list(tok_list), as_list(dst_list)


def _moe(xt, route, tile_counts, w_gu, w_down, t, d):
    m = min(MOE_BLOCK, t)
    nb = t // m
    starts, padded, tok_list, dst_list = _moe_plan(route, tile_counts, nb, m)
    length = tok_list.shape[-1]
    slots = 2 * m + PACK
    grid_spec = pltpu.PrefetchScalarGridSpec(
        num_scalar_prefetch=2, grid=(nb, N_EXPERTS),
        in_specs=[pl.BlockSpec((1, 1, length), lambda b, e, *_: (b, 0, 0), memory_space=pltpu.SMEM),
                  pl.BlockSpec((1, 1, length), lambda b, e, *_: (b, 0, 0), memory_space=pltpu.SMEM),
                  pl.BlockSpec((m * PACK_ROWS, LANES), lambda b, e, *_: (b, 0), pipeline_mode=pl.Buffered(1)),
                  pl.BlockSpec((None, d, 2 * D_EXPERT), lambda b, e, *_: (e, 0, 0)),
                  pl.BlockSpec((None, D_EXPERT, d), lambda b, e, *_: (e, 0, 0))],
        out_specs=pl.BlockSpec((None, slots * PACK_ROWS, LANES), lambda b, e, *_: (b, 0, 0)),
        scratch_shapes=[pltpu.VMEM((PACK_ROWS * MOE_STRIDE, LANES), jnp.uint32),
                        pltpu.VMEM((MOE_ROWS, d), BF16),
                        pltpu.VMEM((PACK_ROWS * MOE_STRIDE, LANES), jnp.uint32),
                        pltpu.SMEM((3,), jnp.int32)])
    return pl.pallas_call(
        _moe_kernel, grid_spec=grid_spec,
        out_shape=jax.ShapeDtypeStruct((nb, slots * PACK_ROWS, LANES), jnp.uint32),
        compiler_params=_params(), name="moe_experts",
    )(starts, padded, tok_list, dst_list, xt, w_gu, w_down)


def _combine_kernel(h_ref, y0_ref, y1_ref, route_ref, nw_ref, o_ref, *, final_norm):
    tm = h_ref.shape[0]
    token_rows = lambda ref: _unpack_pairs([ref[pl.ds(j, tm, stride=PACK_ROWS), :] for j in range(PACK_ROWS)])
    gates = route_ref[...].T
    out = h_ref[...] + gates[:, 2:3] * token_rows(y0_ref) + gates[:, 3:4] * token_rows(y1_ref)
    if final_norm:
        out = _rms(out, nw_ref[...])
    o_ref[...] = out


def _combine(h, y, route, norm_w, final_norm, name):
    t, d = h.shape
    tm = min(ROW_TILE, t)
    m = min(MOE_BLOCK, t)
    per_block = m // tm

    def slot(s):
        return pl.BlockSpec((None, tm * PACK_ROWS, LANES),
                            lambda i: (i // per_block, s * per_block + i % per_block, 0))

    return pl.pallas_call(
        functools.partial(_combine_kernel, final_norm=final_norm), grid=(t // tm,),
        in_specs=[pl.BlockSpec((tm, d), lambda i: (i, 0)),
                  slot(0), slot(1),
                  pl.BlockSpec((SUBLANES, tm), lambda i: (0, i)),
                  _resident((1, d))],
        out_specs=pl.BlockSpec((tm, d), lambda i: (i, 0)),
        out_shape=jax.ShapeDtypeStruct((t, d), F32),
        compiler_params=_params(), name=name,
    )(h, y, y, route, norm_w.reshape(1, d))


def _combine_proj_kernel(h_ref, y0_ref, y1_ref, route_ref, kvn_ref, bn_ref, wkv_ref, wq_ref, wqm_ref,
                         t_ref, ho_ref, kv_ref, q_ref, qm_ref):
    tm = h_ref.shape[0]
    kvw = KV_HEADS * HEAD_DIM
    token_rows = lambda ref: _unpack_pairs([ref[pl.ds(j, tm, stride=PACK_ROWS), :] for j in range(PACK_ROWS)])
    gates = route_ref[...].T
    out = h_ref[...] + gates[:, 2:3] * token_rows(y0_ref) + gates[:, 3:4] * token_rows(y1_ref)
    ho_ref[...] = out
    unit = out * lax.rsqrt(jnp.mean(out * out, axis=-1, keepdims=True) + NORM_EPS)
    kv = jnp.dot((unit * kvn_ref[...]).astype(BF16), wkv_ref[...], preferred_element_type=F32)
    k = _rope(kv[:, :kvw], t_ref[0], t_ref[1], t_ref[2])
    kv_ref[...] = jnp.concatenate([k, kv[:, kvw:]], axis=-1).astype(kv_ref.dtype)
    xn = (unit * bn_ref[...]).astype(BF16)
    for c0 in range(0, wq_ref.shape[1], 512):
        q_ref[:, c0:c0 + 512] = jnp.dot(xn, wq_ref[:, c0:c0 + 512], preferred_element_type=F32).astype(q_ref.dtype)
    qm_ref[...] = jnp.dot(xn, wqm_ref[...], preferred_element_type=F32).astype(qm_ref.dtype)


def _combine_proj(h, y, route, kv_norm_w, b_norm_w, w_kv, w_q, w_qm, tables, seq):
    t, d = h.shape
    tm = min(ROW_TILE, seq)
    per_block = min(MOE_BLOCK, t) // tm
    per_seq = seq // tm
    slot = lambda s: pl.BlockSpec((None, tm * PACK_ROWS, LANES),
                                  lambda i: (i // per_block, s * per_block + i % per_block, 0))
    rows = lambda w: pl.BlockSpec((tm, w), lambda i: (i, 0))
    return pl.pallas_call(
        _combine_proj_kernel, grid=(t // tm,),
        in_specs=[rows(d), slot(0), slot(1),
                  pl.BlockSpec((SUBLANES, tm), lambda i: (0, i)),
                  _resident((1, d)), _resident((1, d)),
                  _resident(w_kv.shape), _resident(w_q.shape), _resident(w_qm.shape),
                  pl.BlockSpec((3, tm, LANES), lambda i: (0, i % per_seq, 0))],
        out_specs=[rows(d), rows(w_kv.shape[1]), rows(w_q.shape[1]), rows(w_qm.shape[1])],
        out_shape=[jax.ShapeDtypeStruct((t, d), F32),
                   jax.ShapeDtypeStruct((t, w_kv.shape[1]), BF16),
                   jax.ShapeDtypeStruct((t, w_q.shape[1]), BF16),
                   jax.ShapeDtypeStruct((t, w_qm.shape[1]), BF16)],
        compiler_params=_params(), name="moe_combine_swa_in_proj",
    )(h, y, y, route, kv_norm_w.reshape(1, d), b_norm_w.reshape(1, d), w_kv, w_q, w_qm, tables)


def _moe_experts(h, norm_w, rg_w, rg_b, re_w, re_b, w_gate, w_up, w_down):
    t, d = h.shape
    xt, route, tile_counts = _router(h, norm_w, rg_w, rg_b, re_w, re_b)
    w_gu = jnp.concatenate([w_gate, w_up], axis=-1).astype(BF16)
    return _moe(xt, route, tile_counts, w_gu, w_down.astype(BF16), t, d), route


def kernel(x, mem, a_norm_w, a_in_w, a_conv_w, a_conv_b, a_dt_bias, a_a_log, a_d_skip, a_gnorm_w, a_out_w, kv_norm_w, w_kv, b_norm_w, b_in_w, b_sinks, b_out_w, mem_norm_w, mem_w_kv, ffn_norm_w, router_group_w, router_group_b, router_expert_w, router_expert_b, w_gate, w_up, w_down, final_norm_w):
    bsz, seq, d = x.shape
    mem_len = mem.shape[1]
    t = bsz * seq
    h = x.reshape(t, d)
    mem2 = mem.reshape(bsz * mem_len, d)
    conv_ch = a_conv_w.shape[-1]
    n_a = a_in_w.shape[0]
    n_layers = n_a + b_in_w.shape[0]
    qw = Q_HEADS * HEAD_DIM
    tables = _rope_tables(seq)
    kv_sh = q = q_mem = None
    for layer in range(n_layers):
        (mem_kv,) = _norm_matmul(mem2, mem_norm_w[layer], [mem_w_kv[layer].astype(BF16)], [BF16],
                                 name=f"mem_kv_{layer}")
        if layer < n_a:
            i = layer
            w_in = a_in_w[i]
            o_x, o_dt, o_q = SSD_INNER, SSD_INNER + conv_ch, SSD_INNER + conv_ch + SSD_HEADS
            w_dt = jnp.pad(w_in[:, o_dt:o_q], ((0, 0), (0, LANES - SSD_HEADS)))
            z, xbc, q_mem, dt = _norm_matmul(
                h, a_norm_w[i],
                [w_in[:, :o_x].astype(BF16), w_in[:, o_x:o_dt].astype(BF16),
                 w_in[:, o_q:].astype(BF16), w_dt.astype(BF16)],
                [BF16, BF16, BF16, F32], name=f"ssd_in_proj_{i}")
            y = _ssd(xbc, z, dt, a_conv_w[i], a_conv_b[i], a_dt_bias[i], a_a_log[i], a_d_skip[i],
                     a_gnorm_w[i], bsz, seq)
            m_att = _mem_attn(q_mem, mem_kv, bsz, seq, mem_len, name=f"mem_attn_{layer}")
            w_out = a_out_w[i].astype(BF16)
            h = _mix_out(y, m_att, w_out[:SSD_INNER], w_out[SSD_INNER:], h, name=f"ssd_out_proj_{i}")
        else:
            j = layer - n_a
            if q is None:
                if j == 0:
                    kv_sh = _shared_kv(h, kv_norm_w, w_kv.astype(BF16), tables, seq)
                w_in = b_in_w[j].astype(BF16)
                q, q_mem = _norm_matmul(h, b_norm_w[j], [w_in[:, :qw], w_in[:, qw:]], [BF16, BF16],
                                        name=f"swa_in_proj_{j}")
            att = _swa(q, kv_sh, b_sinks[j], tables, bsz, seq)
            m_att = _mem_attn(q_mem, mem_kv, bsz, seq, mem_len, name=f"mem_attn_{layer}")
            w_out = b_out_w[j].astype(BF16)
            h = _mix_out(att, m_att, w_out[:qw], w_out[qw:], h, name=f"swa_out_proj_{j}")
            q = None
        y_moe, route = _moe_experts(h, ffn_norm_w[layer], router_group_w[layer], router_group_b[layer],
                                    router_expert_w[layer], router_expert_b[layer],
                                    w_gate[layer], w_up[layer], w_down[layer])
        if layer + 1 == n_a and layer + 1 < n_layers:
            w_in = b_in_w[0].astype(BF16)
            h, kv_sh, q, q_mem = _combine_proj(h, y_moe, route, kv_norm_w, b_norm_w[0], w_kv.astype(BF16),
                                               w_in[:, :qw], w_in[:, qw:], tables, seq)
        else:
            last = layer == n_layers - 1
            h = _combine(h, y_moe, route, final_norm_w, last, name=f"moe_combine_{layer}")
    return h.reshape(bsz, seq, d)
```

```python
import functools

import jax
import jax.numpy as jnp
from jax import lax
from jax.experimental import pallas as pl
from jax.experimental.pallas import tpu as pltpu

F32 = jnp.float32
BF16 = jnp.bfloat16
NORM_EPS = 1e-6
LOG2_E = 1.4426950408889634

LANES = 128
SUBLANES = 8
VMEM_LIMIT = 56 * 1024 * 1024

MEM_HEADS = 4
MEM_HEAD_DIM = 128
MEM_WIDTH = MEM_HEADS * MEM_HEAD_DIM
SSD_HEAD_DIM = 64
SSD_HEADS = 24
SSD_INNER = SSD_HEADS * SSD_HEAD_DIM
SSD_GROUPS = 4
SSD_STATE = 128
SSD_CHUNK = 128
CONV_WIDTH = 4
HEAD_DIM = 64
Q_HEADS = 16
KV_HEADS = 4
ATTN_BLOCK = 128
ROT_DIM = 16
ROPE_THETA = 500000.0
N_GROUPS = 4
EXPERTS_PER_GROUP = 8
N_EXPERTS = 32
D_EXPERT = 512

ROW_TILE = 512
MOE_BLOCK = 4096
MOE_UNIT = 64
MOE_CLASSES = 8
MOE_ROWS = MOE_UNIT * MOE_CLASSES
MOE_STRIDE = MOE_ROWS + SUBLANES
PACK = 2
PACK_ROWS = SUBLANES // PACK
HIGH_HALF = 0xFFFF0000


def _params():
    return pltpu.CompilerParams(vmem_limit_bytes=VMEM_LIMIT)


def _resident(shape):
    nd = len(shape)
    return pl.BlockSpec(shape, lambda *_: (0,) * nd, pipeline_mode=pl.Buffered(1))


def _rms(x, w):
    ms = jnp.mean(x * x, axis=-1, keepdims=True)
    return x * lax.rsqrt(ms + NORM_EPS) * w


def _silu(x):
    return x * (1.0 / (1.0 + jnp.exp(-x)))


def _pack_pairs(x):
    half = x.shape[1] // PACK
    bits = lax.bitcast_convert_type(x.astype(BF16).astype(F32), jnp.uint32)
    return [(bits[:, j * LANES:(j + 1) * LANES] >> 16)
            | (bits[:, half + j * LANES:half + (j + 1) * LANES] & jnp.uint32(HIGH_HALF))
            for j in range(half // LANES)]


def _unpack_pairs(words):
    lo = [lax.bitcast_convert_type(w << 16, F32) for w in words]
    hi = [lax.bitcast_convert_type(w & jnp.uint32(HIGH_HALF), F32) for w in words]
    return jnp.concatenate(lo + hi, axis=-1)


def _norm_matmul_kernel(x_ref, nw_ref, w_ref, *o_refs):
    xn = _rms(x_ref[...], nw_ref[...]).astype(BF16)
    lo = 0
    for o_ref in o_refs:
        n = o_ref.shape[1]
        for c0 in range(0, n, 512):
            cw = min(512, n - c0)
            acc = jnp.dot(xn, w_ref[:, lo + c0:lo + c0 + cw], preferred_element_type=F32)
            o_ref[:, c0:c0 + cw] = acc.astype(o_ref.dtype)
        lo += n


def _norm_matmul(x, norm_w, w, widths, out_dtypes, name):
    t, d = x.shape
    tm = min(ROW_TILE, t)
    assert sum(widths) == w.shape[1]
    return pl.pallas_call(
        _norm_matmul_kernel, grid=(t // tm,),
        in_specs=[pl.BlockSpec((tm, d), lambda i: (i, 0)), _resident((1, d)), _resident(w.shape)],
        out_specs=[pl.BlockSpec((tm, n), lambda i: (i, 0)) for n in widths],
        out_shape=[jax.ShapeDtypeStruct((t, n), dt) for n, dt in zip(widths, out_dtypes)],
        compiler_params=_params(), name=name,
    )(x, norm_w.reshape(1, d), w)


def _mix_out_route_kernel(a1_ref, a2_ref, w1_ref, w2_ref, r_ref, nw_ref, wr_ref, br_ref,
                          o_ref, xt_ref, route_ref, cnt_ref):
    acc = jnp.dot(a1_ref[...], w1_ref[...], preferred_element_type=F32)
    acc = acc + jnp.dot(a2_ref[...], w2_ref[...], preferred_element_type=F32)
    h = r_ref[...] + acc
    o_ref[...] = h
    _route_tokens(h, nw_ref, wr_ref, br_ref, xt_ref, route_ref, cnt_ref)


def _mix_out_route(a1, a2, w1, w2, res, norm_w, rg_w, rg_b, re_w, re_b, name):
    t, d = res.shape
    tm = min(ROW_TILE, t)
    wr, br = _router_weights(rg_w, rg_b, re_w, re_b)
    rows = lambda w: pl.BlockSpec((tm, w), lambda i: (i, 0))
    return pl.pallas_call(
        _mix_out_route_kernel, grid=(t // tm,),
        in_specs=[rows(a1.shape[1]), rows(a2.shape[1]), _resident(w1.shape), _resident(w2.shape), rows(d),
                  _resident((1, d)), _resident(wr.shape), _resident(br.shape)],
        out_specs=[rows(d),
                   pl.BlockSpec((tm * PACK_ROWS, LANES), lambda i: (i, 0)),
                   pl.BlockSpec((SUBLANES, tm), lambda i: (0, i)),
                   pl.BlockSpec((N_EXPERTS, LANES), lambda i: (i, 0))],
        out_shape=[jax.ShapeDtypeStruct((t, d), F32),
                   jax.ShapeDtypeStruct((t * PACK_ROWS, LANES), jnp.uint32),
                   jax.ShapeDtypeStruct((SUBLANES, t), F32),
                   jax.ShapeDtypeStruct((t // tm * N_EXPERTS, LANES), F32)],
        compiler_params=_params(), name=name,
    )(a1, a2, w1, w2, res, norm_w.reshape(1, d), wr, br)


def _mem_attn_kernel(q_ref, kv_ref, o_ref):
    scale = MEM_HEAD_DIM ** -0.5
    outs = []
    for h in range(MEM_HEADS):
        lo = h * MEM_HEAD_DIM
        q = q_ref[:, lo:lo + MEM_HEAD_DIM]
        k = kv_ref[:, lo:lo + MEM_HEAD_DIM]
        v = kv_ref[:, MEM_WIDTH + lo:MEM_WIDTH + lo + MEM_HEAD_DIM]
        s = lax.dot_general(q, k, (((1,), (1,)), ((), ())), preferred_element_type=F32) * scale
        p = jnp.exp(s - jnp.max(s, axis=-1, keepdims=True))
        p = p / jnp.sum(p, axis=-1, keepdims=True)
        outs.append(jnp.dot(p.astype(BF16), v, preferred_element_type=F32))
    o_ref[...] = jnp.concatenate(outs, axis=-1).astype(o_ref.dtype)


def _mem_attn(q, kv, bsz, seq, mem_len, name):
    tq = min(ROW_TILE, seq)
    nq = seq // tq
    return pl.pallas_call(
        _mem_attn_kernel, grid=(bsz, nq),
        in_specs=[pl.BlockSpec((tq, MEM_WIDTH), lambda b, i: (b * nq + i, 0)),
                  pl.BlockSpec((mem_len, 2 * MEM_WIDTH), lambda b, i: (b, 0))],
        out_specs=pl.BlockSpec((tq, MEM_WIDTH), lambda b, i: (b * nq + i, 0)),
        out_shape=jax.ShapeDtypeStruct((bsz * seq, MEM_WIDTH), BF16),
        compiler_params=_params(), name=name,
    )(q, kv)


def _ssd_kernel(xbc_ref, z_ref, dt_ref, cw_ref, cb_ref, dtb_ref, alog_ref, dsk_ref, gnw_ref,
                expand_ref, o_ref, ext_ref, st_ref):
    c = pl.program_id(1)
    L = SSD_CHUNK
    gw = SSD_INNER // SSD_GROUPS
    hpg = SSD_HEADS // SSD_GROUPS

    @pl.when(c == 0)
    def _():
        ext_ref[0:SUBLANES, :] = jnp.zeros((SUBLANES, ext_ref.shape[1]), F32)
        st_ref[...] = jnp.zeros_like(st_ref)

    @pl.when(c > 0)
    def _():
        ext_ref[0:SUBLANES, :] = ext_ref[L:L + SUBLANES, :]

    ext_ref[SUBLANES:SUBLANES + L, :] = xbc_ref[...].astype(F32)
    conv = cb_ref[...]
    for k in range(CONV_WIDTH):
        conv = conv + cw_ref[k:k + 1, :] * ext_ref[pl.ds(SUBLANES - (CONV_WIDTH - 1) + k, L), :]
    xbc = _silu(conv)
    xs = xbc[:, :SSD_INNER]
    bm = xbc[:, SSD_INNER:SSD_INNER + SSD_GROUPS * SSD_STATE]
    cm = xbc[:, SSD_INNER + SSD_GROUPS * SSD_STATE:].astype(BF16)

    pre = dt_ref[...] + dtb_ref[...]
    dt = jnp.maximum(pre, 0.0) + jnp.log(1.0 + jnp.exp(-jnp.abs(pre)))
    a = -jnp.exp(alog_ref[...])
    adt = dt * a
    row = lax.broadcasted_iota(jnp.int32, (L, L), 0)
    col = lax.broadcasted_iota(jnp.int32, (L, L), 1)
    causal = row >= col
    acum = jnp.dot(causal.astype(F32), adt, preferred_element_type=F32,
                   precision=lax.Precision.HIGHEST)
    acum_t = acum.T
    ea = jnp.exp(acum)
    te = jnp.exp(acum[L - 1:L, :] - acum)
    widen = lambda cols: jnp.dot(cols.astype(BF16), expand_ref[...], preferred_element_type=F32)
    dtx, eax, tex = widen(dt), widen(ea), widen(te)
    xdt = xs * dtx
    xdt_b = xdt.astype(BF16)
    xw = (xdt * tex).astype(BF16)

    first_half = col < SSD_HEAD_DIM
    ys = []
    for g in range(SSD_GROUPS):
        cg = cm[:, g * SSD_STATE:(g + 1) * SSD_STATE]
        bg_t = bm[:, g * SSD_STATE:(g + 1) * SSD_STATE].T.astype(BF16)
        cb = jnp.dot(cg, bg_t, preferred_element_type=F32)
        state = st_ref[g]
        y_off = jnp.dot(cg, state.astype(BF16), preferred_element_type=F32) * eax[:, g * gw:(g + 1) * gw]
        parts = []
        for r in range(0, hpg, 2):
            both = []
            for h in (g * hpg + r, g * hpg + r + 1):
                seg = acum[:, h:h + 1] - acum_t[h:h + 1, :]
                m = (cb * jnp.exp(jnp.where(causal, seg, -jnp.inf))).astype(BF16)
                lo = (g * hpg + r) * SSD_HEAD_DIM
                both.append(jnp.dot(m, xdt_b[:, lo:lo + LANES], preferred_element_type=F32))
            parts.append(jnp.where(first_half, both[0], both[1]))
        ys.append(jnp.concatenate(parts, axis=-1) + y_off)
        upd = jnp.dot(bg_t, xw[:, g * gw:(g + 1) * gw], preferred_element_type=F32)
        st_ref[g] = state * eax[L - 1:L, g * gw:(g + 1) * gw] + upd
    y = jnp.concatenate(ys, axis=-1) + dsk_ref[...] * xs
    zf = z_ref[...].astype(F32)
    y = y * _silu(zf)
    normed = []
    for g in range(SSD_GROUPS):
        yg = y[:, g * gw:(g + 1) * gw]
        normed.append(yg * lax.rsqrt(jnp.mean(yg * yg, axis=-1, keepdims=True) + NORM_EPS))
    o_ref[...] = (jnp.concatenate(normed, axis=-1) * gnw_ref[...]).astype(o_ref.dtype)


def _ssd(xbc, z, dt, conv_w, conv_b, dt_bias, a_log, d_skip, gnorm_w, bsz, seq):
    nc = seq // SSD_CHUNK
    conv_ch = xbc.shape[1]
    pad = LANES - SSD_HEADS
    dtb = jnp.pad(dt_bias, (0, pad)).reshape(1, LANES)
    alog = jnp.pad(a_log, (0, pad)).reshape(1, LANES)
    dsk = jnp.repeat(d_skip, SSD_HEAD_DIM).reshape(1, SSD_INNER)
    expand = (jnp.arange(LANES)[:, None] == (jnp.arange(SSD_INNER) // SSD_HEAD_DIM)[None, :]).astype(BF16)
    blk =lambda w: pl.BlockSpec((SSD_CHUNK, w), lambda b, c: (b * nc + c, 0))
    return pl.pallas_call(
        _ssd_kernel, grid=(bsz, nc),
        in_specs=[blk(conv_ch), blk(SSD_INNER), blk(LANES),
                  _resident((CONV_WIDTH, conv_ch)), _resident((1, conv_ch)),
                  _resident((1, LANES)), _resident((1, LANES)),
                  _resident((1, SSD_INNER)), _resident((1, SSD_INNER)),
                  _resident((LANES, SSD_INNER))],
        out_specs=blk(SSD_INNER),
        out_shape=jax.ShapeDtypeStruct((bsz * seq, SSD_INNER), BF16),
        scratch_shapes=[pltpu.VMEM((SSD_CHUNK + 2 * SUBLANES, conv_ch), F32),
                        pltpu.VMEM((SSD_GROUPS, SSD_STATE, SSD_INNER // SSD_GROUPS), F32)],
        compiler_params=_params(), name="ssd_mixer",
    )(xbc, z, dt, conv_w, conv_b.reshape(1, conv_ch), dtb, alog, dsk,
      gnorm_w.reshape(1, SSD_INNER), expand)


def _rope(x, cos, sin_up, sin_dn):
    w = x.shape[1]
    reps = w // LANES
    tile = lambda t: jnp.concatenate([t] * reps, axis=-1) if reps > 1 else t
    half = ROT_DIM // 2
    return (x * tile(cos) + pltpu.roll(x, half, axis=1) * tile(sin_up)
            + pltpu.roll(x, w - half, axis=1) * tile(sin_dn))


def _swa_kernel(sink_ref, q_ref, kvp_ref, kvc_ref, tc_ref, bias_ref, o_ref):
    kvw = KV_HEADS * HEAD_DIM
    g = Q_HEADS // KV_HEADS
    scale = HEAD_DIM ** -0.5 * LOG2_E
    q = _rope(q_ref[...].astype(F32), tc_ref[0], tc_ref[1], tc_ref[2]) * scale
    q_t = q.T.astype(BF16)
    k = jnp.concatenate([kvp_ref[:, :kvw], kvc_ref[:, :kvw]], axis=0)
    v = jnp.concatenate([kvp_ref[:, kvw:], kvc_ref[:, kvw:]], axis=0).astype(F32)
    v_t = v.T.astype(BF16)
    bias = bias_ref[0]
    outs = []
    for kh in range(KV_HEADS):
        kk = k[:, kh * HEAD_DIM:(kh + 1) * HEAD_DIM]
        sc = jnp.concatenate(
            [jnp.dot(kk, q_t[h * HEAD_DIM:(h + 1) * HEAD_DIM, :], preferred_element_type=F32)
             for h in range(kh * g, (kh + 1) * g)], axis=-1) + bias
        sink = jnp.concatenate([jnp.full((1, ATTN_BLOCK), sink_ref[h] * LOG2_E, F32)
                                for h in range(kh * g, (kh + 1) * g)], axis=-1)
        mx = jnp.maximum(jnp.max(sc, axis=0, keepdims=True), sink)
        p = jnp.exp2(sc - mx)
        denom = jnp.sum(p, axis=0, keepdims=True) + jnp.exp2(sink - mx)
        o_t = jnp.dot(v_t[kh * HEAD_DIM:(kh + 1) * HEAD_DIM, :], p.astype(BF16),
                      preferred_element_type=F32) / denom
        outs += [o_t[:, j * ATTN_BLOCK:(j + 1) * ATTN_BLOCK] for j in range(g)]
    o_ref[...] = jnp.concatenate(outs, axis=0).T.astype(o_ref.dtype)


def _swa_bias(g):
    blk = ATTN_BLOCK
    s_idx = jnp.arange(2 * blk)[:, None]
    t_idx = jnp.arange(blk)[None, :]
    band = (s_idx > t_idx) & (s_idx <= t_idx + blk)
    first = band & (s_idx >= blk)
    both = jnp.stack([first, band], axis=0)
    return jnp.tile(jnp.where(both, 0.0, -jnp.inf).astype(F32), (1, 1, g))


def _rope_tables(seq):
    half = ROT_DIM // 2
    inv_freq = ROPE_THETA ** (-2.0 * jnp.arange(half, dtype=F32) / ROT_DIM)
    ang = jnp.arange(seq, dtype=jnp.int32).astype(F32)[:, None] * inv_freq[None, :]
    cos, sin = jnp.cos(ang), jnp.sin(ang)
    ones = jnp.ones((seq, HEAD_DIM - ROT_DIM), F32)
    zeros = jnp.zeros((seq, HEAD_DIM - ROT_DIM), F32)
    zh = jnp.zeros((seq, half), F32)
    cos_h = jnp.concatenate([cos, cos, ones], axis=-1)
    up_h = jnp.concatenate([zh, sin, zeros], axis=-1)
    dn_h = jnp.concatenate([-sin, zh, zeros], axis=-1)
    two = lambda t: jnp.concatenate([t, t], axis=-1)
    return jnp.stack([two(cos_h), two(up_h), two(dn_h)], axis=0)


def _shared_kv_kernel(x_ref, nw_ref, w_ref, t_ref, o_ref):
    kvw = KV_HEADS * HEAD_DIM
    xn = _rms(x_ref[...], nw_ref[...]).astype(BF16)
    kv = jnp.dot(xn, w_ref[...], preferred_element_type=F32)
    k = _rope(kv[:, :kvw], t_ref[0], t_ref[1], t_ref[2])
    o_ref[...] = jnp.concatenate([k, kv[:, kvw:]], axis=-1).astype(o_ref.dtype)


def _shared_kv(h, norm_w, w, tables, seq):
    t, d = h.shape
    tm = min(ROW_TILE, seq)
    per_seq = seq // tm
    return pl.pallas_call(
        _shared_kv_kernel, grid=(t // tm,),
        in_specs=[pl.BlockSpec((tm, d), lambda i: (i, 0)), _resident((1, d)), _resident(w.shape),
                  pl.BlockSpec((3, tm, LANES), lambda i: (0, i % per_seq, 0))],
        out_specs=pl.BlockSpec((tm, w.shape[1]), lambda i: (i, 0)),
        out_shape=jax.ShapeDtypeStruct((t, w.shape[1]), BF16),
        compiler_params=_params(), name="shared_kv",
    )(h, norm_w.reshape(1, d), w, tables)


def _swa(q, kv, sinks, tables, bsz, seq):
    nb = seq // ATTN_BLOCK
    qw = Q_HEADS * HEAD_DIM
    kvw2 = 2 * KV_HEADS * HEAD_DIM
    prev = lambda i: jnp.maximum(i - 1, 0)
    g = Q_HEADS // KV_HEADS
    return pl.pallas_call(
        _swa_kernel, grid=(bsz, nb),
        in_specs=[pl.BlockSpec(memory_space=pltpu.SMEM),
                  pl.BlockSpec((ATTN_BLOCK, qw), lambda b, i: (b * nb + i, 0)),
                  pl.BlockSpec((ATTN_BLOCK, kvw2), lambda b, i: (b * nb + prev(i), 0)),
                  pl.BlockSpec((ATTN_BLOCK, kvw2), lambda b, i: (b * nb + i, 0)),
                  pl.BlockSpec((3, ATTN_BLOCK, LANES), lambda b, i: (0, i, 0)),
                  pl.BlockSpec((1, 2 * ATTN_BLOCK, g * ATTN_BLOCK), lambda b, i: (jnp.minimum(i, 1), 0, 0))],
        out_specs=pl.BlockSpec((ATTN_BLOCK, qw), lambda b, i: (b * nb + i, 0)),
        out_shape=jax.ShapeDtypeStruct((bsz * seq, qw), BF16),
        compiler_params=_params(), name="swa_attention",
    )(sinks, q, kv, kv, tables, _swa_bias(g))


def _route_tokens(h, nw_ref, wr_ref, br_ref, xt_ref, route_ref, cnt_ref):
    tm = h.shape[0]
    xn = _rms(h, nw_ref[...])
    x_hi = xn.astype(BF16)
    for j, word in enumerate(_pack_pairs(xn)):
        xt_ref[pl.ds(j, tm, stride=PACK_ROWS), :] = word
    x_lo = (xn - x_hi.astype(F32)).astype(BF16)
    both = jnp.dot(x_hi, wr_ref[...], preferred_element_type=F32)
    logits = (both[:, :LANES] + both[:, LANES:]
              + jnp.dot(x_lo, wr_ref[:, :LANES], preferred_element_type=F32) + br_ref[...])
    lt = logits.T
    neg = -jnp.inf
    grow = lax.broadcasted_iota(jnp.int32, (SUBLANES, tm), 0)
    gl = jnp.where(grow < N_GROUPS, lt[:SUBLANES], neg)
    gmax = jnp.max(gl, axis=0, keepdims=True)
    g_w = 1.0 / jnp.sum(jnp.exp(gl - gmax), axis=0, keepdims=True)
    g_idx = jnp.min(jnp.where(gl == gmax, grow, SUBLANES), axis=0, keepdims=True)
    erow = lax.broadcasted_iota(jnp.int32, (N_EXPERTS, tm), 0)
    el = jnp.where(erow // EXPERTS_PER_GROUP == g_idx, lt[SUBLANES:SUBLANES + N_EXPERTS], neg)
    m1 = jnp.max(el, axis=0, keepdims=True)
    i1 = jnp.min(jnp.where(el == m1, erow, N_EXPERTS), axis=0, keepdims=True)
    el2 = jnp.where(erow == i1, neg, el)
    m2 = jnp.max(el2, axis=0, keepdims=True)
    i2 = jnp.min(jnp.where(el2 == m2, erow, N_EXPERTS), axis=0, keepdims=True)
    d = jnp.exp(m2 - m1)
    w1 = g_w / (1.0 + d)
    w2 = g_w * d / (1.0 + d)
    zero = jnp.zeros((SUBLANES - 4, tm), F32)
    route_ref[...] = jnp.concatenate([i1.astype(F32), i2.astype(F32), w1, w2, zero], axis=0)
    chosen = ((erow == i1) | (erow == i2)).astype(F32)
    cnt_ref[...] = jnp.broadcast_to(jnp.sum(chosen, axis=1, keepdims=True), cnt_ref.shape)


def _router_weights(rg_w, rg_b, re_w, re_b):
    d = rg_w.shape[0]
    gap = jnp.zeros((d, SUBLANES - N_GROUPS), F32)
    tail = jnp.zeros((d, LANES - SUBLANES - N_EXPERTS), F32)
    wr = jnp.concatenate([rg_w, gap, re_w, tail], axis=1)
    w_hi = wr.astype(BF16)
    w_lo = (wr - w_hi.astype(F32)).astype(BF16)
    br = jnp.concatenate([rg_b, gap[0], re_b, tail[0]]).reshape(1, LANES)
    return jnp.concatenate([w_hi, w_lo], axis=1), br


def _moe_kernel(st_ref, cnt_ref, tok_ref, dst_ref, xs_ref, wgu_ref, wd_ref, o_ref,
                tile_ref, xb_ref, ybuf_ref, state_ref):
    b = pl.program_id(0)
    e = pl.program_id(1)
    stride = MOE_STRIDE
    unroll = SUBLANES
    READY, PEND_BASE, PEND_ROWS = 0, 1, 2

    @pl.when((b == 0) & (e == 0))
    def _():
        ybuf_ref[...] = jnp.zeros_like(ybuf_ref)

    @pl.when(e == 0)
    def _():
        state_ref[READY] = 0
        state_ref[PEND_BASE] = 0
        state_ref[PEND_ROWS] = 0

    start = st_ref[b * N_EXPERTS + e]
    count = cnt_ref[b * N_EXPERTS + e]

    def gather_row(base, r):
        tok = tok_ref[0, 0, base + r]
        src = pl.ds(pl.multiple_of(tok * PACK_ROWS, PACK_ROWS), PACK_ROWS)
        tile_ref[pl.ds(r, PACK_ROWS, stride=stride), :] = xs_ref[src, :]

    def scatter_row(base, r):
        dst = dst_ref[0, 0, base + r]
        o_ref[pl.ds(pl.multiple_of(dst * PACK_ROWS, PACK_ROWS), PACK_ROWS), :] = (
            ybuf_ref[pl.ds(r, PACK_ROWS, stride=stride), :])

    def copy_rows(row_fn, base, lo, hi):
        def group(i, c):
            for u in range(unroll):
                row_fn(base, i * unroll + u)
            return c
        lax.fori_loop(lo // unroll, hi // unroll, group, 0)

    def expert_mlp(rows, next_base, pend_base):
        def run():
            words = [tile_ref[pl.ds(j * stride, rows), :] for j in range(PACK_ROWS)]
            xb_ref[0:rows, :] = _unpack_pairs(words).astype(BF16)
            for r in range(rows):
                gather_row(next_base, r)
                scatter_row(pend_base, r)
            gu = jnp.dot(xb_ref[0:rows, :], wgu_ref[...], preferred_element_type=F32)
            gate, up = gu[:, :D_EXPERT], gu[:, D_EXPERT:]
            he = (_silu(gate) * up).astype(BF16)
            y = jnp.dot(he, wd_ref[...], preferred_element_type=F32)
            for j, word in enumerate(_pack_pairs(y)):
                ybuf_ref[pl.ds(j * stride, rows), :] = word
        return run

    def tile_body(t, carry):
        base = start + t * MOE_ROWS
        left = jnp.minimum(count - t * MOE_ROWS, MOE_ROWS)
        rows = (left + MOE_UNIT - 1) // MOE_UNIT * MOE_UNIT
        copy_rows(gather_row, base, state_ref[READY], rows)
        pend_rows = state_ref[PEND_ROWS]
        pend_base = jnp.where(pend_rows > 0, state_ref[PEND_BASE], base)
        copy_rows(scatter_row, pend_base, jnp.minimum(rows, pend_rows), pend_rows)
        next_base = base + left
        lax.switch(rows // MOE_UNIT - 1, [expert_mlp(MOE_UNIT * (k + 1), next_base, pend_base)
                                          for k in range(MOE_CLASSES)])
        state_ref[READY] = rows
        state_ref[PEND_BASE] = base
        state_ref[PEND_ROWS] = rows
        return carry

    lax.fori_loop(0, (count + MOE_ROWS - 1) // MOE_ROWS, tile_body, 0)

    @pl.when(e == N_EXPERTS - 1)
    def _():
        copy_rows(scatter_row, state_ref[PEND_BASE], 0, state_ref[PEND_ROWS])


def _moe_plan(route, tile_counts, nb, m):
    pick = lambda r: route[r].reshape(nb, m)
    e_idx = jnp.concatenate([pick(0), pick(1)], axis=1).astype(jnp.int32)
    order = jnp.argsort(e_idx, axis=1, stable=True).astype(jnp.int32)
    counts = tile_counts[:, 0].reshape(nb, -1, N_EXPERTS).sum(axis=1).astype(jnp.int32)
    starts = jnp.cumsum(counts, axis=1) - counts
    tail = ((0, 0), (0, 2 * MOE_ROWS))
    tok_list = jnp.pad(order % m, tail)
    dst_list = jnp.pad(order, tail, constant_values=2 * m)
    as_list = lambda a: a.reshape(nb, 1, a.shape[-1])
    return starts.reshape(-1), counts.reshape(-1), as_list(tok_list), as_list(dst_list)


def _moe(xt, route, tile_counts, w_gu, w_down, t, d):
    m = min(MOE_BLOCK, t)
    nb = t // m
    starts, counts, tok_list, dst_list = _moe_plan(route, tile_counts, nb, m)
    length = tok_list.shape[-1]
    slots = 2 * m + PACK
    grid_spec = pltpu.PrefetchScalarGridSpec(
        num_scalar_prefetch=2, grid=(nb, N_EXPERTS),
        in_specs=[pl.BlockSpec((1, 1, length), lambda b, e, *_: (b, 0, 0), memory_space=pltpu.SMEM),
                  pl.BlockSpec((1, 1, length), lambda b, e, *_: (b, 0, 0), memory_space=pltpu.SMEM),
                  pl.BlockSpec((m * PACK_ROWS, LANES), lambda b, e, *_: (b, 0), pipeline_mode=pl.Buffered(1)),
                  pl.BlockSpec((None, d, 2 * D_EXPERT), lambda b, e, *_: (e, 0, 0)),
                  pl.BlockSpec((None, D_EXPERT, d), lambda b, e, *_: (e, 0, 0))],
        out_specs=pl.BlockSpec((None, slots * PACK_ROWS, LANES), lambda b, e, *_: (b, 0, 0)),
        scratch_shapes=[pltpu.VMEM((PACK_ROWS * MOE_STRIDE, LANES), jnp.uint32),
                        pltpu.VMEM((MOE_ROWS, d), BF16),
                        pltpu.VMEM((PACK_ROWS * MOE_STRIDE, LANES), jnp.uint32),
                        pltpu.SMEM((3,), jnp.int32)])
    return pl.pallas_call(
        _moe_kernel, grid_spec=grid_spec,
        out_shape=jax.ShapeDtypeStruct((nb, slots * PACK_ROWS, LANES), jnp.uint32),
        compiler_params=_params(), name="moe_experts",
    )(starts, counts, tok_list, dst_list, xt, w_gu, w_down)


def _combine_kernel(h_ref, y0_ref, y1_ref, route_ref, nw_ref, o_ref, *, final_norm):
    tm = h_ref.shape[0]
    token_rows = lambda ref: _unpack_pairs([ref[pl.ds(j, tm, stride=PACK_ROWS), :] for j in range(PACK_ROWS)])
    gates = route_ref[...].T
    out = h_ref[...] + gates[:, 2:3] * token_rows(y0_ref) + gates[:, 3:4] * token_rows(y1_ref)
    if final_norm:
        out = _rms(out, nw_ref[...])
    o_ref[...] = out


def _combine(h, y, route, norm_w, final_norm, name):
    t, d = h.shape
    tm = min(ROW_TILE, t)
    m = min(MOE_BLOCK, t)
    per_block = m // tm

    def slot(s):
        return pl.BlockSpec((None, tm * PACK_ROWS, LANES),
                            lambda i: (i // per_block, s * per_block + i % per_block, 0))

    return pl.pallas_call(
        functools.partial(_combine_kernel, final_norm=final_norm), grid=(t // tm,),
        in_specs=[pl.BlockSpec((tm, d), lambda i: (i, 0)),
                  slot(0), slot(1),
                  pl.BlockSpec((SUBLANES, tm), lambda i: (0, i)),
                  _resident((1, d))],
        out_specs=pl.BlockSpec((tm, d), lambda i: (i, 0)),
        out_shape=jax.ShapeDtypeStruct((t, d), F32),
        compiler_params=_params(), name=name,
    )(h, y, y, route, norm_w.reshape(1, d))


def _combine_proj_kernel(h_ref, y0_ref, y1_ref, route_ref, kvn_ref, bn_ref, wkv_ref, wq_ref, wqm_ref,
                         t_ref, ho_ref, kv_ref, q_ref, qm_ref):
    tm = h_ref.shape[0]
    kvw = KV_HEADS * HEAD_DIM
    token_rows = lambda ref: _unpack_pairs([ref[pl.ds(j, tm, stride=PACK_ROWS), :] for j in range(PACK_ROWS)])
    gates = route_ref[...].T
    out = h_ref[...] + gates[:, 2:3] * token_rows(y0_ref) + gates[:, 3:4] * token_rows(y1_ref)
    ho_ref[...] = out
    unit = out * lax.rsqrt(jnp.mean(out * out, axis=-1, keepdims=True) + NORM_EPS)
    kv = jnp.dot((unit * kvn_ref[...]).astype(BF16), wkv_ref[...], preferred_element_type=F32)
    k = _rope(kv[:, :kvw], t_ref[0], t_ref[1], t_ref[2])
    kv_ref[...] = jnp.concatenate([k, kv[:, kvw:]], axis=-1).astype(kv_ref.dtype)
    xn = (unit * bn_ref[...]).astype(BF16)
    for c0 in range(0, wq_ref.shape[1], 512):
        q_ref[:, c0:c0 + 512] = jnp.dot(xn, wq_ref[:, c0:c0 + 512], preferred_element_type=F32).astype(q_ref.dtype)
    qm_ref[...] = jnp.dot(xn, wqm_ref[...], preferred_element_type=F32).astype(qm_ref.dtype)


def _combine_proj(h, y, route, kv_norm_w, b_norm_w, w_kv, w_q, w_qm, tables, seq):
    t, d = h.shape
    tm = min(ROW_TILE, seq)
    per_block = min(MOE_BLOCK, t) // tm
    per_seq = seq // tm
    slot = lambda s: pl.BlockSpec((None, tm * PACK_ROWS, LANES),
                                  lambda i: (i // per_block, s * per_block + i % per_block, 0))
    rows = lambda w: pl.BlockSpec((tm, w), lambda i: (i, 0))
    return pl.pallas_call(
        _combine_proj_kernel, grid=(t // tm,),
        in_specs=[rows(d), slot(0), slot(1),
                  pl.BlockSpec((SUBLANES, tm), lambda i: (0, i)),
                  _resident((1, d)), _resident((1, d)),
                  _resident(w_kv.shape), _resident(w_q.shape), _resident(w_qm.shape),
                  pl.BlockSpec((3, tm, LANES), lambda i: (0, i % per_seq, 0))],
        out_specs=[rows(d), rows(w_kv.shape[1]), rows(w_q.shape[1]), rows(w_qm.shape[1])],
        out_shape=[jax.ShapeDtypeStruct((t, d), F32),
                   jax.ShapeDtypeStruct((t, w_kv.shape[1]), BF16),
                   jax.ShapeDtypeStruct((t, w_q.shape[1]), BF16),
                   jax.ShapeDtypeStruct((t, w_qm.shape[1]), BF16)],
        compiler_params=_params(), name="moe_combine_swa_in_proj",
    )(h, y, y, route, kv_norm_w.reshape(1, d), b_norm_w.reshape(1, d), w_kv, w_q, w_qm, tables)


def _moe_experts(xt, route, tile_counts, w_gate, w_up, w_down, t, d):
    w_gu = jnp.concatenate([w_gate, w_up], axis=-1).astype(BF16)
    return _moe(xt, route, tile_counts, w_gu, w_down.astype(BF16), t, d)


def kernel(x, mem, a_norm_w, a_in_w, a_conv_w, a_conv_b, a_dt_bias, a_a_log, a_d_skip, a_gnorm_w, a_out_w, kv_norm_w, w_kv, b_norm_w, b_in_w, b_sinks, b_out_w, mem_norm_w, mem_w_kv, ffn_norm_w, router_group_w, router_group_b, router_expert_w, router_expert_b, w_gate, w_up, w_down, final_norm_w):
    bsz, seq, d = x.shape
    mem_len = mem.shape[1]
    t = bsz * seq
    h = x.reshape(t, d)
    mem2 = mem.reshape(bsz * mem_len, d)
    conv_ch = a_conv_w.shape[-1]
    n_a = a_in_w.shape[0]
    n_layers = n_a + b_in_w.shape[0]
    qw = Q_HEADS * HEAD_DIM
    tables = _rope_tables(seq)
    kv_sh = q = q_mem = None
    for layer in range(n_layers):
        (mem_kv,) = _norm_matmul(mem2, mem_norm_w[layer], mem_w_kv[layer].astype(BF16), [2 * MEM_WIDTH], [BF16],
                                 name=f"mem_kv_{layer}")
        if layer < n_a:
            i = layer
            w_in = a_in_w[i]
            o_dt, o_q = SSD_INNER + conv_ch, SSD_INNER + conv_ch + SSD_HEADS
            w_all = jnp.concatenate(
                [w_in[:, :o_dt], w_in[:, o_q:], w_in[:, o_dt:o_q], jnp.zeros((d, LANES - SSD_HEADS), F32)],
                axis=1).astype(BF16)
            z, xbc, q_mem, dt = _norm_matmul(h, a_norm_w[i], w_all, [SSD_INNER, conv_ch, MEM_WIDTH, LANES],
                                             [BF16, BF16, BF16, F32], name=f"ssd_in_proj_{i}")
            y = _ssd(xbc, z, dt, a_conv_w[i], a_conv_b[i], a_dt_bias[i], a_a_log[i], a_d_skip[i],
                     a_gnorm_w[i], bsz, seq)
            m_att = _mem_attn(q_mem, mem_kv, bsz, seq, mem_len, name=f"mem_attn_{layer}")
            w_out = a_out_w[i].astype(BF16)
            h, xt, route, tile_counts = _mix_out_route(
                y, m_att, w_out[:SSD_INNER], w_out[SSD_INNER:], h, ffn_norm_w[layer],
                router_group_w[layer], router_group_b[layer], router_expert_w[layer], router_expert_b[layer],
                name=f"ssd_out_proj_route_{i}")
        else:
            j = layer - n_a
            if q is None:
                if j == 0:
                    kv_sh = _shared_kv(h, kv_norm_w, w_kv.astype(BF16), tables, seq)
                w_in = b_in_w[j].astype(BF16)
                q, q_mem = _norm_matmul(h, b_norm_w[j], w_in, [qw, MEM_WIDTH], [BF16, BF16],
                                        name=f"swa_in_proj_{j}")
            att = _swa(q, kv_sh, b_sinks[j], tables, bsz, seq)
            m_att = _mem_attn(q_mem, mem_kv, bsz, seq, mem_len, name=f"mem_attn_{layer}")
            w_out = b_out_w[j].astype(BF16)
            h, xt, route, tile_counts = _mix_out_route(
                att, m_att, w_out[:qw], w_out[qw:], h, ffn_norm_w[layer],
                router_group_w[layer], router_group_b[layer], router_expert_w[layer], router_expert_b[layer],
                name=f"swa_out_proj_route_{j}")
            q = None
        y_moe = _moe_experts(xt, route, tile_counts, w_gate[layer], w_up[layer], w_down[layer], t, d)
        if layer + 1 == n_a and layer + 1 < n_layers:
            w_in = b_in_w[0].astype(BF16)
            h, kv_sh, q, q_mem = _combine_proj(h, y_moe, route, kv_norm_w, b_norm_w[0], w_kv.astype(BF16),
                                               w_in[:, :qw], w_in[:, qw:], tables, seq)
        else:
            last = layer == n_layers - 1
            h = _combine(h, y_moe, route, final_norm_w, last, name=f"moe_combine_{layer}")
    return h.reshape(bsz, seq, d)
```

```python
import functools

import jax
import jax.numpy as jnp
from jax import lax
from jax.experimental import pallas as pl
from jax.experimental.pallas import tpu as pltpu

F32 = jnp.float32
BF16 = jnp.bfloat16
NORM_EPS = 1e-6
LOG2_E = 1.4426950408889634

LANES = 128
SUBLANES = 8
VMEM_LIMIT = 56 * 1024 * 1024

MEM_HEADS = 4
MEM_HEAD_DIM = 128
MEM_WIDTH = MEM_HEADS * MEM_HEAD_DIM
SSD_HEAD_DIM = 64
SSD_HEADS = 24
SSD_INNER = SSD_HEADS * SSD_HEAD_DIM
SSD_GROUPS = 4
SSD_STATE = 128
SSD_CHUNK = 128
CONV_WIDTH = 4
HEAD_DIM = 64
Q_HEADS = 16
KV_HEADS = 4
ATTN_BLOCK = 128
ROT_DIM = 16
ROPE_THETA = 500000.0
N_GROUPS = 4
EXPERTS_PER_GROUP = 8
N_EXPERTS = 32
D_EXPERT = 512

ROW_TILE = 512
MOE_BLOCK = 4096
MOE_UNIT = 64
MOE_CLASSES = 8
MOE_ROWS = MOE_UNIT * MOE_CLASSES
MOE_STRIDE = MOE_ROWS + SUBLANES
PACK = 2
PACK_ROWS = SUBLANES // PACK
HIGH_HALF = 0xFFFF0000


def _params():
    return pltpu.CompilerParams(vmem_limit_bytes=VMEM_LIMIT)


def _resident(shape):
    nd = len(shape)
    return pl.BlockSpec(shape, lambda *_: (0,) * nd, pipeline_mode=pl.Buffered(1))


def _rms(x, w):
    ms = jnp.mean(x * x, axis=-1, keepdims=True)
    return x * lax.rsqrt(ms + NORM_EPS) * w


def _silu(x):
    u = 0.5 * x
    return u * jnp.tanh(u) + u


def _pack_pairs(x):
    half = x.shape[1] // PACK
    bits = lax.bitcast_convert_type(x.astype(BF16).astype(F32), jnp.uint32)
    return [(bits[:, j * LANES:(j + 1) * LANES] >> 16)
            | (bits[:, half + j * LANES:half + (j + 1) * LANES] & jnp.uint32(HIGH_HALF))
            for j in range(half // LANES)]


def _unpack_pairs(words):
    lo = [lax.bitcast_convert_type(w << 16, F32) for w in words]
    hi = [lax.bitcast_convert_type(w & jnp.uint32(HIGH_HALF), F32) for w in words]
    return jnp.concatenate(lo + hi, axis=-1)


def _norm_matmul_kernel(x_ref, nw_ref, w_ref, *o_refs):
    xn = _rms(x_ref[...], nw_ref[...]).astype(BF16)
    lo = 0
    for o_ref in o_refs:
        n = o_ref.shape[1]
        for c0 in range(0, n, 512):
            cw = min(512, n - c0)
            acc = jnp.dot(xn, w_ref[:, lo + c0:lo + c0 + cw], preferred_element_type=F32)
            o_ref[:, c0:c0 + cw] = acc.astype(o_ref.dtype)
        lo += n


def _norm_matmul(x, norm_w, w, widths, out_dtypes, name):
    t, d = x.shape
    tm = min(ROW_TILE, t)
    assert sum(widths) == w.shape[1]
    return pl.pallas_call(
        _norm_matmul_kernel, grid=(t // tm,),
        in_specs=[pl.BlockSpec((tm, d), lambda i: (i, 0)), _resident((1, d)), _resident(w.shape)],
        out_specs=[pl.BlockSpec((tm, n), lambda i: (i, 0)) for n in widths],
        out_shape=[jax.ShapeDtypeStruct((t, n), dt) for n, dt in zip(widths, out_dtypes)],
        compiler_params=_params(), name=name,
    )(x, norm_w.reshape(1, d), w)


def _mix_out_route_kernel(a1_ref, a2_ref, w1_ref, w2_ref, r_ref, nw_ref, wr_ref, br_ref,
                          o_ref, xt_ref, route_ref, cnt_ref):
    acc = jnp.dot(a1_ref[...], w1_ref[...], preferred_element_type=F32)
    acc = acc + jnp.dot(a2_ref[...], w2_ref[...], preferred_element_type=F32)
    h = r_ref[...] + acc
    o_ref[...] = h
    _route_tokens(h, nw_ref, wr_ref, br_ref, xt_ref, route_ref, cnt_ref)


def _mix_out_route(a1, a2, w1, w2, res, norm_w, rg_w, rg_b, re_w, re_b, name):
    t, d = res.shape
    tm = min(ROW_TILE, t)
    wr, br = _router_weights(rg_w, rg_b, re_w, re_b)
    rows = lambda w: pl.BlockSpec((tm, w), lambda i: (i, 0))
    return pl.pallas_call(
        _mix_out_route_kernel, grid=(t // tm,),
        in_specs=[rows(a1.shape[1]), rows(a2.shape[1]), _resident(w1.shape), _resident(w2.shape), rows(d),
                  _resident((1, d)), _resident(wr.shape), _resident(br.shape)],
        out_specs=[rows(d),
                   pl.BlockSpec((tm * PACK_ROWS, LANES), lambda i: (i, 0)),
                   pl.BlockSpec((SUBLANES, tm), lambda i: (0, i)),
                   pl.BlockSpec((N_EXPERTS, LANES), lambda i: (i, 0))],
        out_shape=[jax.ShapeDtypeStruct((t, d), F32),
                   jax.ShapeDtypeStruct((t * PACK_ROWS, LANES), jnp.uint32),
                   jax.ShapeDtypeStruct((SUBLANES, t), F32),
                   jax.ShapeDtypeStruct((t // tm * N_EXPERTS, LANES), F32)],
        compiler_params=_params(), name=name,
    )(a1, a2, w1, w2, res, norm_w.reshape(1, d), wr, br)


def _mem_attn_kernel(q_ref, kv_ref, o_ref):
    scale = MEM_HEAD_DIM ** -0.5
    outs = []
    for h in range(MEM_HEADS):
        lo = h * MEM_HEAD_DIM
        q = q_ref[:, lo:lo + MEM_HEAD_DIM]
        k = kv_ref[:, lo:lo + MEM_HEAD_DIM]
        v = kv_ref[:, MEM_WIDTH + lo:MEM_WIDTH + lo + MEM_HEAD_DIM]
        s = lax.dot_general(q, k, (((1,), (1,)), ((), ())), preferred_element_type=F32) * scale
        p = jnp.exp(s - jnp.max(s, axis=-1, keepdims=True))
        p = p / jnp.sum(p, axis=-1, keepdims=True)
        outs.append(jnp.dot(p.astype(BF16), v, preferred_element_type=F32))
    o_ref[...] = jnp.concatenate(outs, axis=-1).astype(o_ref.dtype)


def _mem_attn(q, kv, bsz, seq, mem_len, name):
    tq = min(ROW_TILE, seq)
    nq = seq // tq
    return pl.pallas_call(
        _mem_attn_kernel, grid=(bsz, nq),
        in_specs=[pl.BlockSpec((tq, MEM_WIDTH), lambda b, i: (b * nq + i, 0)),
                  pl.BlockSpec((mem_len, 2 * MEM_WIDTH), lambda b, i: (b, 0))],
        out_specs=pl.BlockSpec((tq, MEM_WIDTH), lambda b, i: (b * nq + i, 0)),
        out_shape=jax.ShapeDtypeStruct((bsz * seq, MEM_WIDTH), BF16),
        compiler_params=_params(), name=name,
    )(q, kv)


def _ssd_kernel(xbc_ref, z_ref, dt_ref, cw_ref, cb_ref, dtb_ref, alog_ref, dsk_ref, gnw_ref,
                expand_ref, o_ref, ext_ref, st_ref):
    c = pl.program_id(1)
    L = SSD_CHUNK
    gw = SSD_INNER // SSD_GROUPS
    hpg = SSD_HEADS // SSD_GROUPS

    @pl.when(c == 0)
    def _():
        ext_ref[0:SUBLANES, :] = jnp.zeros((SUBLANES, ext_ref.shape[1]), F32)
        st_ref[...] = jnp.zeros_like(st_ref)

    @pl.when(c > 0)
    def _():
        ext_ref[0:SUBLANES, :] = ext_ref[L:L + SUBLANES, :]

    ext_ref[SUBLANES:SUBLANES + L, :] = xbc_ref[...].astype(F32)
    conv = cb_ref[...]
    for k in range(CONV_WIDTH):
        conv = conv + cw_ref[k:k + 1, :] * ext_ref[pl.ds(SUBLANES - (CONV_WIDTH - 1) + k, L), :]
    xbc = _silu(conv)
    xs = xbc[:, :SSD_INNER]
    bm = xbc[:, SSD_INNER:SSD_INNER + SSD_GROUPS * SSD_STATE]
    cm = xbc[:, SSD_INNER + SSD_GROUPS * SSD_STATE:].astype(BF16)

    pre = dt_ref[...] + dtb_ref[...]
    dt = jnp.maximum(pre, 0.0) + jnp.log(1.0 + jnp.exp(-jnp.abs(pre)))
    a = -jnp.exp(alog_ref[...])
    adt = dt * a
    row = lax.broadcasted_iota(jnp.int32, (L, L), 0)
    col = lax.broadcasted_iota(jnp.int32, (L, L), 1)
    causal = row >= col
    acum = jnp.dot(causal.astype(F32), adt, preferred_element_type=F32,
                   precision=lax.Precision.HIGHEST)
    acum_t = acum.T
    ea = jnp.exp(acum)
    te = jnp.exp(acum[L - 1:L, :] - acum)
    widen = lambda cols: jnp.dot(cols.astype(BF16), expand_ref[...], preferred_element_type=F32)
    dtx, eax, tex = widen(dt), widen(ea), widen(te)
    xdt = xs * dtx
    xdt_b = xdt.astype(BF16)
    xw = (xdt * tex).astype(BF16)

    first_half = col < SSD_HEAD_DIM
    ys = []
    for g in range(SSD_GROUPS):
        cg = cm[:, g * SSD_STATE:(g + 1) * SSD_STATE]
        bg_t = bm[:, g * SSD_STATE:(g + 1) * SSD_STATE].T.astype(BF16)
        cb = jnp.dot(cg, bg_t, preferred_element_type=F32)
        state = st_ref[g]
        y_off = jnp.dot(cg, state.astype(BF16), preferred_element_type=F32) * eax[:, g * gw:(g + 1) * gw]
        parts = []
        for r in range(0, hpg, 2):
            both = []
            for h in (g * hpg + r, g * hpg + r + 1):
                seg = acum[:, h:h + 1] - acum_t[h:h + 1, :]
                m = (cb * jnp.exp(jnp.where(causal, seg, -jnp.inf))).astype(BF16)
                lo = (g * hpg + r) * SSD_HEAD_DIM
                both.append(jnp.dot(m, xdt_b[:, lo:lo + LANES], preferred_element_type=F32))
            parts.append(jnp.where(first_half, both[0], both[1]))
        ys.append(jnp.concatenate(parts, axis=-1) + y_off)
        upd = jnp.dot(bg_t, xw[:, g * gw:(g + 1) * gw], preferred_element_type=F32)
        st_ref[g] = state * eax[L - 1:L, g * gw:(g + 1) * gw] + upd
    y = jnp.concatenate(ys, axis=-1) + dsk_ref[...] * xs
    zf = z_ref[...].astype(F32)
    y = y * _silu(zf)
    normed = []
    for g in range(SSD_GROUPS):
        yg = y[:, g * gw:(g + 1) * gw]
        normed.append(yg * lax.rsqrt(jnp.mean(yg * yg, axis=-1, keepdims=True) + NORM_EPS))
    o_ref[...] = (jnp.concatenate(normed, axis=-1) * gnw_ref[...]).astype(o_ref.dtype)


def _ssd(xbc, z, dt, conv_w, conv_b, dt_bias, a_log, d_skip, gnorm_w, bsz, seq):
    nc = seq // SSD_CHUNK
    conv_ch = xbc.shape[1]
    pad = LANES - SSD_HEADS
    dtb = jnp.pad(dt_bias, (0, pad)).reshape(1, LANES)
    alog = jnp.pad(a_log, (0, pad)).reshape(1, LANES)
    dsk = jnp.repeat(d_skip, SSD_HEAD_DIM).reshape(1, SSD_INNER)
    expand = (jnp.arange(LANES)[:, None] == (jnp.arange(SSD_INNER) // SSD_HEAD_DIM)[None, :]).astype(BF16)
    blk = lambda w: pl.BlockSpec((SSD_CHUNK, w), lambda b, c: (b * nc + c, 0))
    return pl.pallas_call(
        _ssd_kernel, grid=(bsz, nc),
        in_specs=[blk(conv_ch), blk(SSD_INNER), blk(LANES),
                  _resident((CONV_WIDTH, conv_ch)), _resident((1, conv_ch)),
                  _resident((1, LANES)), _resident((1, LANES)),
                  _resident((1, SSD_INNER)), _resident((1, SSD_INNER)),
                  _resident((LANES, SSD_INNER))],
        out_specs=blk(SSD_INNER),
        out_shape=jax.ShapeDtypeStruct((bsz * seq, SSD_INNER), BF16),
        scratch_shapes=[pltpu.VMEM((SSD_CHUNK + 2 * SUBLANES, conv_ch), F32),
                        pltpu.VMEM((SSD_GROUPS, SSD_STATE, SSD_INNER // SSD_GROUPS), F32)],
        compiler_params=_params(), name="ssd_mixer",
    )(xbc, z, dt, conv_w, conv_b.reshape(1, conv_ch), dtb, alog, dsk,
      gnorm_w.reshape(1, SSD_INNER), expand)


def _rope(x, cos, sin_up, sin_dn):
    w = x.shape[1]
    reps = w // LANES
    tile = lambda t: jnp.concatenate([t] * reps, axis=-1) if reps > 1 else t
    half = ROT_DIM // 2
    return (x * tile(cos) + pltpu.roll(x, half, axis=1) * tile(sin_up)
            + pltpu.roll(x, w - half, axis=1) * tile(sin_dn))


def _swa_kernel(sink_ref, q_ref, kvp_ref, kvc_ref, tc_ref, bias_ref, o_ref):
    kvw = KV_HEADS * HEAD_DIM
    g = Q_HEADS // KV_HEADS
    scale = HEAD_DIM ** -0.5 * LOG2_E
    q = _rope(q_ref[...].astype(F32), tc_ref[0], tc_ref[1], tc_ref[2]) * scale
    q_t = q.T.astype(BF16)
    k = jnp.concatenate([kvp_ref[:, :kvw], kvc_ref[:, :kvw]], axis=0)
    v = jnp.concatenate([kvp_ref[:, kvw:], kvc_ref[:, kvw:]], axis=0).astype(F32)
    v_t = v.T.astype(BF16)
    bias = bias_ref[0]
    heads = lambda kh: range(kh * g, (kh + 1) * g)
    scores = [jnp.concatenate(
        [jnp.dot(k[:, kh * HEAD_DIM:(kh + 1) * HEAD_DIM], q_t[h * HEAD_DIM:(h + 1) * HEAD_DIM, :],
                 preferred_element_type=F32) for h in heads(kh)], axis=-1) + bias
        for kh in range(KV_HEADS)]
    sinks = [jnp.concatenate([jnp.full((1, ATTN_BLOCK), sink_ref[h] * LOG2_E, F32) for h in heads(kh)], axis=-1)
             for kh in range(KV_HEADS)]
    maxes = [jnp.maximum(jnp.max(sc, axis=0, keepdims=True), sk) for sc, sk in zip(scores, sinks)]
    probs = [jnp.exp2(sc - mx) for sc, mx in zip(scores, maxes)]
    denoms = [jnp.sum(p, axis=0, keepdims=True) + jnp.exp2(sk - mx) for p, sk, mx in zip(probs, sinks, maxes)]
    outs = []
    for kh in range(KV_HEADS):
        o_t = jnp.dot(v_t[kh * HEAD_DIM:(kh + 1) * HEAD_DIM, :], probs[kh].astype(BF16),
                      preferred_element_type=F32) / denoms[kh]
        outs += [o_t[:, j * ATTN_BLOCK:(j + 1) * ATTN_BLOCK] for j in range(g)]
    o_ref[...] = jnp.concatenate(outs, axis=0).T.astype(o_ref.dtype)


def _swa_bias(g):
    blk = ATTN_BLOCK
    s_idx = jnp.arange(2 * blk)[:, None]
    t_idx = jnp.arange(blk)[None, :]
    band = (s_idx > t_idx) & (s_idx <= t_idx + blk)
    first = band & (s_idx >= blk)
    both = jnp.stack([first, band], axis=0)
    return jnp.tile(jnp.where(both, 0.0, -jnp.inf).astype(F32), (1, 1, g))


def _rope_tables(seq):
    half = ROT_DIM // 2
    inv_freq = ROPE_THETA ** (-2.0 * jnp.arange(half, dtype=F32) / ROT_DIM)
    ang = jnp.arange(seq, dtype=jnp.int32).astype(F32)[:, None] * inv_freq[None, :]
    cos, sin = jnp.cos(ang), jnp.sin(ang)
    ones = jnp.ones((seq, HEAD_DIM - ROT_DIM), F32)
    zeros = jnp.zeros((seq, HEAD_DIM - ROT_DIM), F32)
    zh = jnp.zeros((seq, half), F32)
    cos_h = jnp.concatenate([cos, cos, ones], axis=-1)
    up_h = jnp.concatenate([zh, sin, zeros], axis=-1)
    dn_h = jnp.concatenate([-sin, zh, zeros], axis=-1)
    two = lambda t: jnp.concatenate([t, t], axis=-1)
    return jnp.stack([two(cos_h), two(up_h), two(dn_h)], axis=0)


def _shared_kv_kernel(x_ref, nw_ref, w_ref, t_ref, o_ref):
    kvw = KV_HEADS * HEAD_DIM
    xn = _rms(x_ref[...], nw_ref[...]).astype(BF16)
    kv = jnp.dot(xn, w_ref[...], preferred_element_type=F32)
    k = _rope(kv[:, :kvw], t_ref[0], t_ref[1], t_ref[2])
    o_ref[...] = jnp.concatenate([k, kv[:, kvw:]], axis=-1).astype(o_ref.dtype)


def _shared_kv(h, norm_w, w, tables, seq):
    t, d = h.shape
    tm = min(ROW_TILE, seq)
    per_seq = seq // tm
    return pl.pallas_call(
        _shared_kv_kernel, grid=(t // tm,),
        in_specs=[pl.BlockSpec((tm, d), lambda i: (i, 0)), _resident((1, d)), _resident(w.shape),
                  pl.BlockSpec((3, tm, LANES), lambda i: (0, i % per_seq, 0))],
        out_specs=pl.BlockSpec((tm, w.shape[1]), lambda i: (i, 0)),
        out_shape=jax.ShapeDtypeStruct((t, w.shape[1]), BF16),
        compiler_params=_params(), name="shared_kv",
    )(h, norm_w.reshape(1, d), w, tables)


def _swa(q, kv, sinks, tables, bsz, seq):
    nb = seq // ATTN_BLOCK
    qw = Q_HEADS * HEAD_DIM
    kvw2 = 2 * KV_HEADS * HEAD_DIM
    prev = lambda i: jnp.maximum(i - 1, 0)
    g = Q_HEADS // KV_HEADS
    return pl.pallas_call(
        _swa_kernel, grid=(bsz, nb),
        in_specs=[pl.BlockSpec(memory_space=pltpu.SMEM),
                  pl.BlockSpec((ATTN_BLOCK, qw), lambda b, i: (b * nb + i, 0)),
                  pl.BlockSpec((ATTN_BLOCK, kvw2), lambda b, i: (b * nb + prev(i), 0)),
                  pl.BlockSpec((ATTN_BLOCK, kvw2), lambda b, i: (b * nb + i, 0)),
                  pl.BlockSpec((3, ATTN_BLOCK, LANES), lambda b, i: (0, i, 0)),
                  pl.BlockSpec((1, 2 * ATTN_BLOCK, g * ATTN_BLOCK), lambda b, i: (jnp.minimum(i, 1), 0, 0))],
        out_specs=pl.BlockSpec((ATTN_BLOCK, qw), lambda b, i: (b * nb + i, 0)),
        out_shape=jax.ShapeDtypeStruct((bsz * seq, qw), BF16),
        compiler_params=_params(), name="swa_attention",
    )(sinks, q, kv, kv, tables, _swa_bias(g))


def _route_tokens(h, nw_ref, wr_ref, br_ref, xt_ref, route_ref, cnt_ref):
    tm = h.shape[0]
    xn = _rms(h, nw_ref[...])
    x_hi = xn.astype(BF16)
    for j, word in enumerate(_pack_pairs(xn)):
        xt_ref[pl.ds(j, tm, stride=PACK_ROWS), :] = word
    x_lo = (xn - x_hi.astype(F32)).astype(BF16)
    both = jnp.dot(x_hi, wr_ref[...], preferred_element_type=F32)
    logits = (both[:, :LANES] + both[:, LANES:]
              + jnp.dot(x_lo, wr_ref[:, :LANES], preferred_element_type=F32) + br_ref[...])
    lt = logits.T
    neg = -jnp.inf
    grow = lax.broadcasted_iota(jnp.int32, (SUBLANES, tm), 0)
    gl = jnp.where(grow < N_GROUPS, lt[:SUBLANES], neg)
    gmax = jnp.max(gl, axis=0, keepdims=True)
    g_w = 1.0 / jnp.sum(jnp.exp(gl - gmax), axis=0, keepdims=True)
    g_idx = jnp.min(jnp.where(gl == gmax, grow, SUBLANES), axis=0, keepdims=True)
    erow = lax.broadcasted_iota(jnp.int32, (N_EXPERTS, tm), 0)
    el = jnp.where(erow // EXPERTS_PER_GROUP == g_idx, lt[SUBLANES:SUBLANES + N_EXPERTS], neg)
    m1 = jnp.max(el, axis=0, keepdims=True)
    i1 = jnp.min(jnp.where(el == m1, erow, N_EXPERTS), axis=0, keepdims=True)
    el2 = jnp.where(erow == i1, neg, el)
    m2 = jnp.max(el2, axis=0, keepdims=True)
    i2 = jnp.min(jnp.where(el2 == m2, erow, N_EXPERTS), axis=0, keepdims=True)
    d = jnp.exp(m2 - m1)
    w1 = g_w / (1.0 + d)
    w2 = g_w * d / (1.0 + d)
    zero = jnp.zeros((SUBLANES - 4, tm), F32)
    route_ref[...] = jnp.concatenate([i1.astype(F32), i2.astype(F32), w1, w2, zero], axis=0)
    chosen = ((erow == i1) | (erow == i2)).astype(F32)
    cnt_ref[...] = jnp.broadcast_to(jnp.sum(chosen, axis=1, keepdims=True), cnt_ref.shape)


def _router_weights(rg_w, rg_b, re_w, re_b):
    d = rg_w.shape[0]
    gap = jnp.zeros((d, SUBLANES - N_GROUPS), F32)
    tail = jnp.zeros((d, LANES - SUBLANES - N_EXPERTS), F32)
    wr = jnp.concatenate([rg_w, gap, re_w, tail], axis=1)
    w_hi = wr.astype(BF16)
    w_lo = (wr - w_hi.astype(F32)).astype(BF16)
    br = jnp.concatenate([rg_b, gap[0], re_b, tail[0]]).reshape(1, LANES)
    return jnp.concatenate([w_hi, w_lo], axis=1), br


def _moe_kernel(st_ref, cnt_ref, tok_ref, dst_ref, xs_ref, wgu_ref, wd_ref, o_ref,
                tile_ref, xb_ref, ybuf_ref, state_ref):
    b = pl.program_id(0)
    e = pl.program_id(1)
    stride = MOE_STRIDE
    unroll = SUBLANES
    READY, PEND_BASE, PEND_ROWS = 0, 1, 2

    @pl.when((b == 0) & (e == 0))
    def _():
        ybuf_ref[...] = jnp.zeros_like(ybuf_ref)

    @pl.when(e == 0)
    def _():
        state_ref[READY] = 0
        state_ref[PEND_BASE] = 0
        state_ref[PEND_ROWS] = 0

    start = st_ref[b * N_EXPERTS + e]
    count = cnt_ref[b * N_EXPERTS + e]

    def gather_row(base, r):
        tok = tok_ref[0, 0, base + r]
        src = pl.ds(pl.multiple_of(tok * PACK_ROWS, PACK_ROWS), PACK_ROWS)
        tile_ref[pl.ds(r, PACK_ROWS, stride=stride), :] = xs_ref[src, :]

    def scatter_row(base, r):
        dst = dst_ref[0, 0, base + r]
        o_ref[pl.ds(pl.multiple_of(dst * PACK_ROWS, PACK_ROWS), PACK_ROWS), :] = (
            ybuf_ref[pl.ds(r, PACK_ROWS, stride=stride), :])

    def copy_rows(row_fn, base, lo, hi):
        def group(i, c):
            for u in range(unroll):
                row_fn(base, i * unroll + u)
            return c
        lax.fori_loop(lo // unroll, hi // unroll, group, 0)

    def expert_mlp(rows, next_base, pend_base):
        def run():
            words = [tile_ref[pl.ds(j * stride, rows), :] for j in range(PACK_ROWS)]
            xb_ref[0:rows, :] = _unpack_pairs(words).astype(BF16)
            for r in range(rows):
                gather_row(next_base, r)
                scatter_row(pend_base, r)
            gu = jnp.dot(xb_ref[0:rows, :], wgu_ref[...], preferred_element_type=F32)
            gate, up = gu[:, :D_EXPERT], gu[:, D_EXPERT:]
            he = (_silu(gate) * up).astype(BF16)
            y = jnp.dot(he, wd_ref[...], preferred_element_type=F32)
            for j, word in enumerate(_pack_pairs(y)):
                ybuf_ref[pl.ds(j * stride, rows), :] = word
        return run

    def tile_body(t, carry):
        base = start + t * MOE_ROWS
        left = jnp.minimum(count - t * MOE_ROWS, MOE_ROWS)
        rows = (left + MOE_UNIT - 1) // MOE_UNIT * MOE_UNIT
        copy_rows(gather_row, base, state_ref[READY], rows)
        pend_rows = state_ref[PEND_ROWS]
        pend_base = jnp.where(pend_rows > 0, state_ref[PEND_BASE], base)
        copy_rows(scatter_row, pend_base, jnp.minimum(rows, pend_rows), pend_rows)
        next_base = base + left
        lax.switch(rows // MOE_UNIT - 1, [expert_mlp(MOE_UNIT * (k + 1), next_base, pend_base)
                                          for k in range(MOE_CLASSES)])
        state_ref[READY] = rows
        state_ref[PEND_BASE] = base
        state_ref[PEND_ROWS] = rows
        return carry

    lax.fori_loop(0, (count + MOE_ROWS - 1) // MOE_ROWS, tile_body, 0)

    @pl.when(e == N_EXPERTS - 1)
    def _():
        copy_rows(scatter_row, state_ref[PEND_BASE], 0, state_ref[PEND_ROWS])


def _moe_plan(route, tile_counts, nb, m):
    pick = lambda r: route[r].reshape(nb, m)
    e_idx = jnp.concatenate([pick(0), pick(1)], axis=1).astype(jnp.int32)
    order = jnp.argsort(e_idx, axis=1, stable=True).astype(jnp.int32)
    counts = tile_counts[:, 0].reshape(nb, -1, N_EXPERTS).sum(axis=1).astype(jnp.int32)
    starts = jnp.cumsum(counts, axis=1) - counts
    tail = ((0, 0), (0, 2 * MOE_ROWS))
    tok_list = jnp.pad(order % m, tail)
    dst_list = jnp.pad(order, tail, constant_values=2 * m)
    as_list = lambda a: a.reshape(nb, 1, a.shape[-1])
    return starts.reshape(-1), counts.reshape(-1), as_list(tok_list), as_list(dst_list)


def _moe(xt, route, tile_counts, w_gu, w_down, t, d):
    m = min(MOE_BLOCK, t)
    nb = t // m
    starts, counts, tok_list, dst_list = _moe_plan(route, tile_counts, nb, m)
    length = tok_list.shape[-1]
    slots = 2 * m + PACK
    grid_spec = pltpu.PrefetchScalarGridSpec(
        num_scalar_prefetch=2, grid=(nb, N_EXPERTS),
        in_specs=[pl.BlockSpec((1, 1, length), lambda b, e, *_: (b, 0, 0), memory_space=pltpu.SMEM),
                  pl.BlockSpec((1, 1, length), lambda b, e, *_: (b, 0, 0), memory_space=pltpu.SMEM),
                  pl.BlockSpec((m * PACK_ROWS, LANES), lambda b, e, *_: (b, 0), pipeline_mode=pl.Buffered(1)),
                  pl.BlockSpec((None, d, 2 * D_EXPERT), lambda b, e, *_: (e, 0, 0)),
                  pl.BlockSpec((None, D_EXPERT, d), lambda b, e, *_: (e, 0, 0))],
        out_specs=pl.BlockSpec((None, slots * PACK_ROWS, LANES), lambda b, e, *_: (b, 0, 0)),
        scratch_shapes=[pltpu.VMEM((PACK_ROWS * MOE_STRIDE, LANES), jnp.uint32),
                        pltpu.VMEM((MOE_ROWS, d), BF16),
                        pltpu.VMEM((PACK_ROWS * MOE_STRIDE, LANES), jnp.uint32),
                        pltpu.SMEM((3,), jnp.int32)])
    return pl.pallas_call(
        _moe_kernel, grid_spec=grid_spec,
        out_shape=jax.ShapeDtypeStruct((nb, slots * PACK_ROWS, LANES), jnp.uint32),
        compiler_params=_params(), name="moe_experts",
    )(starts, counts, tok_list, dst_list, xt, w_gu, w_down)


def _combine_kernel(h_ref, y0_ref, y1_ref, route_ref, nw_ref, o_ref, *, final_norm):
    tm = h_ref.shape[0]
    token_rows = lambda ref: _unpack_pairs([ref[pl.ds(j, tm, stride=PACK_ROWS), :] for j in range(PACK_ROWS)])
    gates = route_ref[...].T
    out = h_ref[...] + gates[:, 2:3] * token_rows(y0_ref) + gates[:, 3:4] * token_rows(y1_ref)
    if final_norm:
        out = _rms(out, nw_ref[...])
    o_ref[...] = out


def _combine(h, y, route, norm_w, final_norm, name):
    t, d = h.shape
    tm = min(ROW_TILE, t)
    m = min(MOE_BLOCK, t)
    per_block = m // tm

    def slot(s):
        return pl.BlockSpec((None, tm * PACK_ROWS, LANES),
                            lambda i: (i // per_block, s * per_block + i % per_block, 0))

    return pl.pallas_call(
        functools.partial(_combine_kernel, final_norm=final_norm), grid=(t // tm,),
        in_specs=[pl.BlockSpec((tm, d), lambda i: (i, 0)),
                  slot(0), slot(1),
                  pl.BlockSpec((SUBLANES, tm), lambda i: (0, i)),
                  _resident((1, d))],
        out_specs=pl.BlockSpec((tm, d), lambda i: (i, 0)),
        out_shape=jax.ShapeDtypeStruct((t, d), F32),
        compiler_params=_params(), name=name,
    )(h, y, y, route, norm_w.reshape(1, d))


def _combine_proj_kernel(h_ref, y0_ref, y1_ref, route_ref, kvn_ref, bn_ref, wkv_ref, wq_ref, wqm_ref,
                         t_ref, ho_ref, kv_ref, q_ref, qm_ref):
    tm = h_ref.shape[0]
    kvw = KV_HEADS * HEAD_DIM
    token_rows = lambda ref: _unpack_pairs([ref[pl.ds(j, tm, stride=PACK_ROWS), :] for j in range(PACK_ROWS)])
    gates = route_ref[...].T
    out = h_ref[...] + gates[:, 2:3] * token_rows(y0_ref) + gates[:, 3:4] * token_rows(y1_ref)
    ho_ref[...] = out
    unit = out * lax.rsqrt(jnp.mean(out * out, axis=-1, keepdims=True) + NORM_EPS)
    kv = jnp.dot((unit * kvn_ref[...]).astype(BF16), wkv_ref[...], preferred_element_type=F32)
    k = _rope(kv[:, :kvw], t_ref[0], t_ref[1], t_ref[2])
    kv_ref[...] = jnp.concatenate([k, kv[:, kvw:]], axis=-1).astype(kv_ref.dtype)
    xn = (unit * bn_ref[...]).astype(BF16)
    for c0 in range(0, wq_ref.shape[1], 512):
        q_ref[:, c0:c0 + 512] = jnp.dot(xn, wq_ref[:, c0:c0 + 512], preferred_element_type=F32).astype(q_ref.dtype)
    qm_ref[...] = jnp.dot(xn, wqm_ref[...], preferred_element_type=F32).astype(qm_ref.dtype)


def _combine_proj(h, y, route, kv_norm_w, b_norm_w, w_kv, w_q, w_qm, tables, seq):
    t, d = h.shape
    tm = min(ROW_TILE, seq)
    per_block = min(MOE_BLOCK, t) // tm
    per_seq = seq // tm
    slot = lambda s: pl.BlockSpec((None, tm * PACK_ROWS, LANES),
                                  lambda i: (i // per_block, s * per_block + i % per_block, 0))
    rows = lambda w: pl.BlockSpec((tm, w), lambda i: (i, 0))
    return pl.pallas_call(
        _combine_proj_kernel, grid=(t // tm,),
        in_specs=[rows(d), slot(0), slot(1),
                  pl.BlockSpec((SUBLANES, tm), lambda i: (0, i)),
                  _resident((1, d)), _resident((1, d)),
                  _resident(w_kv.shape), _resident(w_q.shape), _resident(w_qm.shape),
                  pl.BlockSpec((3, tm, LANES), lambda i: (0, i % per_seq, 0))],
        out_specs=[rows(d), rows(w_kv.shape[1]), rows(w_q.shape[1]), rows(w_qm.shape[1])],
        out_shape=[jax.ShapeDtypeStruct((t, d), F32),
                   jax.ShapeDtypeStruct((t, w_kv.shape[1]), BF16),
                   jax.ShapeDtypeStruct((t, w_q.shape[1]), BF16),
                   jax.ShapeDtypeStruct((t, w_qm.shape[1]), BF16)],
        compiler_params=_params(), name="moe_combine_swa_in_proj",
    )(h, y, y, route, kv_norm_w.reshape(1, d), b_norm_w.reshape(1, d), w_kv, w_q, w_qm, tables)


def _moe_experts(xt, route, tile_counts, w_gate, w_up, w_down, t, d):
    w_gu = jnp.concatenate([w_gate, w_up], axis=-1).astype(BF16)
    return _moe(xt, route, tile_counts, w_gu, w_down.astype(BF16), t, d)


def kernel(x, mem, a_norm_w, a_in_w, a_conv_w, a_conv_b, a_dt_bias, a_a_log, a_d_skip, a_gnorm_w, a_out_w, kv_norm_w, w_kv, b_norm_w, b_in_w, b_sinks, b_out_w, mem_norm_w, mem_w_kv, ffn_norm_w, router_group_w, router_group_b, router_expert_w, router_expert_b, w_gate, w_up, w_down, final_norm_w):
    bsz, seq, d = x.shape
    mem_len = mem.shape[1]
    t = bsz * seq
    h = x.reshape(t, d)
    mem2 = mem.reshape(bsz * mem_len, d)
    conv_ch = a_conv_w.shape[-1]
    n_a = a_in_w.shape[0]
    n_layers = n_a + b_in_w.shape[0]
    qw = Q_HEADS * HEAD_DIM
    tables = _rope_tables(seq)
    kv_sh = q = q_mem = None
    for layer in range(n_layers):
        (mem_kv,) = _norm_matmul(mem2, mem_norm_w[layer], mem_w_kv[layer].astype(BF16), [2 * MEM_WIDTH], [BF16],
                                 name=f"mem_kv_{layer}")
        if layer < n_a:
            i = layer
            w_in = a_in_w[i]
            o_dt, o_q = SSD_INNER + conv_ch, SSD_INNER + conv_ch + SSD_HEADS
            w_all = jnp.concatenate(
                [w_in[:, :o_dt], w_in[:, o_q:], w_in[:, o_dt:o_q], jnp.zeros((d, LANES - SSD_HEADS), F32)],
                axis=1).astype(BF16)
            z, xbc, q_mem, dt = _norm_matmul(h, a_norm_w[i], w_all, [SSD_INNER, conv_ch, MEM_WIDTH, LANES],
                                             [BF16, BF16, BF16, F32], name=f"ssd_in_proj_{i}")
            y = _ssd(xbc, z, dt, a_conv_w[i], a_conv_b[i], a_dt_bias[i], a_a_log[i], a_d_skip[i],
                     a_gnorm_w[i], bsz, seq)
            m_att = _mem_attn(q_mem, mem_kv, bsz, seq, mem_len, name=f"mem_attn_{layer}")
            w_out = a_out_w[i].astype(BF16)
            h, xt, route, tile_counts = _mix_out_route(
                y, m_att, w_out[:SSD_INNER], w_out[SSD_INNER:], h, ffn_norm_w[layer],
                router_group_w[layer], router_group_b[layer], router_expert_w[layer], router_expert_b[layer],
                name=f"ssd_out_proj_route_{i}")
        else:
            j = layer - n_a
            if q is None:
                if j == 0:
                    kv_sh = _shared_kv(h, kv_norm_w, w_kv.astype(BF16), tables, seq)
                w_in = b_in_w[j].astype(BF16)
                q, q_mem = _norm_matmul(h, b_norm_w[j], w_in, [qw, MEM_WIDTH], [BF16, BF16],
                                        name=f"swa_in_proj_{j}")
            att = _swa(q, kv_sh, b_sinks[j], tables, bsz, seq)
            m_att = _mem_attn(q_mem, mem_kv, bsz, seq, mem_len, name=f"mem_attn_{layer}")
            w_out = b_out_w[j].astype(BF16)
            h, xt, route, tile_counts = _mix_out_route(
                att, m_att, w_out[:qw], w_out[qw:], h, ffn_norm_w[layer],
                router_group_w[layer], router_group_b[layer], router_expert_w[layer], router_expert_b[layer],
                name=f"swa_out_proj_route_{j}")
            q = None
        y_moe = _moe_experts(xt, route, tile_counts, w_gate[layer], w_up[layer], w_down[layer], t, d)
        if layer + 1 == n_a and layer + 1 < n_layers:
            w_in = b_in_w[0].astype(BF16)
            h, kv_sh, q, q_mem = _combine_proj(h, y_moe, route, kv_norm_w, b_norm_w[0], w_kv.astype(BF16),
                                               w_in[:, :qw], w_in[:, qw:], tables, seq)
        else:
            last = layer == n_layers - 1
            h = _combine(h, y_moe, route, final_norm_w, last, name=f"moe_combine_{layer}")
    return h.reshape(bsz, seq, d)
```

```python
import functools

import jax
import jax.numpy as jnp
from jax import lax
from jax.experimental import pallas as pl
from jax.experimental.pallas import tpu as pltpu

F32 = jnp.float32
BF16 = jnp.bfloat16
NORM_EPS = 1e-6
LOG2_E = 1.4426950408889634

LANES = 128
SUBLANES = 8
VMEM_LIMIT = 56 * 1024 * 1024

MEM_HEADS = 4
MEM_HEAD_DIM = 128
MEM_WIDTH = MEM_HEADS * MEM_HEAD_DIM
SSD_HEAD_DIM = 64
SSD_HEADS = 24
SSD_INNER = SSD_HEADS * SSD_HEAD_DIM
SSD_GROUPS = 4
SSD_STATE = 128
SSD_CHUNK = 128
CONV_WIDTH = 4
HEAD_DIM = 64
Q_HEADS = 16
KV_HEADS = 4
ATTN_BLOCK = 128
ROT_DIM = 16
ROPE_THETA = 500000.0
N_GROUPS = 4
EXPERTS_PER_GROUP = 8
N_EXPERTS = 32
D_EXPERT = 512

ROW_TILE = 512
MOE_BLOCK = 4096
MOE_UNIT = 64
MOE_CLASSES = 8
MOE_ROWS = MOE_UNIT * MOE_CLASSES
MOE_STRIDE = MOE_ROWS + SUBLANES
PACK = 2
PACK_ROWS = SUBLANES // PACK
HIGH_HALF = 0xFFFF0000


def _params():
    return pltpu.CompilerParams(vmem_limit_bytes=VMEM_LIMIT)


def _resident(shape):
    nd = len(shape)
    return pl.BlockSpec(shape, lambda *_: (0,) * nd, pipeline_mode=pl.Buffered(1))


def _rms(x, w):
    ms = jnp.mean(x * x, axis=-1, keepdims=True)
    return x * lax.rsqrt(ms + NORM_EPS) * w


def _silu(x):
    u = 0.5 * x
    return u * jnp.tanh(u) + u


def _pack_pairs(x):
    half = x.shape[1] // PACK
    bits = lax.bitcast_convert_type(x.astype(BF16).astype(F32), jnp.uint32)
    return [(bits[:, j * LANES:(j + 1) * LANES] >> 16)
            | (bits[:, half + j * LANES:half + (j + 1) * LANES] & jnp.uint32(HIGH_HALF))
            for j in range(half // LANES)]


def _unpack_pairs(words):
    lo = [lax.bitcast_convert_type(w << 16, F32) for w in words]
    hi = [lax.bitcast_convert_type(w & jnp.uint32(HIGH_HALF), F32) for w in words]
    return jnp.concatenate(lo + hi, axis=-1)


def _norm_matmul_kernel(x_ref, nw_ref, w_ref, *o_refs):
    xn = _rms(x_ref[...], nw_ref[...]).astype(BF16)
    lo = 0
    for o_ref in o_refs:
        n = o_ref.shape[1]
        for c0 in range(0, n, 512):
            cw = min(512, n - c0)
            acc = jnp.dot(xn, w_ref[:, lo + c0:lo + c0 + cw], preferred_element_type=F32)
            o_ref[:, c0:c0 + cw] = acc.astype(o_ref.dtype)
        lo += n


def _norm_matmul(x, norm_w, w, widths, out_dtypes, name):
    t, d = x.shape
    tm = min(ROW_TILE, t)
    assert sum(widths) == w.shape[1]
    return pl.pallas_call(
        _norm_matmul_kernel, grid=(t // tm,),
        in_specs=[pl.BlockSpec((tm, d), lambda i: (i, 0)), _resident((1, d)), _resident(w.shape)],
        out_specs=[pl.BlockSpec((tm, n), lambda i: (i, 0)) for n in widths],
        out_shape=[jax.ShapeDtypeStruct((t, n), dt) for n, dt in zip(widths, out_dtypes)],
        compiler_params=_params(), name=name,
    )(x, norm_w.reshape(1, d), w)


def _mix_out_route_kernel(a1_ref, a2_ref, w1_ref, w2_ref, r_ref, nw_ref, wr_ref, br_ref,
                          o_ref, xt_ref, route_ref, cnt_ref):
    acc = jnp.dot(a1_ref[...], w1_ref[...], preferred_element_type=F32)
    acc = acc + jnp.dot(a2_ref[...], w2_ref[...], preferred_element_type=F32)
    h = r_ref[...] + acc
    o_ref[...] = h
    _route_tokens(h, nw_ref, wr_ref, br_ref, xt_ref, route_ref, cnt_ref)


def _mix_out_route(a1, a2, w1, w2, res, norm_w, rg_w, rg_b, re_w, re_b, name):
    t, d = res.shape
    tm = min(ROW_TILE, t)
    wr, br = _router_weights(rg_w, rg_b, re_w, re_b)
    rows = lambda w: pl.BlockSpec((tm, w), lambda i: (i, 0))
    return pl.pallas_call(
        _mix_out_route_kernel, grid=(t // tm,),
        in_specs=[rows(a1.shape[1]), rows(a2.shape[1]), _resident(w1.shape), _resident(w2.shape), rows(d),
                  _resident((1, d)), _resident(wr.shape), _resident(br.shape)],
        out_specs=[rows(d),
                   pl.BlockSpec((tm * PACK_ROWS, LANES), lambda i: (i, 0)),
                   pl.BlockSpec((SUBLANES, tm), lambda i: (0, i)),
                   pl.BlockSpec((N_EXPERTS, LANES), lambda i: (i, 0))],
        out_shape=[jax.ShapeDtypeStruct((t, d), F32),
                   jax.ShapeDtypeStruct((t * PACK_ROWS, LANES), jnp.uint32),
                   jax.ShapeDtypeStruct((SUBLANES, t), F32),
                   jax.ShapeDtypeStruct((t // tm * N_EXPERTS, LANES), F32)],
        compiler_params=_params(), name=name,
    )(a1, a2, w1, w2, res, norm_w.reshape(1, d), wr, br)


def _mem_attn_kernel(q_ref, kv_ref, o_ref):
    scale = MEM_HEAD_DIM ** -0.5
    cols = lambda h: slice(h * MEM_HEAD_DIM, (h + 1) * MEM_HEAD_DIM)
    scores = [lax.dot_general(q_ref[:, cols(h)], kv_ref[:, cols(h)], (((1,), (1,)), ((), ())),
                              preferred_element_type=F32) * scale for h in range(MEM_HEADS)]
    probs = [jnp.exp(s - jnp.max(s, axis=-1, keepdims=True)) for s in scores]
    probs = [p / jnp.sum(p, axis=-1, keepdims=True) for p in probs]
    outs = [jnp.dot(p.astype(BF16), kv_ref[:, MEM_WIDTH + h * MEM_HEAD_DIM:MEM_WIDTH + (h + 1) * MEM_HEAD_DIM],
                    preferred_element_type=F32) for h, p in enumerate(probs)]
    o_ref[...] = jnp.concatenate(outs, axis=-1).astype(o_ref.dtype)


def _mem_attn(q, kv, bsz, seq, mem_len, name):
    tq = min(ROW_TILE, seq)
    nq = seq // tq
    return pl.pallas_call(
        _mem_attn_kernel, grid=(bsz, nq),
        in_specs=[pl.BlockSpec((tq, MEM_WIDTH), lambda b, i: (b * nq + i, 0)),
                  pl.BlockSpec((mem_len, 2 * MEM_WIDTH), lambda b, i: (b, 0))],
        out_specs=pl.BlockSpec((tq, MEM_WIDTH), lambda b, i: (b * nq + i, 0)),
        out_shape=jax.ShapeDtypeStruct((bsz * seq, MEM_WIDTH), BF16),
        compiler_params=_params(), name=name,
    )(q, kv)


def _ssd_kernel(xbc_ref, z_ref, dt_ref, cw_ref, cb_ref, dtb_ref, alog_ref, dsk_ref, gnw_ref,
                expand_ref, o_ref, ext_ref, st_ref):
    c = pl.program_id(1)
    L = SSD_CHUNK
    gw = SSD_INNER // SSD_GROUPS
    hpg = SSD_HEADS // SSD_GROUPS

    @pl.when(c == 0)
    def _():
        ext_ref[0:SUBLANES, :] = jnp.zeros((SUBLANES, ext_ref.shape[1]), F32)
        st_ref[...] = jnp.zeros_like(st_ref)

    @pl.when(c > 0)
    def _():
        ext_ref[0:SUBLANES, :] = ext_ref[L:L + SUBLANES, :]

    ext_ref[SUBLANES:SUBLANES + L, :] = xbc_ref[...].astype(F32)
    conv = cb_ref[...]
    for k in range(CONV_WIDTH):
        conv = conv + cw_ref[k:k + 1, :] * ext_ref[pl.ds(SUBLANES - (CONV_WIDTH - 1) + k, L), :]
    xbc = _silu(conv)
    xs = xbc[:, :SSD_INNER]
    bm = xbc[:, SSD_INNER:SSD_INNER + SSD_GROUPS * SSD_STATE]
    cm = xbc[:, SSD_INNER + SSD_GROUPS * SSD_STATE:].astype(BF16)

    pre = dt_ref[...] + dtb_ref[...]
    dt = jnp.maximum(pre, 0.0) + jnp.log(1.0 + jnp.exp(-jnp.abs(pre)))
    a = -jnp.exp(alog_ref[...])
    adt = dt * a
    row = lax.broadcasted_iota(jnp.int32, (L, L), 0)
    col = lax.broadcasted_iota(jnp.int32, (L, L), 1)
    causal = row >= col
    acum = jnp.dot(causal.astype(F32), adt, preferred_element_type=F32,
                   precision=lax.Precision.HIGHEST)
    acum_t = acum.T
    ea = jnp.exp(acum)
    te = jnp.exp(acum[L - 1:L, :] - acum)
    widen = lambda cols: jnp.dot(cols.astype(BF16), expand_ref[...], preferred_element_type=F32)
    dtx, eax, tex = widen(dt), widen(ea), widen(te)
    xdt = xs * dtx
    xdt_b = xdt.astype(BF16)
    xw = (xdt * tex).astype(BF16)

    first_half = col < SSD_HEAD_DIM
    ys = []
    for g in range(SSD_GROUPS):
        cg = cm[:, g * SSD_STATE:(g + 1) * SSD_STATE]
        bg_t = bm[:, g * SSD_STATE:(g + 1) * SSD_STATE].T.astype(BF16)
        cb = jnp.dot(cg, bg_t, preferred_element_type=F32)
        state = st_ref[g]
        y_off = jnp.dot(cg, state.astype(BF16), preferred_element_type=F32) * eax[:, g * gw:(g + 1) * gw]
        parts = []
        for r in range(0, hpg, 2):
            both = []
            for h in (g * hpg + r, g * hpg + r + 1):
                seg = acum[:, h:h + 1] - acum_t[h:h + 1, :]
                m = (cb * jnp.exp(jnp.where(causal, seg, -jnp.inf))).astype(BF16)
                lo = (g * hpg + r) * SSD_HEAD_DIM
                both.append(jnp.dot(m, xdt_b[:, lo:lo + LANES], preferred_element_type=F32))
            parts.append(jnp.where(first_half, both[0], both[1]))
        ys.append(jnp.concatenate(parts, axis=-1) + y_off)
        upd = jnp.dot(bg_t, xw[:, g * gw:(g + 1) * gw], preferred_element_type=F32)
        st_ref[g] = state * eax[L - 1:L, g * gw:(g + 1) * gw] + upd
    y = jnp.concatenate(ys, axis=-1) + dsk_ref[...] * xs
    zf = z_ref[...].astype(F32)
    y = y * _silu(zf)
    normed = []
    for g in range(SSD_GROUPS):
        yg = y[:, g * gw:(g + 1) * gw]
        normed.append(yg * lax.rsqrt(jnp.mean(yg * yg, axis=-1, keepdims=True) + NORM_EPS))
    o_ref[...] = (jnp.concatenate(normed, axis=-1) * gnw_ref[...]).astype(o_ref.dtype)


def _ssd(xbc, z, dt, conv_w, conv_b, dt_bias, a_log, d_skip, gnorm_w, bsz, seq):
    nc = seq // SSD_CHUNK
    conv_ch = xbc.shape[1]
    pad = LANES - SSD_HEADS
    dtb = jnp.pad(dt_bias, (0, pad)).reshape(1, LANES)
    alog = jnp.pad(a_log, (0, pad)).reshape(1, LANES)
    dsk = jnp.repeat(d_skip, SSD_HEAD_DIM).reshape(1, SSD_INNER)
    expand = (jnp.arange(LANES)[:, None] == (jnp.arange(SSD_INNER) // SSD_HEAD_DIM)[None, :]).astype(BF16)
    blk = lambda w: pl.BlockSpec((SSD_CHUNK, w), lambda b, c: (b * nc + c, 0))
    return pl.pallas_call(
        _ssd_kernel, grid=(bsz, nc),
        in_specs=[blk(conv_ch), blk(SSD_INNER), blk(LANES),
                  _resident((CONV_WIDTH, conv_ch)), _resident((1, conv_ch)),
                  _resident((1, LANES)), _resident((1, LANES)),
                  _resident((1, SSD_INNER)), _resident((1, SSD_INNER)),
                  _resident((LANES, SSD_INNER))],
        out_specs=blk(SSD_INNER),
        out_shape=jax.ShapeDtypeStruct((bsz * seq, SSD_INNER), BF16),
        scratch_shapes=[pltpu.VMEM((SSD_CHUNK + 2 * SUBLANES, conv_ch), F32),
                        pltpu.VMEM((SSD_GROUPS, SSD_STATE, SSD_INNER // SSD_GROUPS), F32)],
        compiler_params=_params(), name="ssd_mixer",
    )(xbc, z, dt, conv_w, conv_b.reshape(1, conv_ch), dtb, alog, dsk,
      gnorm_w.reshape(1, SSD_INNER), expand)


def _rope(x, cos, sin_up, sin_dn):
    w = x.shape[1]
    reps = w // LANES
    tile = lambda t: jnp.concatenate([t] * reps, axis=-1) if reps > 1 else t
    half = ROT_DIM // 2
    return (x * tile(cos) + pltpu.roll(x, half, axis=1) * tile(sin_up)
            + pltpu.roll(x, w - half, axis=1) * tile(sin_dn))


def _swa_kernel(sink_ref, q_ref, kvp_ref, kvc_ref, tc_ref, bias_ref, o_ref):
    kvw = KV_HEADS * HEAD_DIM
    g = Q_HEADS // KV_HEADS
    scale = HEAD_DIM ** -0.5 * LOG2_E
    q = _rope(q_ref[...].astype(F32), tc_ref[0], tc_ref[1], tc_ref[2]) * scale
    q_t = q.T.astype(BF16)
    k = jnp.concatenate([kvp_ref[:, :kvw], kvc_ref[:, :kvw]], axis=0)
    v = jnp.concatenate([kvp_ref[:, kvw:], kvc_ref[:, kvw:]], axis=0).astype(F32)
    v_t = v.T.astype(BF16)
    bias = bias_ref[0]
    heads = lambda kh: range(kh * g, (kh + 1) * g)
    scores = [jnp.concatenate(
        [jnp.dot(k[:, kh * HEAD_DIM:(kh + 1) * HEAD_DIM], q_t[h * HEAD_DIM:(h + 1) * HEAD_DIM, :],
                 preferred_element_type=F32) for h in heads(kh)], axis=-1) + bias
        for kh in range(KV_HEADS)]
    sinks = [jnp.concatenate([jnp.full((1, ATTN_BLOCK), sink_ref[h] * LOG2_E, F32) for h in heads(kh)], axis=-1)
             for kh in range(KV_HEADS)]
    maxes = [jnp.maximum(jnp.max(sc, axis=0, keepdims=True), sk) for sc, sk in zip(scores, sinks)]
    probs = [jnp.exp2(sc - mx) for sc, mx in zip(scores, maxes)]
    denoms = [jnp.sum(p, axis=0, keepdims=True) + jnp.exp2(sk - mx) for p, sk, mx in zip(probs, sinks, maxes)]
    outs = []
    for kh in range(KV_HEADS):
        o_t = jnp.dot(v_t[kh * HEAD_DIM:(kh + 1) * HEAD_DIM, :], probs[kh].astype(BF16),
                      preferred_element_type=F32) / denoms[kh]
        outs += [o_t[:, j * ATTN_BLOCK:(j + 1) * ATTN_BLOCK] for j in range(g)]
    o_ref[...] = jnp.concatenate(outs, axis=0).T.astype(o_ref.dtype)


def _swa_bias(g):
    blk = ATTN_BLOCK
    s_idx = jnp.arange(2 * blk)[:, None]
    t_idx = jnp.arange(blk)[None, :]
    band = (s_idx > t_idx) & (s_idx <= t_idx + blk)
    first = band & (s_idx >= blk)
    both = jnp.stack([first, band], axis=0)
    return jnp.tile(jnp.where(both, 0.0, -jnp.inf).astype(F32), (1, 1, g))


def _rope_tables(seq):
    half = ROT_DIM // 2
    inv_freq = ROPE_THETA ** (-2.0 * jnp.arange(half, dtype=F32) / ROT_DIM)
    ang = jnp.arange(seq, dtype=jnp.int32).astype(F32)[:, None] * inv_freq[None, :]
    cos, sin = jnp.cos(ang), jnp.sin(ang)
    ones = jnp.ones((seq, HEAD_DIM - ROT_DIM), F32)
    zeros = jnp.zeros((seq, HEAD_DIM - ROT_DIM), F32)
    zh = jnp.zeros((seq, half), F32)
    cos_h = jnp.concatenate([cos, cos, ones], axis=-1)
    up_h = jnp.concatenate([zh, sin, zeros], axis=-1)
    dn_h = jnp.concatenate([-sin, zh, zeros], axis=-1)
    two = lambda t: jnp.concatenate([t, t], axis=-1)
    return jnp.stack([two(cos_h), two(up_h), two(dn_h)], axis=0)


def _shared_kv_kernel(x_ref, nw_ref, w_ref, t_ref, o_ref):
    kvw = KV_HEADS * HEAD_DIM
    xn = _rms(x_ref[...], nw_ref[...]).astype(BF16)
    kv = jnp.dot(xn, w_ref[...], preferred_element_type=F32)
    k = _rope(kv[:, :kvw], t_ref[0], t_ref[1], t_ref[2])
    o_ref[...] = jnp.concatenate([k, kv[:, kvw:]], axis=-1).astype(o_ref.dtype)


def _shared_kv(h, norm_w, w, tables, seq):
    t, d = h.shape
    tm = min(ROW_TILE, seq)
    per_seq = seq // tm
    return pl.pallas_call(
        _shared_kv_kernel, grid=(t // tm,),
        in_specs=[pl.BlockSpec((tm, d), lambda i: (i, 0)), _resident((1, d)), _resident(w.shape),
                  pl.BlockSpec((3, tm, LANES), lambda i: (0, i % per_seq, 0))],
        out_specs=pl.BlockSpec((tm, w.shape[1]), lambda i: (i, 0)),
        out_shape=jax.ShapeDtypeStruct((t, w.shape[1]), BF16),
        compiler_params=_params(), name="shared_kv",
    )(h, norm_w.reshape(1, d), w, tables)


def _swa(q, kv, sinks, tables, bsz, seq):
    nb = seq // ATTN_BLOCK
    qw = Q_HEADS * HEAD_DIM
    kvw2 = 2 * KV_HEADS * HEAD_DIM
    prev = lambda i: jnp.maximum(i - 1, 0)
    g = Q_HEADS // KV_HEADS
    return pl.pallas_call(
        _swa_kernel, grid=(bsz, nb),
        in_specs=[pl.BlockSpec(memory_space=pltpu.SMEM),
                  pl.BlockSpec((ATTN_BLOCK, qw), lambda b, i: (b * nb + i, 0)),
                  pl.BlockSpec((ATTN_BLOCK, kvw2), lambda b, i: (b * nb + prev(i), 0)),
                  pl.BlockSpec((ATTN_BLOCK, kvw2), lambda b, i: (b * nb + i, 0)),
                  pl.BlockSpec((3, ATTN_BLOCK, LANES), lambda b, i: (0, i, 0)),
                  pl.BlockSpec((1, 2 * ATTN_BLOCK, g * ATTN_BLOCK), lambda b, i: (jnp.minimum(i, 1), 0, 0))],
        out_specs=pl.BlockSpec((ATTN_BLOCK, qw), lambda b, i: (b * nb + i, 0)),
        out_shape=jax.ShapeDtypeStruct((bsz * seq, qw), BF16),
        compiler_params=_params(), name="swa_attention",
    )(sinks, q, kv, kv, tables, _swa_bias(g))


def _route_tokens(h, nw_ref, wr_ref, br_ref, xt_ref, route_ref, cnt_ref):
    tm = h.shape[0]
    xn = _rms(h, nw_ref[...])
    x_hi = xn.astype(BF16)
    for j, word in enumerate(_pack_pairs(xn)):
        xt_ref[pl.ds(j, tm, stride=PACK_ROWS), :] = word
    x_lo = (xn - x_hi.astype(F32)).astype(BF16)
    both = jnp.dot(x_hi, wr_ref[...], preferred_element_type=F32)
    logits = (both[:, :LANES] + both[:, LANES:]
              + jnp.dot(x_lo, wr_ref[:, :LANES], preferred_element_type=F32) + br_ref[...])
    lt = logits.T
    neg = -jnp.inf
    grow = lax.broadcasted_iota(jnp.int32, (SUBLANES, tm), 0)
    gl = jnp.where(grow < N_GROUPS, lt[:SUBLANES], neg)
    gmax = jnp.max(gl, axis=0, keepdims=True)
    g_w = 1.0 / jnp.sum(jnp.exp(gl - gmax), axis=0, keepdims=True)
    g_idx = jnp.min(jnp.where(gl == gmax, grow, SUBLANES), axis=0, keepdims=True)
    erow = lax.broadcasted_iota(jnp.int32, (N_EXPERTS, tm), 0)
    el = jnp.where(erow // EXPERTS_PER_GROUP == g_idx, lt[SUBLANES:SUBLANES + N_EXPERTS], neg)
    m1 = jnp.max(el, axis=0, keepdims=True)
    i1 = jnp.min(jnp.where(el == m1, erow, N_EXPERTS), axis=0, keepdims=True)
    el2 = jnp.where(erow == i1, neg, el)
    m2 = jnp.max(el2, axis=0, keepdims=True)
    i2 = jnp.min(jnp.where(el2 == m2, erow, N_EXPERTS), axis=0, keepdims=True)
    d = jnp.exp(m2 - m1)
    w1 = g_w / (1.0 + d)
    w2 = g_w * d / (1.0 + d)
    zero = jnp.zeros((SUBLANES - 4, tm), F32)
    route_ref[...] = jnp.concatenate([i1.astype(F32), i2.astype(F32), w1, w2, zero], axis=0)
    chosen = ((erow == i1) | (erow == i2)).astype(F32)
    cnt_ref[...] = jnp.broadcast_to(jnp.sum(chosen, axis=1, keepdims=True), cnt_ref.shape)


def _router_weights(rg_w, rg_b, re_w, re_b):
    d = rg_w.shape[0]
    gap = jnp.zeros((d, SUBLANES - N_GROUPS), F32)
    tail = jnp.zeros((d, LANES - SUBLANES - N_EXPERTS), F32)
    wr = jnp.concatenate([rg_w, gap, re_w, tail], axis=1)
    w_hi = wr.astype(BF16)
    w_lo = (wr - w_hi.astype(F32)).astype(BF16)
    br = jnp.concatenate([rg_b, gap[0], re_b, tail[0]]).reshape(1, LANES)
    return jnp.concatenate([w_hi, w_lo], axis=1), br


def _moe_kernel(st_ref, cnt_ref, tok_ref, dst_ref, xs_ref, wgu_ref, wd_ref, o_ref,
                tile_ref, xb_ref, ybuf_ref, state_ref):
    b = pl.program_id(0)
    e = pl.program_id(1)
    stride = MOE_STRIDE
    unroll = SUBLANES
    READY, PEND_BASE, PEND_ROWS = 0, 1, 2

    @pl.when((b == 0) & (e == 0))
    def _():
        ybuf_ref[...] = jnp.zeros_like(ybuf_ref)

    @pl.when(e == 0)
    def _():
        state_ref[READY] = 0
        state_ref[PEND_BASE] = 0
        state_ref[PEND_ROWS] = 0

    start = st_ref[b * N_EXPERTS + e]
    count = cnt_ref[b * N_EXPERTS + e]

    def gather_row(base, r):
        tok = tok_ref[0, 0, base + r]
        src = pl.ds(pl.multiple_of(tok * PACK_ROWS, PACK_ROWS), PACK_ROWS)
        tile_ref[pl.ds(r, PACK_ROWS, stride=stride), :] = xs_ref[src, :]

    def scatter_row(base, r):
        dst = dst_ref[0, 0, base + r]
        o_ref[pl.ds(pl.multiple_of(dst * PACK_ROWS, PACK_ROWS), PACK_ROWS), :] = (
            ybuf_ref[pl.ds(r, PACK_ROWS, stride=stride), :])

    def copy_rows(row_fn, base, lo, hi):
        def group(i, c):
            for u in range(unroll):
                row_fn(base, i * unroll + u)
            return c
        lax.fori_loop(lo // unroll, hi // unroll, group, 0)

    def expert_mlp(rows, next_base, pend_base):
        def run():
            words = [tile_ref[pl.ds(j * stride, rows), :] for j in range(PACK_ROWS)]
            xb_ref[0:rows, :] = _unpack_pairs(words).astype(BF16)
            for r in range(rows):
                gather_row(next_base, r)
                scatter_row(pend_base, r)
            gu = jnp.dot(xb_ref[0:rows, :], wgu_ref[...], preferred_element_type=F32)
            gate, up = gu[:, :D_EXPERT], gu[:, D_EXPERT:]
            he = (_silu(gate) * up).astype(BF16)
            y = jnp.dot(he, wd_ref[...], preferred_element_type=F32)
            for j, word in enumerate(_pack_pairs(y)):
                ybuf_ref[pl.ds(j * stride, rows), :] = word
        return run

    def tile_body(t, carry):
        base = start + t * MOE_ROWS
        left = jnp.minimum(count - t * MOE_ROWS, MOE_ROWS)
        rows = (left + MOE_UNIT - 1) // MOE_UNIT * MOE_UNIT
        copy_rows(gather_row, base, state_ref[READY], rows)
        pend_rows = state_ref[PEND_ROWS]
        pend_base = jnp.where(pend_rows > 0, state_ref[PEND_BASE], base)
        copy_rows(scatter_row, pend_base, jnp.minimum(rows, pend_rows), pend_rows)
        next_base = base + left
        lax.switch(rows // MOE_UNIT - 1, [expert_mlp(MOE_UNIT * (k + 1), next_base, pend_base)
                                          for k in range(MOE_CLASSES)])
        state_ref[READY] = rows
        state_ref[PEND_BASE] = base
        state_ref[PEND_ROWS] = rows
        return carry

    lax.fori_loop(0, (count + MOE_ROWS - 1) // MOE_ROWS, tile_body, 0)

    @pl.when(e == N_EXPERTS - 1)
    def _():
        copy_rows(scatter_row, state_ref[PEND_BASE], 0, state_ref[PEND_ROWS])


def _moe_plan(route, tile_counts, nb, m):
    pick = lambda r: route[r].reshape(nb, m)
    e_idx = jnp.concatenate([pick(0), pick(1)], axis=1).astype(jnp.int32)
    order = jnp.argsort(e_idx, axis=1, stable=True).astype(jnp.int32)
    counts = tile_counts[:, 0].reshape(nb, -1, N_EXPERTS).sum(axis=1).astype(jnp.int32)
    starts = jnp.cumsum(counts, axis=1) - counts
    tail = ((0, 0), (0, 2 * MOE_ROWS))
    tok_list = jnp.pad(order % m, tail)
    dst_list = jnp.pad(order, tail, constant_values=2 * m)
    as_list = lambda a: a.reshape(nb, 1, a.shape[-1])
    return starts.reshape(-1), counts.reshape(-1), as_list(tok_list), as_list(dst_list)


def _moe(xt, route, tile_counts, w_gu, w_down, t, d):
    m = min(MOE_BLOCK, t)
    nb = t // m
    starts, counts, tok_list, dst_list = _moe_plan(route, tile_counts, nb, m)
    length = tok_list.shape[-1]
    slots = 2 * m + PACK
    grid_spec = pltpu.PrefetchScalarGridSpec(
        num_scalar_prefetch=2, grid=(nb, N_EXPERTS),
        in_specs=[pl.BlockSpec((1, 1, length), lambda b, e, *_: (b, 0, 0), memory_space=pltpu.SMEM),
                  pl.BlockSpec((1, 1, length), lambda b, e, *_: (b, 0, 0), memory_space=pltpu.SMEM),
                  pl.BlockSpec((m * PACK_ROWS, LANES), lambda b, e, *_: (b, 0), pipeline_mode=pl.Buffered(1)),
                  pl.BlockSpec((None, d, 2 * D_EXPERT), lambda b, e, *_: (e, 0, 0)),
                  pl.BlockSpec((None, D_EXPERT, d), lambda b, e, *_: (e, 0, 0))],
        out_specs=pl.BlockSpec((None, slots * PACK_ROWS, LANES), lambda b, e, *_: (b, 0, 0)),
        scratch_shapes=[pltpu.VMEM((PACK_ROWS * MOE_STRIDE, LANES), jnp.uint32),
                        pltpu.VMEM((MOE_ROWS, d), BF16),
                        pltpu.VMEM((PACK_ROWS * MOE_STRIDE, LANES), jnp.uint32),
                        pltpu.SMEM((3,), jnp.int32)])
    return pl.pallas_call(
        _moe_kernel, grid_spec=grid_spec,
        out_shape=jax.ShapeDtypeStruct((nb, slots * PACK_ROWS, LANES), jnp.uint32),
        compiler_params=_params(), name="moe_experts",
    )(starts, counts, tok_list, dst_list, xt, w_gu, w_down)


def _combine_kernel(h_ref, y0_ref, y1_ref, route_ref, nw_ref, o_ref, *, final_norm):
    tm = h_ref.shape[0]
    token_rows = lambda ref: _unpack_pairs([ref[pl.ds(j, tm, stride=PACK_ROWS), :] for j in range(PACK_ROWS)])
    gates = route_ref[...].T
    out = h_ref[...] + gates[:, 2:3] * token_rows(y0_ref) + gates[:, 3:4] * token_rows(y1_ref)
    if final_norm:
        out = _rms(out, nw_ref[...])
    o_ref[...] = out


def _combine(h, y, route, norm_w, final_norm, name):
    t, d = h.shape
    tm = min(ROW_TILE, t)
    m = min(MOE_BLOCK, t)
    per_block = m // tm

    def slot(s):
        return pl.BlockSpec((None, tm * PACK_ROWS, LANES),
                            lambda i: (i // per_block, s * per_block + i % per_block, 0))

    return pl.pallas_call(
        functools.partial(_combine_kernel, final_norm=final_norm), grid=(t // tm,),
        in_specs=[pl.BlockSpec((tm, d), lambda i: (i, 0)),
                  slot(0), slot(1),
                  pl.BlockSpec((SUBLANES, tm), lambda i: (0, i)),
                  _resident((1, d))],
        out_specs=pl.BlockSpec((tm, d), lambda i: (i, 0)),
        out_shape=jax.ShapeDtypeStruct((t, d), F32),
        compiler_params=_params(), name=name,
    )(h, y, y, route, norm_w.reshape(1, d))


def _combine_proj_kernel(h_ref, y0_ref, y1_ref, route_ref, kvn_ref, bn_ref, wkv_ref, wq_ref, wqm_ref,
                         t_ref, ho_ref, kv_ref, q_ref, qm_ref):
    tm = h_ref.shape[0]
    kvw = KV_HEADS * HEAD_DIM
    token_rows = lambda ref: _unpack_pairs([ref[pl.ds(j, tm, stride=PACK_ROWS), :] for j in range(PACK_ROWS)])
    gates = route_ref[...].T
    out = h_ref[...] + gates[:, 2:3] * token_rows(y0_ref) + gates[:, 3:4] * token_rows(y1_ref)
    ho_ref[...] = out
    unit = out * lax.rsqrt(jnp.mean(out * out, axis=-1, keepdims=True) + NORM_EPS)
    kv = jnp.dot((unit * kvn_ref[...]).astype(BF16), wkv_ref[...], preferred_element_type=F32)
    k = _rope(kv[:, :kvw], t_ref[0], t_ref[1], t_ref[2])
    kv_ref[...] = jnp.concatenate([k, kv[:, kvw:]], axis=-1).astype(kv_ref.dtype)
    xn = (unit * bn_ref[...]).astype(BF16)
    for c0 in range(0, wq_ref.shape[1], 512):
        q_ref[:, c0:c0 + 512] = jnp.dot(xn, wq_ref[:, c0:c0 + 512], preferred_element_type=F32).astype(q_ref.dtype)
    qm_ref[...] = jnp.dot(xn, wqm_ref[...], preferred_element_type=F32).astype(qm_ref.dtype)


def _combine_proj(h, y, route, kv_norm_w, b_norm_w, w_kv, w_q, w_qm, tables, seq):
    t, d = h.shape
    tm = min(ROW_TILE, seq)
    per_block = min(MOE_BLOCK, t) // tm
    per_seq = seq // tm
    slot = lambda s: pl.BlockSpec((None, tm * PACK_ROWS, LANES),
                                  lambda i: (i // per_block, s * per_block + i % per_block, 0))
    rows = lambda w: pl.BlockSpec((tm, w), lambda i: (i, 0))
    return pl.pallas_call(
        _combine_proj_kernel, grid=(t // tm,),
        in_specs=[rows(d), slot(0), slot(1),
                  pl.BlockSpec((SUBLANES, tm), lambda i: (0, i)),
                  _resident((1, d)), _resident((1, d)),
                  _resident(w_kv.shape), _resident(w_q.shape), _resident(w_qm.shape),
                  pl.BlockSpec((3, tm, LANES), lambda i: (0, i % per_seq, 0))],
        out_specs=[rows(d), rows(w_kv.shape[1]), rows(w_q.shape[1]), rows(w_qm.shape[1])],
        out_shape=[jax.ShapeDtypeStruct((t, d), F32),
                   jax.ShapeDtypeStruct((t, w_kv.shape[1]), BF16),
                   jax.ShapeDtypeStruct((t, w_q.shape[1]), BF16),
                   jax.ShapeDtypeStruct((t, w_qm.shape[1]), BF16)],
        compiler_params=_params(), name="moe_combine_swa_in_proj",
    )(h, y, y, route, kv_norm_w.reshape(1, d), b_norm_w.reshape(1, d), w_kv, w_q, w_qm, tables)


def _moe_experts(xt, route, tile_counts, w_gate, w_up, w_down, t, d):
    w_gu = jnp.concatenate([w_gate, w_up], axis=-1).astype(BF16)
    return _moe(xt, route, tile_counts, w_gu, w_down.astype(BF16), t, d)


def kernel(x, mem, a_norm_w, a_in_w, a_conv_w, a_conv_b, a_dt_bias, a_a_log, a_d_skip, a_gnorm_w, a_out_w, kv_norm_w, w_kv, b_norm_w, b_in_w, b_sinks, b_out_w, mem_norm_w, mem_w_kv, ffn_norm_w, router_group_w, router_group_b, router_expert_w, router_expert_b, w_gate, w_up, w_down, final_norm_w):
    bsz, seq, d = x.shape
    mem_len = mem.shape[1]
    t = bsz * seq
    h = x.reshape(t, d)
    mem2 = mem.reshape(bsz * mem_len, d)
    conv_ch = a_conv_w.shape[-1]
    n_a = a_in_w.shape[0]
    n_layers = n_a + b_in_w.shape[0]
    qw = Q_HEADS * HEAD_DIM
    tables = _rope_tables(seq)
    kv_sh = q = q_mem = None
    for layer in range(n_layers):
        (mem_kv,) = _norm_matmul(mem2, mem_norm_w[layer], mem_w_kv[layer].astype(BF16), [2 * MEM_WIDTH], [BF16],
                                 name=f"mem_kv_{layer}")
        if layer < n_a:
            i = layer
            w_in = a_in_w[i]
            o_dt, o_q = SSD_INNER + conv_ch, SSD_INNER + conv_ch + SSD_HEADS
            w_all = jnp.concatenate(
                [w_in[:, :o_dt], w_in[:, o_q:], w_in[:, o_dt:o_q], jnp.zeros((d, LANES - SSD_HEADS), F32)],
                axis=1).astype(BF16)
            z, xbc, q_mem, dt = _norm_matmul(h, a_norm_w[i], w_all, [SSD_INNER, conv_ch, MEM_WIDTH, LANES],
                                             [BF16, BF16, BF16, F32], name=f"ssd_in_proj_{i}")
            y = _ssd(xbc, z, dt, a_conv_w[i], a_conv_b[i], a_dt_bias[i], a_a_log[i], a_d_skip[i],
                     a_gnorm_w[i], bsz, seq)
            m_att = _mem_attn(q_mem, mem_kv, bsz, seq, mem_len, name=f"mem_attn_{layer}")
            w_out = a_out_w[i].astype(BF16)
            h, xt, route, tile_counts = _mix_out_route(
                y, m_att, w_out[:SSD_INNER], w_out[SSD_INNER:], h, ffn_norm_w[layer],
                router_group_w[layer], router_group_b[layer], router_expert_w[layer], router_expert_b[layer],
                name=f"ssd_out_proj_route_{i}")
        else:
            j = layer - n_a
            if q is None:
                if j == 0:
                    kv_sh = _shared_kv(h, kv_norm_w, w_kv.astype(BF16), tables, seq)
                w_in = b_in_w[j].astype(BF16)
                q, q_mem = _norm_matmul(h, b_norm_w[j], w_in, [qw, MEM_WIDTH], [BF16, BF16],
                                        name=f"swa_in_proj_{j}")
            att = _swa(q, kv_sh, b_sinks[j], tables, bsz, seq)
            m_att = _mem_attn(q_mem, mem_kv, bsz, seq, mem_len, name=f"mem_attn_{layer}")
            w_out = b_out_w[j].astype(BF16)
            h, xt, route, tile_counts = _mix_out_route(
                att, m_att, w_out[:qw], w_out[qw:], h, ffn_norm_w[layer],
                router_group_w[layer], router_group_b[layer], router_expert_w[layer], router_expert_b[layer],
                name=f"swa_out_proj_route_{j}")
            q = None
        y_moe = _moe_experts(xt, route, tile_counts, w_gate[layer], w_up[layer], w_down[layer], t, d)
        if layer + 1 == n_a and layer + 1 < n_layers:
            w_in = b_in_w[0].astype(BF16)
            h, kv_sh, q, q_mem = _combine_proj(h, y_moe, route, kv_norm_w, b_norm_w[0], w_kv.astype(BF16),
                                               w_in[:, :qw], w_in[:, qw:], tables, seq)
        else:
            last = layer == n_layers - 1
            h = _combine(h, y_moe, route, final_norm_w, last, name=f"moe_combine_{layer}")
    return h.reshape(bsz, seq, d)
```

```python
import functools

import jax
import jax.numpy as jnp
from jax import lax
from jax.experimental import pallas as pl
from jax.experimental.pallas import tpu as pltpu

F32 = jnp.float32
BF16 = jnp.bfloat16
NORM_EPS = 1e-6
LOG2_E = 1.4426950408889634

LANES = 128
SUBLANES = 8
VMEM_LIMIT = 56 * 1024 * 1024

MEM_HEADS = 4
MEM_HEAD_DIM = 128
MEM_WIDTH = MEM_HEADS * MEM_HEAD_DIM
SSD_HEAD_DIM = 64
SSD_HEADS = 24
SSD_INNER = SSD_HEADS * SSD_HEAD_DIM
SSD_GROUPS = 4
SSD_STATE = 128
SSD_CHUNK = 128
CONV_WIDTH = 4
HEAD_DIM = 64
Q_HEADS = 16
KV_HEADS = 4
ATTN_BLOCK = 128
ROT_DIM = 16
ROPE_THETA = 500000.0
N_GROUPS = 4
EXPERTS_PER_GROUP = 8
N_EXPERTS = 32
D_EXPERT = 512

ROW_TILE = 512
MOE_BLOCK = 4096
MOE_UNIT = 64
MOE_CLASSES = 8
MOE_ROWS = MOE_UNIT * MOE_CLASSES
MOE_STRIDE = MOE_ROWS + SUBLANES
PACK = 2
PACK_ROWS = SUBLANES // PACK
HIGH_HALF = 0xFFFF0000


def _params():
    return pltpu.CompilerParams(vmem_limit_bytes=VMEM_LIMIT)


def _resident(shape):
    nd = len(shape)
    return pl.BlockSpec(shape, lambda *_: (0,) * nd, pipeline_mode=pl.Buffered(1))


def _rms(x, w):
    ms = jnp.mean(x * x, axis=-1, keepdims=True)
    return x * lax.rsqrt(ms + NORM_EPS) * w


def _silu(x):
    u = 0.5 * x
    return u * jnp.tanh(u) + u


def _pack_pairs(x):
    half = x.shape[1] // PACK
    bits = lax.bitcast_convert_type(x.astype(BF16).astype(F32), jnp.uint32)
    return [(bits[:, j * LANES:(j + 1) * LANES] >> 16)
            | (bits[:, half + j * LANES:half + (j + 1) * LANES] & jnp.uint32(HIGH_HALF))
            for j in range(half // LANES)]


def _unpack_pairs(words):
    lo = [lax.bitcast_convert_type(w << 16, F32) for w in words]
    hi = [lax.bitcast_convert_type(w & jnp.uint32(HIGH_HALF), F32) for w in words]
    return jnp.concatenate(lo + hi, axis=-1)


def _norm_matmul_kernel(x_ref, nw_ref, w_ref, *o_refs):
    xn = _rms(x_ref[...], nw_ref[...]).astype(BF16)
    lo = 0
    for o_ref in o_refs:
        n = o_ref.shape[1]
        for c0 in range(0, n, 512):
            cw = min(512, n - c0)
            acc = jnp.dot(xn, w_ref[:, lo + c0:lo + c0 + cw], preferred_element_type=F32)
            o_ref[:, c0:c0 + cw] = acc.astype(o_ref.dtype)
        lo += n


def _norm_matmul(x, norm_w, w, widths, out_dtypes, name):
    t, d = x.shape
    tm = min(ROW_TILE, t)
    assert sum(widths) == w.shape[1]
    return pl.pallas_call(
        _norm_matmul_kernel, grid=(t // tm,),
        in_specs=[pl.BlockSpec((tm, d), lambda i: (i, 0)), _resident((1, d)), _resident(w.shape)],
        out_specs=[pl.BlockSpec((tm, n), lambda i: (i, 0)) for n in widths],
        out_shape=[jax.ShapeDtypeStruct((t, n), dt) for n, dt in zip(widths, out_dtypes)],
        compiler_params=_params(), name=name,
    )(x, norm_w.reshape(1, d), w)


def _mix_out_route_kernel(a1_ref, a2_ref, w1_ref, w2_ref, r_ref, nw_ref, wr_ref, br_ref,
                          o_ref, xt_ref, route_ref, cnt_ref):
    acc = jnp.dot(a1_ref[...], w1_ref[...], preferred_element_type=F32)
    acc = acc + jnp.dot(a2_ref[...], w2_ref[...], preferred_element_type=F32)
    h = r_ref[...] + acc
    o_ref[...] = h
    _route_tokens(h, nw_ref, wr_ref, br_ref, xt_ref, route_ref, cnt_ref)


def _mix_out_route(a1, a2, w1, w2, res, norm_w, rg_w, rg_b, re_w, re_b, name):
    t, d = res.shape
    tm = min(ROW_TILE, t)
    wr, br = _router_weights(rg_w, rg_b, re_w, re_b)
    rows = lambda w: pl.BlockSpec((tm, w), lambda i: (i, 0))
    return pl.pallas_call(
        _mix_out_route_kernel, grid=(t // tm,),
        in_specs=[rows(a1.shape[1]), rows(a2.shape[1]), _resident(w1.shape), _resident(w2.shape), rows(d),
                  _resident((1, d)), _resident(wr.shape), _resident(br.shape)],
        out_specs=[rows(d),
                   pl.BlockSpec((tm * PACK_ROWS, LANES), lambda i: (i, 0)),
                   pl.BlockSpec((SUBLANES, tm), lambda i: (0, i)),
                   pl.BlockSpec((N_EXPERTS, LANES), lambda i: (i, 0))],
        out_shape=[jax.ShapeDtypeStruct((t, d), F32),
                   jax.ShapeDtypeStruct((t * PACK_ROWS, LANES), jnp.uint32),
                   jax.ShapeDtypeStruct((SUBLANES, t), F32),
                   jax.ShapeDtypeStruct((t // tm * N_EXPERTS, LANES), F32)],
        compiler_params=_params(), name=name,
    )(a1, a2, w1, w2, res, norm_w.reshape(1, d), wr, br)


def _mem_attn_kernel(q_ref, kv_ref, o_ref):
    scale = MEM_HEAD_DIM ** -0.5
    cols = lambda h: slice(h * MEM_HEAD_DIM, (h + 1) * MEM_HEAD_DIM)
    scores = [lax.dot_general(q_ref[:, cols(h)], kv_ref[:, cols(h)], (((1,), (1,)), ((), ())),
                              preferred_element_type=F32) * scale for h in range(MEM_HEADS)]
    probs = [jnp.exp(s - jnp.max(s, axis=-1, keepdims=True)) for s in scores]
    scales = [1.0 / jnp.sum(p, axis=-1, keepdims=True) for p in probs]
    outs = [jnp.dot(p.astype(BF16), kv_ref[:, MEM_WIDTH + h * MEM_HEAD_DIM:MEM_WIDTH + (h + 1) * MEM_HEAD_DIM],
                    preferred_element_type=F32) * r for h, (p, r) in enumerate(zip(probs, scales))]
    o_ref[...] = jnp.concatenate(outs, axis=-1).astype(o_ref.dtype)


def _mem_attn(q, kv, bsz, seq, mem_len, name):
    tq = min(ROW_TILE, seq)
    nq = seq // tq
    return pl.pallas_call(
        _mem_attn_kernel, grid=(bsz, nq),
        in_specs=[pl.BlockSpec((tq, MEM_WIDTH), lambda b, i: (b * nq + i, 0)),
                  pl.BlockSpec((mem_len, 2 * MEM_WIDTH), lambda b, i: (b, 0))],
        out_specs=pl.BlockSpec((tq, MEM_WIDTH), lambda b, i: (b * nq + i, 0)),
        out_shape=jax.ShapeDtypeStruct((bsz * seq, MEM_WIDTH), BF16),
        compiler_params=_params(), name=name,
    )(q, kv)


def _ssd_kernel(xbc_ref, z_ref, dt_ref, cw_ref, cb_ref, dtb_ref, alog_ref, dsk_ref, gnw_ref,
                expand_ref, o_ref, ext_ref, st_ref):
    c = pl.program_id(1)
    L = SSD_CHUNK
    gw = SSD_INNER // SSD_GROUPS
    hpg = SSD_HEADS // SSD_GROUPS

    @pl.when(c == 0)
    def _():
        ext_ref[0:SUBLANES, :] = jnp.zeros((SUBLANES, ext_ref.shape[1]), F32)
        st_ref[...] = jnp.zeros_like(st_ref)

    @pl.when(c > 0)
    def _():
        ext_ref[0:SUBLANES, :] = ext_ref[L:L + SUBLANES, :]

    ext_ref[SUBLANES:SUBLANES + L, :] = xbc_ref[...].astype(F32)
    conv = cb_ref[...]
    for k in range(CONV_WIDTH):
        conv = conv + cw_ref[k:k + 1, :] * ext_ref[pl.ds(SUBLANES - (CONV_WIDTH - 1) + k, L), :]
    xbc = _silu(conv)
    xs = xbc[:, :SSD_INNER]
    bm = xbc[:, SSD_INNER:SSD_INNER + SSD_GROUPS * SSD_STATE]
    cm = xbc[:, SSD_INNER + SSD_GROUPS * SSD_STATE:].astype(BF16)

    pre = dt_ref[...] + dtb_ref[...]
    dt = jnp.maximum(pre, 0.0) + jnp.log(1.0 + jnp.exp(-jnp.abs(pre)))
    a = -jnp.exp(alog_ref[...])
    adt = dt * a
    row = lax.broadcasted_iota(jnp.int32, (L, L), 0)
    col = lax.broadcasted_iota(jnp.int32, (L, L), 1)
    causal = row >= col
    acum = jnp.dot(causal.astype(F32), adt, preferred_element_type=F32,
                   precision=lax.Precision.HIGHEST)
    acum_t = acum.T
    ea = jnp.exp(acum)
    te = jnp.exp(acum[L - 1:L, :] - acum)
    widen = lambda cols: jnp.dot(cols.astype(BF16), expand_ref[...], preferred_element_type=F32)
    dtx, eax, tex = widen(dt), widen(ea), widen(te)
    xdt = xs * dtx
    xdt_b = xdt.astype(BF16)
    xw = (xdt * tex).astype(BF16)

    first_half = col < SSD_HEAD_DIM
    ys = []
    for g in range(SSD_GROUPS):
        cg = cm[:, g * SSD_STATE:(g + 1) * SSD_STATE]
        bg_t = bm[:, g * SSD_STATE:(g + 1) * SSD_STATE].T.astype(BF16)
        cb = jnp.dot(cg, bg_t, preferred_element_type=F32)
        state = st_ref[g]
        y_off = jnp.dot(cg, state.astype(BF16), preferred_element_type=F32) * eax[:, g * gw:(g + 1) * gw]
        parts = []
        for r in range(0, hpg, 2):
            both = []
            for h in (g * hpg + r, g * hpg + r + 1):
                seg = acum[:, h:h + 1] - acum_t[h:h + 1, :]
                m = (cb * jnp.exp(jnp.where(causal, seg, -jnp.inf))).astype(BF16)
                lo = (g * hpg + r) * SSD_HEAD_DIM
                both.append(jnp.dot(m, xdt_b[:, lo:lo + LANES], preferred_element_type=F32))
            parts.append(jnp.where(first_half, both[0], both[1]))
        ys.append(jnp.concatenate(parts, axis=-1) + y_off)
        upd = jnp.dot(bg_t, xw[:, g * gw:(g + 1) * gw], preferred_element_type=F32)
        st_ref[g] = state * eax[L - 1:L, g * gw:(g + 1) * gw] + upd
    y = jnp.concatenate(ys, axis=-1) + dsk_ref[...] * xs
    zf = z_ref[...].astype(F32)
    y = y * _silu(zf)
    normed = []
    for g in range(SSD_GROUPS):
        yg = y[:, g * gw:(g + 1) * gw]
        normed.append(yg * lax.rsqrt(jnp.mean(yg * yg, axis=-1, keepdims=True) + NORM_EPS))
    o_ref[...] = (jnp.concatenate(normed, axis=-1) * gnw_ref[...]).astype(o_ref.dtype)


def _ssd(xbc, z, dt, conv_w, conv_b, dt_bias, a_log, d_skip, gnorm_w, bsz, seq):
    nc = seq // SSD_CHUNK
    conv_ch = xbc.shape[1]
    pad = LANES - SSD_HEADS
    dtb = jnp.pad(dt_bias, (0, pad)).reshape(1, LANES)
    alog = jnp.pad(a_log, (0, pad)).reshape(1, LANES)
    dsk = jnp.repeat(d_skip, SSD_HEAD_DIM).reshape(1, SSD_INNER)
    expand = (jnp.arange(LANES)[:, None] == (jnp.arange(SSD_INNER) // SSD_HEAD_DIM)[None, :]).astype(BF16)
    blk = lambda w: pl.BlockSpec((SSD_CHUNK, w), lambda b, c: (b * nc + c, 0))
    return pl.pallas_call(
        _ssd_kernel, grid=(bsz, nc),
        in_specs=[blk(conv_ch), blk(SSD_INNER), blk(LANES),
                  _resident((CONV_WIDTH, conv_ch)), _resident((1, conv_ch)),
                  _resident((1, LANES)), _resident((1, LANES)),
                  _resident((1, SSD_INNER)), _resident((1, SSD_INNER)),
                  _resident((LANES, SSD_INNER))],
        out_specs=blk(SSD_INNER),
        out_shape=jax.ShapeDtypeStruct((bsz * seq, SSD_INNER), BF16),
        scratch_shapes=[pltpu.VMEM((SSD_CHUNK + 2 * SUBLANES, conv_ch), F32),
                        pltpu.VMEM((SSD_GROUPS, SSD_STATE, SSD_INNER // SSD_GROUPS), F32)],
        compiler_params=_params(), name="ssd_mixer",
    )(xbc, z, dt, conv_w, conv_b.reshape(1, conv_ch), dtb, alog, dsk,
      gnorm_w.reshape(1, SSD_INNER), expand)


def _rope(x, cos, sin_up, sin_dn):
    w = x.shape[1]
    reps = w // LANES
    tile = lambda t: jnp.concatenate([t] * reps, axis=-1) if reps > 1 else t
    half = ROT_DIM // 2
    return (x * tile(cos) + pltpu.roll(x, half, axis=1) * tile(sin_up)
            + pltpu.roll(x, w - half, axis=1) * tile(sin_dn))


def _swa_kernel(sink_ref, q_ref, kvp_ref, kvc_ref, tc_ref, bias_ref, o_ref):
    kvw = KV_HEADS * HEAD_DIM
    g = Q_HEADS // KV_HEADS
    scale = HEAD_DIM ** -0.5 * LOG2_E
    q = _rope(q_ref[...].astype(F32), tc_ref[0], tc_ref[1], tc_ref[2]) * scale
    q_t = q.T.astype(BF16)
    k = jnp.concatenate([kvp_ref[:, :kvw], kvc_ref[:, :kvw]], axis=0)
    v = jnp.concatenate([kvp_ref[:, kvw:], kvc_ref[:, kvw:]], axis=0).astype(F32)
    v_t = v.T.astype(BF16)
    bias = bias_ref[0]
    heads = lambda kh: range(kh * g, (kh + 1) * g)
    scores = [jnp.concatenate(
        [jnp.dot(k[:, kh * HEAD_DIM:(kh + 1) * HEAD_DIM], q_t[h * HEAD_DIM:(h + 1) * HEAD_DIM, :],
                 preferred_element_type=F32) for h in heads(kh)], axis=-1) + bias
        for kh in range(KV_HEADS)]
    sinks = [jnp.concatenate([jnp.full((1, ATTN_BLOCK), sink_ref[h] * LOG2_E, F32) for h in heads(kh)], axis=-1)
             for kh in range(KV_HEADS)]
    maxes = [jnp.maximum(jnp.max(sc, axis=0, keepdims=True), sk) for sc, sk in zip(scores, sinks)]
    probs = [jnp.exp2(sc - mx) for sc, mx in zip(scores, maxes)]
    denoms = [jnp.sum(p, axis=0, keepdims=True) + jnp.exp2(sk - mx) for p, sk, mx in zip(probs, sinks, maxes)]
    outs = []
    for kh in range(KV_HEADS):
        o_t = jnp.dot(v_t[kh * HEAD_DIM:(kh + 1) * HEAD_DIM, :], probs[kh].astype(BF16),
                      preferred_element_type=F32) / denoms[kh]
        outs += [o_t[:, j * ATTN_BLOCK:(j + 1) * ATTN_BLOCK] for j in range(g)]
    o_ref[...] = jnp.concatenate(outs, axis=0).T.astype(o_ref.dtype)


def _swa_bias(g):
    blk = ATTN_BLOCK
    s_idx = jnp.arange(2 * blk)[:, None]
    t_idx = jnp.arange(blk)[None, :]
    band = (s_idx > t_idx) & (s_idx <= t_idx + blk)
    first = band & (s_idx >= blk)
    both = jnp.stack([first, band], axis=0)
    return jnp.tile(jnp.where(both, 0.0, -jnp.inf).astype(F32), (1, 1, g))


def _rope_tables(seq):
    half = ROT_DIM // 2
    inv_freq = ROPE_THETA ** (-2.0 * jnp.arange(half, dtype=F32) / ROT_DIM)
    ang = jnp.arange(seq, dtype=jnp.int32).astype(F32)[:, None] * inv_freq[None, :]
    cos, sin = jnp.cos(ang), jnp.sin(ang)
    ones = jnp.ones((seq, HEAD_DIM - ROT_DIM), F32)
    zeros = jnp.zeros((seq, HEAD_DIM - ROT_DIM), F32)
    zh = jnp.zeros((seq, half), F32)
    cos_h = jnp.concatenate([cos, cos, ones], axis=-1)
    up_h = jnp.concatenate([zh, sin, zeros], axis=-1)
    dn_h = jnp.concatenate([-sin, zh, zeros], axis=-1)
    two = lambda t: jnp.concatenate([t, t], axis=-1)
    return jnp.stack([two(cos_h), two(up_h), two(dn_h)], axis=0)


def _shared_kv_kernel(x_ref, nw_ref, w_ref, t_ref, o_ref):
    kvw = KV_HEADS * HEAD_DIM
    xn = _rms(x_ref[...], nw_ref[...]).astype(BF16)
    kv = jnp.dot(xn, w_ref[...], preferred_element_type=F32)
    k = _rope(kv[:, :kvw], t_ref[0], t_ref[1], t_ref[2])
    o_ref[...] = jnp.concatenate([k, kv[:, kvw:]], axis=-1).astype(o_ref.dtype)


def _shared_kv(h, norm_w, w, tables, seq):
    t, d = h.shape
    tm = min(ROW_TILE, seq)
    per_seq = seq // tm
    return pl.pallas_call(
        _shared_kv_kernel, grid=(t // tm,),
        in_specs=[pl.BlockSpec((tm, d), lambda i: (i, 0)), _resident((1, d)), _resident(w.shape),
                  pl.BlockSpec((3, tm, LANES), lambda i: (0, i % per_seq, 0))],
        out_specs=pl.BlockSpec((tm, w.shape[1]), lambda i: (i, 0)),
        out_shape=jax.ShapeDtypeStruct((t, w.shape[1]), BF16),
        compiler_params=_params(), name="shared_kv",
    )(h, norm_w.reshape(1, d), w, tables)


def _swa(q, kv, sinks, tables, bsz, seq):
    nb = seq // ATTN_BLOCK
    qw = Q_HEADS * HEAD_DIM
    kvw2 = 2 * KV_HEADS * HEAD_DIM
    prev = lambda i: jnp.maximum(i - 1, 0)
    g = Q_HEADS // KV_HEADS
    return pl.pallas_call(
        _swa_kernel, grid=(bsz, nb),
        in_specs=[pl.BlockSpec(memory_space=pltpu.SMEM),
                  pl.BlockSpec((ATTN_BLOCK, qw), lambda b, i: (b * nb + i, 0)),
                  pl.BlockSpec((ATTN_BLOCK, kvw2), lambda b, i: (b * nb + prev(i), 0)),
                  pl.BlockSpec((ATTN_BLOCK, kvw2), lambda b, i: (b * nb + i, 0)),
                  pl.BlockSpec((3, ATTN_BLOCK, LANES), lambda b, i: (0, i, 0)),
                  pl.BlockSpec((1, 2 * ATTN_BLOCK, g * ATTN_BLOCK), lambda b, i: (jnp.minimum(i, 1), 0, 0))],
        out_specs=pl.BlockSpec((ATTN_BLOCK, qw), lambda b, i: (b * nb + i, 0)),
        out_shape=jax.ShapeDtypeStruct((bsz * seq, qw), BF16),
        compiler_params=_params(), name="swa_attention",
    )(sinks, q, kv, kv, tables, _swa_bias(g))


def _route_tokens(h, nw_ref, wr_ref, br_ref, xt_ref, route_ref, cnt_ref):
    tm = h.shape[0]
    xn = _rms(h, nw_ref[...])
    x_hi = xn.astype(BF16)
    for j, word in enumerate(_pack_pairs(xn)):
        xt_ref[pl.ds(j, tm, stride=PACK_ROWS), :] = word
    x_lo = (xn - x_hi.astype(F32)).astype(BF16)
    both = jnp.dot(x_hi, wr_ref[...], preferred_element_type=F32)
    logits = (both[:, :LANES] + both[:, LANES:]
              + jnp.dot(x_lo, wr_ref[:, :LANES], preferred_element_type=F32) + br_ref[...])
    lt = logits.T
    neg = -jnp.inf
    grow = lax.broadcasted_iota(jnp.int32, (SUBLANES, tm), 0)
    gl = jnp.where(grow < N_GROUPS, lt[:SUBLANES], neg)
    gmax = jnp.max(gl, axis=0, keepdims=True)
    g_w = 1.0 / jnp.sum(jnp.exp(gl - gmax), axis=0, keepdims=True)
    g_idx = jnp.min(jnp.where(gl == gmax, grow, SUBLANES), axis=0, keepdims=True)
    erow = lax.broadcasted_iota(jnp.int32, (N_EXPERTS, tm), 0)
    el = jnp.where(erow // EXPERTS_PER_GROUP == g_idx, lt[SUBLANES:SUBLANES + N_EXPERTS], neg)
    m1 = jnp.max(el, axis=0, keepdims=True)
    i1 = jnp.min(jnp.where(el == m1, erow, N_EXPERTS), axis=0, keepdims=True)
    el2 = jnp.where(erow == i1, neg, el)
    m2 = jnp.max(el2, axis=0, keepdims=True)
    i2 = jnp.min(jnp.where(el2 == m2, erow, N_EXPERTS), axis=0, keepdims=True)
    d = jnp.exp(m2 - m1)
    w1 = g_w / (1.0 + d)
    w2 = g_w * d / (1.0 + d)
    zero = jnp.zeros((SUBLANES - 4, tm), F32)
    route_ref[...] = jnp.concatenate([i1.astype(F32), i2.astype(F32), w1, w2, zero], axis=0)
    chosen = ((erow == i1) | (erow == i2)).astype(F32)
    cnt_ref[...] = jnp.broadcast_to(jnp.sum(chosen, axis=1, keepdims=True), cnt_ref.shape)


def _router_weights(rg_w, rg_b, re_w, re_b):
    d = rg_w.shape[0]
    gap = jnp.zeros((d, SUBLANES - N_GROUPS), F32)
    tail = jnp.zeros((d, LANES - SUBLANES - N_EXPERTS), F32)
    wr = jnp.concatenate([rg_w, gap, re_w, tail], axis=1)
    w_hi = wr.astype(BF16)
    w_lo = (wr - w_hi.astype(F32)).astype(BF16)
    br = jnp.concatenate([rg_b, gap[0], re_b, tail[0]]).reshape(1, LANES)
    return jnp.concatenate([w_hi, w_lo], axis=1), br


def _moe_kernel(st_ref, cnt_ref, tok_ref, dst_ref, xs_ref, wgu_ref, wd_ref, o_ref,
                tile_ref, xb_ref, ybuf_ref, state_ref):
    b = pl.program_id(0)
    e = pl.program_id(1)
    stride = MOE_STRIDE
    unroll = SUBLANES
    READY, PEND_BASE, PEND_ROWS = 0, 1, 2

    @pl.when((b == 0) & (e == 0))
    def _():
        ybuf_ref[...] = jnp.zeros_like(ybuf_ref)

    @pl.when(e == 0)
    def _():
        spare = o_ref.shape[0] - PACK * PACK_ROWS
        o_ref[spare:, :] = jnp.zeros((PACK * PACK_ROWS, LANES), jnp.uint32)
        state_ref[READY] = 0
        state_ref[PEND_BASE] = 0
        state_ref[PEND_ROWS] = 0

    start = st_ref[b * N_EXPERTS + e]
    count = cnt_ref[b * N_EXPERTS + e]

    def gather_row(base, r):
        tok = tok_ref[0, 0, base + r]
        src = pl.ds(pl.multiple_of(tok * PACK_ROWS, PACK_ROWS), PACK_ROWS)
        tile_ref[pl.ds(r, PACK_ROWS, stride=stride), :] = xs_ref[src, :]

    def scatter_row(base, r):
        dst = dst_ref[0, 0, base + r]
        o_ref[pl.ds(pl.multiple_of(dst * PACK_ROWS, PACK_ROWS), PACK_ROWS), :] = (
            ybuf_ref[pl.ds(r, PACK_ROWS, stride=stride), :])

    def copy_rows(row_fn, base, lo, hi):
        def group(i, c):
            for u in range(unroll):
                row_fn(base, i * unroll + u)
            return c
        lax.fori_loop(lo // unroll, hi // unroll, group, 0)

    def expert_mlp(rows, next_base, pend_base):
        def run():
            words = [tile_ref[pl.ds(j * stride, rows), :] for j in range(PACK_ROWS)]
            xb_ref[0:rows, :] = _unpack_pairs(words).astype(BF16)
            for r in range(rows):
                gather_row(next_base, r)
                scatter_row(pend_base, r)
            gu = jnp.dot(xb_ref[0:rows, :], wgu_ref[...], preferred_element_type=F32)
            gate, up = gu[:, :D_EXPERT], gu[:, D_EXPERT:]
            he = (_silu(gate) * up).astype(BF16)
            y = jnp.dot(he, wd_ref[...], preferred_element_type=F32)
            for j, word in enumerate(_pack_pairs(y)):
                ybuf_ref[pl.ds(j * stride, rows), :] = word
        return run

    def tile_body(t, carry):
        base = start + t * MOE_ROWS
        left = jnp.minimum(count - t * MOE_ROWS, MOE_ROWS)
        rows = (left + MOE_UNIT - 1) // MOE_UNIT * MOE_UNIT
        copy_rows(gather_row, base, state_ref[READY], rows)
        pend_rows = state_ref[PEND_ROWS]
        pend_base = jnp.where(pend_rows > 0, state_ref[PEND_BASE], base)
        copy_rows(scatter_row, pend_base, jnp.minimum(rows, pend_rows), pend_rows)
        next_base = base + left
        lax.switch(rows // MOE_UNIT - 1, [expert_mlp(MOE_UNIT * (k + 1), next_base, pend_base)
                                          for k in range(MOE_CLASSES)])
        state_ref[READY] = rows
        state_ref[PEND_BASE] = base
        state_ref[PEND_ROWS] = rows
        return carry

    lax.fori_loop(0, (count + MOE_ROWS - 1) // MOE_ROWS, tile_body, 0)

    @pl.when(e == N_EXPERTS - 1)
    def _():
        copy_rows(scatter_row, state_ref[PEND_BASE], 0, state_ref[PEND_ROWS])


def _moe_plan(route, tile_counts, nb, m):
    pick = lambda r: route[r].reshape(nb, m)
    e_idx = jnp.concatenate([pick(0), pick(1)], axis=1).astype(jnp.int32)
    order = jnp.argsort(e_idx, axis=1, stable=True).astype(jnp.int32)
    counts = tile_counts[:, 0].reshape(nb, -1, N_EXPERTS).sum(axis=1).astype(jnp.int32)
    starts = jnp.cumsum(counts, axis=1) - counts
    tail = ((0, 0), (0, 2 * MOE_ROWS))
    tok_list = jnp.pad(order % m, tail)
    dst_list = jnp.pad(order, tail, constant_values=2 * m)
    as_list = lambda a: a.reshape(nb, 1, a.shape[-1])
    return starts.reshape(-1), counts.reshape(-1), as_list(tok_list), as_list(dst_list)


def _moe(xt, route, tile_counts, w_gu, w_down, layer, t, d):
    m = min(MOE_BLOCK, t)
    nb = t // m
    starts, counts, tok_list, dst_list = _moe_plan(route, tile_counts, nb, m)
    length = tok_list.shape[-1]
    slots = 2 * m + PACK
    grid_spec = pltpu.PrefetchScalarGridSpec(
        num_scalar_prefetch=2, grid=(nb, N_EXPERTS),
        in_specs=[pl.BlockSpec((1, 1, length), lambda b, e, *_: (b, 0, 0), memory_space=pltpu.SMEM),
                  pl.BlockSpec((1, 1, length), lambda b, e, *_: (b, 0, 0), memory_space=pltpu.SMEM),
                  pl.BlockSpec((m * PACK_ROWS, LANES), lambda b, e, *_: (b, 0), pipeline_mode=pl.Buffered(1)),
                  pl.BlockSpec((None, None, d, 2 * D_EXPERT), lambda b, e, *_: (layer, e, 0, 0)),
                  pl.BlockSpec((None, None, D_EXPERT, d), lambda b, e, *_: (layer, e, 0, 0))],
        out_specs=pl.BlockSpec((None, slots * PACK_ROWS, LANES), lambda b, e, *_: (b, 0, 0)),
        scratch_shapes=[pltpu.VMEM((PACK_ROWS * MOE_STRIDE, LANES), jnp.uint32),
                        pltpu.VMEM((MOE_ROWS, d), BF16),
                        pltpu.VMEM((PACK_ROWS * MOE_STRIDE, LANES), jnp.uint32),
                        pltpu.SMEM((3,), jnp.int32)])
    return pl.pallas_call(
        _moe_kernel, grid_spec=grid_spec,
        out_shape=jax.ShapeDtypeStruct((nb, slots * PACK_ROWS, LANES), jnp.uint32),
        compiler_params=_params(), name="moe_experts",
    )(starts, counts, tok_list, dst_list, xt, w_gu, w_down)


def _combine_kernel(h_ref, y0_ref, y1_ref, route_ref, nw_ref, o_ref, *, final_norm):
    tm = h_ref.shape[0]
    token_rows = lambda ref: _unpack_pairs([ref[pl.ds(j, tm, stride=PACK_ROWS), :] for j in range(PACK_ROWS)])
    gates = route_ref[...].T
    out = h_ref[...] + gates[:, 2:3] * token_rows(y0_ref) + gates[:, 3:4] * token_rows(y1_ref)
    if final_norm:
        out = _rms(out, nw_ref[...])
    o_ref[...] = out


def _combine(h, y, route, norm_w, final_norm, name):
    t, d = h.shape
    tm = min(ROW_TILE, t)
    m = min(MOE_BLOCK, t)
    per_block = m // tm

    def slot(s):
        return pl.BlockSpec((None, tm * PACK_ROWS, LANES),
                            lambda i: (i // per_block, s * per_block + i % per_block, 0))

    return pl.pallas_call(
        functools.partial(_combine_kernel, final_norm=final_norm), grid=(t // tm,),
        in_specs=[pl.BlockSpec((tm, d), lambda i: (i, 0)),
                  slot(0), slot(1),
                  pl.BlockSpec((SUBLANES, tm), lambda i: (0, i)),
                  _resident((1, d))],
        out_specs=pl.BlockSpec((tm, d), lambda i: (i, 0)),
        out_shape=jax.ShapeDtypeStruct((t, d), F32),
        compiler_params=_params(), name=name,
    )(h, y, y, route, norm_w.reshape(1, d))


def _combine_proj_kernel(h_ref, y0_ref, y1_ref, route_ref, kvn_ref, bn_ref, wkv_ref, wq_ref, wqm_ref,
                         t_ref, ho_ref, kv_ref, q_ref, qm_ref):
    tm = h_ref.shape[0]
    kvw = KV_HEADS * HEAD_DIM
    token_rows = lambda ref: _unpack_pairs([ref[pl.ds(j, tm, stride=PACK_ROWS), :] for j in range(PACK_ROWS)])
    gates = route_ref[...].T
    out = h_ref[...] + gates[:, 2:3] * token_rows(y0_ref) + gates[:, 3:4] * token_rows(y1_ref)
    ho_ref[...] = out
    unit = out * lax.rsqrt(jnp.mean(out * out, axis=-1, keepdims=True) + NORM_EPS)
    kv = jnp.dot((unit * kvn_ref[...]).astype(BF16), wkv_ref[...], preferred_element_type=F32)
    k = _rope(kv[:, :kvw], t_ref[0], t_ref[1], t_ref[2])
    kv_ref[...] = jnp.concatenate([k, kv[:, kvw:]], axis=-1).astype(kv_ref.dtype)
    xn = (unit * bn_ref[...]).astype(BF16)
    for c0 in range(0, wq_ref.shape[1], 512):
        q_ref[:, c0:c0 + 512] = jnp.dot(xn, wq_ref[:, c0:c0 + 512], preferred_element_type=F32).astype(q_ref.dtype)
    qm_ref[...] = jnp.dot(xn, wqm_ref[...], preferred_element_type=F32).astype(qm_ref.dtype)


def _combine_proj(h, y, route, kv_norm_w, b_norm_w, w_kv, w_q, w_qm, tables, seq):
    t, d = h.shape
    tm = min(ROW_TILE, seq)
    per_block = min(MOE_BLOCK, t) // tm
    per_seq = seq // tm
    slot = lambda s: pl.BlockSpec((None, tm * PACK_ROWS, LANES),
                                  lambda i: (i // per_block, s * per_block + i % per_block, 0))
    rows = lambda w: pl.BlockSpec((tm, w), lambda i: (i, 0))
    return pl.pallas_call(
        _combine_proj_kernel, grid=(t // tm,),
        in_specs=[rows(d), slot(0), slot(1),
                  pl.BlockSpec((SUBLANES, tm), lambda i: (0, i)),
                  _resident((1, d)), _resident((1, d)),
                  _resident(w_kv.shape), _resident(w_q.shape), _resident(w_qm.shape),
                  pl.BlockSpec((3, tm, LANES), lambda i: (0, i % per_seq, 0))],
        out_specs=[rows(d), rows(w_kv.shape[1]), rows(w_q.shape[1]), rows(w_qm.shape[1])],
        out_shape=[jax.ShapeDtypeStruct((t, d), F32),
                   jax.ShapeDtypeStruct((t, w_kv.shape[1]), BF16),
                   jax.ShapeDtypeStruct((t, w_q.shape[1]), BF16),
                   jax.ShapeDtypeStruct((t, w_qm.shape[1]), BF16)],
        compiler_params=_params(), name="moe_combine_swa_in_proj",
    )(h, y, y, route, kv_norm_w.reshape(1, d), b_norm_w.reshape(1, d), w_kv, w_q, w_qm, tables)


def kernel(x, mem, a_norm_w, a_in_w, a_conv_w, a_conv_b, a_dt_bias, a_a_log, a_d_skip, a_gnorm_w, a_out_w, kv_norm_w, w_kv, b_norm_w, b_in_w, b_sinks, b_out_w, mem_norm_w, mem_w_kv, ffn_norm_w, router_group_w, router_group_b, router_expert_w, router_expert_b, w_gate, w_up, w_down, final_norm_w):
    bsz, seq, d = x.shape
    mem_len = mem.shape[1]
    t = bsz * seq
    h = x.reshape(t, d)
    mem2 = mem.reshape(bsz * mem_len, d)
    conv_ch = a_conv_w.shape[-1]
    n_a = a_in_w.shape[0]
    n_layers = n_a + b_in_w.shape[0]
    qw = Q_HEADS * HEAD_DIM
    tables = _rope_tables(seq)
    kv_sh = q = q_mem = None
    w_gu = jnp.concatenate([w_gate, w_up], axis=-1).astype(BF16)
    w_dn = w_down.astype(BF16)
    for layer in range(n_layers):
        (mem_kv,) = _norm_matmul(mem2, mem_norm_w[layer], mem_w_kv[layer].astype(BF16), [2 * MEM_WIDTH], [BF16],
                                 name=f"mem_kv_{layer}")
        if layer < n_a:
            i = layer
            w_in = a_in_w[i]
            o_dt, o_q = SSD_INNER + conv_ch, SSD_INNER + conv_ch + SSD_HEADS
            w_all = jnp.concatenate(
                [w_in[:, :o_dt], w_in[:, o_q:], w_in[:, o_dt:o_q], jnp.zeros((d, LANES - SSD_HEADS), F32)],
                axis=1).astype(BF16)
            z, xbc, q_mem, dt = _norm_matmul(h, a_norm_w[i], w_all, [SSD_INNER, conv_ch, MEM_WIDTH, LANES],
                                             [BF16, BF16, BF16, F32], name=f"ssd_in_proj_{i}")
            y = _ssd(xbc, z, dt, a_conv_w[i], a_conv_b[i], a_dt_bias[i], a_a_log[i], a_d_skip[i],
                     a_gnorm_w[i], bsz, seq)
            m_att = _mem_attn(q_mem, mem_kv, bsz, seq, mem_len, name=f"mem_attn_{layer}")
            w_out = a_out_w[i].astype(BF16)
            h, xt, route, tile_counts = _mix_out_route(
                y, m_att, w_out[:SSD_INNER], w_out[SSD_INNER:], h, ffn_norm_w[layer],
                router_group_w[layer], router_group_b[layer], router_expert_w[layer], router_expert_b[layer],
                name=f"ssd_out_proj_route_{i}")
        else:
            j = layer - n_a
            if q is None:
                if j == 0:
                    kv_sh = _shared_kv(h, kv_norm_w, w_kv.astype(BF16), tables, seq)
                w_in = b_in_w[j].astype(BF16)
                q, q_mem = _norm_matmul(h, b_norm_w[j], w_in, [qw, MEM_WIDTH], [BF16, BF16],
                                        name=f"swa_in_proj_{j}")
            att = _swa(q, kv_sh, b_sinks[j], tables, bsz, seq)
            m_att = _mem_attn(q_mem, mem_kv, bsz, seq, mem_len, name=f"mem_attn_{layer}")
            w_out = b_out_w[j].astype(BF16)
            h, xt, route, tile_counts = _mix_out_route(
                att, m_att, w_out[:qw], w_out[qw:], h, ffn_norm_w[layer],
                router_group_w[layer], router_group_b[layer], router_expert_w[layer], router_expert_b[layer],
                name=f"swa_out_proj_route_{j}")
            q = None
        y_moe = _moe(xt, route, tile_counts, w_gu, w_dn, layer, t, d)
        if layer + 1 == n_a and layer + 1 < n_layers:
            w_in = b_in_w[0].astype(BF16)
            h, kv_sh, q, q_mem = _combine_proj(h, y_moe, route, kv_norm_w, b_norm_w[0], w_kv.astype(BF16),
                                               w_in[:, :qw], w_in[:, qw:], tables, seq)
        else:
            last = layer == n_layers - 1
            h = _combine(h, y_moe, route, final_norm_w, last, name=f"moe_combine_{layer}")
    return h.reshape(bsz, seq, d)
```

```python
import functools

import jax
import jax.numpy as jnp
from jax import lax
from jax.experimental import pallas as pl
from jax.experimental.pallas import tpu as pltpu

F32 = jnp.float32
BF16 = jnp.bfloat16
NORM_EPS = 1e-6
LOG2_E = 1.4426950408889634

LANES = 128
SUBLANES = 8
VMEM_LIMIT = 56 * 1024 * 1024

MEM_HEADS = 4
MEM_HEAD_DIM = 128
MEM_WIDTH = MEM_HEADS * MEM_HEAD_DIM
SSD_HEAD_DIM = 64
SSD_HEADS = 24
SSD_INNER = SSD_HEADS * SSD_HEAD_DIM
SSD_GROUPS = 4
SSD_STATE = 128
SSD_CHUNK = 128
CONV_WIDTH = 4
HEAD_DIM = 64
Q_HEADS = 16
KV_HEADS = 4
ATTN_BLOCK = 128
ROT_DIM = 16
ROPE_THETA = 500000.0
N_GROUPS = 4
EXPERTS_PER_GROUP = 8
N_EXPERTS = 32
D_EXPERT = 512

ROW_TILE = 512
MOE_BLOCK = 4096
MOE_UNIT = 32
MOE_CLASSES = 16
MOE_ROWS = MOE_UNIT * MOE_CLASSES
MOE_STRIDE = MOE_ROWS + SUBLANES
PACK = 2
PACK_ROWS = SUBLANES // PACK
HIGH_HALF = 0xFFFF0000


def _params():
    return pltpu.CompilerParams(vmem_limit_bytes=VMEM_LIMIT)


def _resident(shape):
    nd = len(shape)
    return pl.BlockSpec(shape, lambda *_: (0,) * nd, pipeline_mode=pl.Buffered(1))


def _rms(x, w):
    ms = jnp.mean(x * x, axis=-1, keepdims=True)
    return x * lax.rsqrt(ms + NORM_EPS) * w


def _silu(x):
    u = 0.5 * x
    return u * jnp.tanh(u) + u


def _pack_pairs(x):
    half = x.shape[1] // PACK
    bits = lax.bitcast_convert_type(x.astype(BF16).astype(F32), jnp.uint32)
    return [(bits[:, j * LANES:(j + 1) * LANES] >> 16)
            | (bits[:, half + j * LANES:half + (j + 1) * LANES] & jnp.uint32(HIGH_HALF))
            for j in range(half // LANES)]


def _unpack_pairs(words):
    lo = [lax.bitcast_convert_type(w << 16, F32) for w in words]
    hi = [lax.bitcast_convert_type(w & jnp.uint32(HIGH_HALF), F32) for w in words]
    return jnp.concatenate(lo + hi, axis=-1)


def _norm_matmul_kernel(x_ref, nw_ref, w_ref, *o_refs):
    xn = _rms(x_ref[...], nw_ref[...]).astype(BF16)
    lo = 0
    for o_ref in o_refs:
        n = o_ref.shape[1]
        for c0 in range(0, n, 512):
            cw = min(512, n - c0)
            acc = jnp.dot(xn, w_ref[:, lo + c0:lo + c0 + cw], preferred_element_type=F32)
            o_ref[:, c0:c0 + cw] = acc.astype(o_ref.dtype)
        lo += n


def _norm_matmul(x, norm_w, w, widths, out_dtypes, name):
    t, d = x.shape
    tm = min(ROW_TILE, t)
    assert sum(widths) == w.shape[1]
    return pl.pallas_call(
        _norm_matmul_kernel, grid=(t // tm,),
        in_specs=[pl.BlockSpec((tm, d), lambda i: (i, 0)), _resident((1, d)), _resident(w.shape)],
        out_specs=[pl.BlockSpec((tm, n), lambda i: (i, 0)) for n in widths],
        out_shape=[jax.ShapeDtypeStruct((t, n), dt) for n, dt in zip(widths, out_dtypes)],
        compiler_params=_params(), name=name,
    )(x, norm_w.reshape(1, d), w)


def _mix_out_route_kernel(a1_ref, a2_ref, w1_ref, w2_ref, r_ref, nw_ref, wr_ref, br_ref,
                          o_ref, xt_ref, route_ref, cnt_ref):
    acc = jnp.dot(a1_ref[...], w1_ref[...], preferred_element_type=F32)
    acc = acc + jnp.dot(a2_ref[...], w2_ref[...], preferred_element_type=F32)
    h = r_ref[...] + acc
    o_ref[...] = h
    _route_tokens(h, nw_ref, wr_ref, br_ref, xt_ref, route_ref, cnt_ref)


def _mix_out_route(a1, a2, w1, w2, res, norm_w, rg_w, rg_b, re_w, re_b, name):
    t, d = res.shape
    tm = min(ROW_TILE, t)
    wr, br = _router_weights(rg_w, rg_b, re_w, re_b)
    rows = lambda w: pl.BlockSpec((tm, w), lambda i: (i, 0))
    return pl.pallas_call(
        _mix_out_route_kernel, grid=(t // tm,),
        in_specs=[rows(a1.shape[1]), rows(a2.shape[1]), _resident(w1.shape), _resident(w2.shape), rows(d),
                  _resident((1, d)), _resident(wr.shape), _resident(br.shape)],
        out_specs=[rows(d),
                   pl.BlockSpec((tm * PACK_ROWS, LANES), lambda i: (i, 0)),
                   pl.BlockSpec((SUBLANES, tm), lambda i: (0, i)),
                   pl.BlockSpec((N_EXPERTS, LANES), lambda i: (i, 0))],
        out_shape=[jax.ShapeDtypeStruct((t, d), F32),
                   jax.ShapeDtypeStruct((t * PACK_ROWS, LANES), jnp.uint32),
                   jax.ShapeDtypeStruct((SUBLANES, t), F32),
                   jax.ShapeDtypeStruct((t // tm * N_EXPERTS, LANES), F32)],
        compiler_params=_params(), name=name,
    )(a1, a2, w1, w2, res, norm_w.reshape(1, d), wr, br)


def _mem_attn_kernel(q_ref, kv_ref, o_ref):
    scale = MEM_HEAD_DIM ** -0.5
    cols = lambda h: slice(h * MEM_HEAD_DIM, (h + 1) * MEM_HEAD_DIM)
    scores = [lax.dot_general(q_ref[:, cols(h)], kv_ref[:, cols(h)], (((1,), (1,)), ((), ())),
                              preferred_element_type=F32) * scale for h in range(MEM_HEADS)]
    probs = [jnp.exp(s - jnp.max(s, axis=-1, keepdims=True)) for s in scores]
    scales = [1.0 / jnp.sum(p, axis=-1, keepdims=True) for p in probs]
    outs = [jnp.dot(p.astype(BF16), kv_ref[:, MEM_WIDTH + h * MEM_HEAD_DIM:MEM_WIDTH + (h + 1) * MEM_HEAD_DIM],
                    preferred_element_type=F32) * r for h, (p, r) in enumerate(zip(probs, scales))]
    o_ref[...] = jnp.concatenate(outs, axis=-1).astype(o_ref.dtype)


def _mem_attn(q, kv, bsz, seq, mem_len, name):
    tq = min(ROW_TILE, seq)
    nq = seq // tq
    return pl.pallas_call(
        _mem_attn_kernel, grid=(bsz, nq),
        in_specs=[pl.BlockSpec((tq, MEM_WIDTH), lambda b, i: (b * nq + i, 0)),
                  pl.BlockSpec((mem_len, 2 * MEM_WIDTH), lambda b, i: (b, 0))],
        out_specs=pl.BlockSpec((tq, MEM_WIDTH), lambda b, i: (b * nq + i, 0)),
        out_shape=jax.ShapeDtypeStruct((bsz * seq, MEM_WIDTH), BF16),
        compiler_params=_params(), name=name,
    )(q, kv)


def _ssd_kernel(xbc_ref, z_ref, dt_ref, cw_ref, cb_ref, dtb_ref, alog_ref, dsk_ref, gnw_ref,
                expand_ref, o_ref, ext_ref, st_ref):
    c = pl.program_id(1)
    L = SSD_CHUNK
    gw = SSD_INNER // SSD_GROUPS
    hpg = SSD_HEADS // SSD_GROUPS

    @pl.when(c == 0)
    def _():
        ext_ref[0:SUBLANES, :] = jnp.zeros((SUBLANES, ext_ref.shape[1]), F32)
        st_ref[...] = jnp.zeros_like(st_ref)

    @pl.when(c > 0)
    def _():
        ext_ref[0:SUBLANES, :] = ext_ref[L:L + SUBLANES, :]

    ext_ref[SUBLANES:SUBLANES + L, :] = xbc_ref[...].astype(F32)
    conv = cb_ref[...]
    for k in range(CONV_WIDTH):
        conv = conv + cw_ref[k:k + 1, :] * ext_ref[pl.ds(SUBLANES - (CONV_WIDTH - 1) + k, L), :]
    xbc = _silu(conv)
    xs = xbc[:, :SSD_INNER]
    bm = xbc[:, SSD_INNER:SSD_INNER + SSD_GROUPS * SSD_STATE]
    cm = xbc[:, SSD_INNER + SSD_GROUPS * SSD_STATE:].astype(BF16)

    pre = dt_ref[...] + dtb_ref[...]
    dt = jnp.maximum(pre, 0.0) + jnp.log(1.0 + jnp.exp(-jnp.abs(pre)))
    a = -jnp.exp(alog_ref[...])
    adt = dt * a
    row = lax.broadcasted_iota(jnp.int32, (L, L), 0)
    col = lax.broadcasted_iota(jnp.int32, (L, L), 1)
    causal = row >= col
    acum = jnp.dot(causal.astype(F32), adt, preferred_element_type=F32,
                   precision=lax.Precision.HIGHEST)
    acum_t = acum.T
    ea = jnp.exp(acum)
    te = jnp.exp(acum[L - 1:L, :] - acum)
    widen = lambda cols: jnp.dot(cols.astype(BF16), expand_ref[...], preferred_element_type=F32)
    dtx, eax, tex = widen(dt), widen(ea), widen(te)
    xdt = xs * dtx
    xdt_b = xdt.astype(BF16)
    xw = (xdt * tex).astype(BF16)

    first_half = col < SSD_HEAD_DIM
    ys = []
    for g in range(SSD_GROUPS):
        cg = cm[:, g * SSD_STATE:(g + 1) * SSD_STATE]
        bg_t = bm[:, g * SSD_STATE:(g + 1) * SSD_STATE].T.astype(BF16)
        cb = jnp.dot(cg, bg_t, preferred_element_type=F32)
        state = st_ref[g]
        y_off = jnp.dot(cg, state.astype(BF16), preferred_element_type=F32) * eax[:, g * gw:(g + 1) * gw]
        parts = []
        for r in range(0, hpg, 2):
            both = []
            for h in (g * hpg + r, g * hpg + r + 1):
                seg = acum[:, h:h + 1] - acum_t[h:h + 1, :]
                m = (cb * jnp.exp(jnp.where(causal, seg, -jnp.inf))).astype(BF16)
                lo = (g * hpg + r) * SSD_HEAD_DIM
                both.append(jnp.dot(m, xdt_b[:, lo:lo + LANES], preferred_element_type=F32))
            parts.append(jnp.where(first_half, both[0], both[1]))
        ys.append(jnp.concatenate(parts, axis=-1) + y_off)
        upd = jnp.dot(bg_t, xw[:, g * gw:(g + 1) * gw], preferred_element_type=F32)
        st_ref[g] = state * eax[L - 1:L, g * gw:(g + 1) * gw] + upd
    y = jnp.concatenate(ys, axis=-1) + dsk_ref[...] * xs
    zf = z_ref[...].astype(F32)
    y = y * _silu(zf)
    normed = []
    for g in range(SSD_GROUPS):
        yg = y[:, g * gw:(g + 1) * gw]
        normed.append(yg * lax.rsqrt(jnp.mean(yg * yg, axis=-1, keepdims=True) + NORM_EPS))
    o_ref[...] = (jnp.concatenate(normed, axis=-1) * gnw_ref[...]).astype(o_ref.dtype)


def _ssd(xbc, z, dt, conv_w, conv_b, dt_bias, a_log, d_skip, gnorm_w, bsz, seq):
    nc = seq // SSD_CHUNK
    conv_ch = xbc.shape[1]
    pad = LANES - SSD_HEADS
    dtb = jnp.pad(dt_bias, (0, pad)).reshape(1, LANES)
    alog = jnp.pad(a_log, (0, pad)).reshape(1, LANES)
    dsk = jnp.repeat(d_skip, SSD_HEAD_DIM).reshape(1, SSD_INNER)
    expand = (jnp.arange(LANES)[:, None] == (jnp.arange(SSD_INNER) // SSD_HEAD_DIM)[None, :]).astype(BF16)
    blk = lambda w: pl.BlockSpec((SSD_CHUNK, w), lambda b, c: (b * nc + c, 0))
    return pl.pallas_call(
        _ssd_kernel, grid=(bsz, nc),
        in_specs=[blk(conv_ch), blk(SSD_INNER), blk(LANES),
                  _resident((CONV_WIDTH, conv_ch)), _resident((1, conv_ch)),
                  _resident((1, LANES)), _resident((1, LANES)),
                  _resident((1, SSD_INNER)), _resident((1, SSD_INNER)),
                  _resident((LANES, SSD_INNER))],
        out_specs=blk(SSD_INNER),
        out_shape=jax.ShapeDtypeStruct((bsz * seq, SSD_INNER), BF16),
        scratch_shapes=[pltpu.VMEM((SSD_CHUNK + 2 * SUBLANES, conv_ch), F32),
                        pltpu.VMEM((SSD_GROUPS, SSD_STATE, SSD_INNER // SSD_GROUPS), F32)],
        compiler_params=_params(), name="ssd_mixer",
    )(xbc, z, dt, conv_w, conv_b.reshape(1, conv_ch), dtb, alog, dsk,
      gnorm_w.reshape(1, SSD_INNER), expand)


def _rope(x, cos, sin_up, sin_dn):
    w = x.shape[1]
    reps = w // LANES
    tile = lambda t: jnp.concatenate([t] * reps, axis=-1) if reps > 1 else t
    half = ROT_DIM // 2
    return (x * tile(cos) + pltpu.roll(x, half, axis=1) * tile(sin_up)
            + pltpu.roll(x, w - half, axis=1) * tile(sin_dn))


def _swa_kernel(sink_ref, q_ref, kvp_ref, kvc_ref, tc_ref, bias_ref, o_ref):
    kvw = KV_HEADS * HEAD_DIM
    g = Q_HEADS // KV_HEADS
    scale = HEAD_DIM ** -0.5 * LOG2_E
    q = _rope(q_ref[...].astype(F32), tc_ref[0], tc_ref[1], tc_ref[2]) * scale
    q_t = q.T.astype(BF16)
    k = jnp.concatenate([kvp_ref[:, :kvw], kvc_ref[:, :kvw]], axis=0)
    v = jnp.concatenate([kvp_ref[:, kvw:], kvc_ref[:, kvw:]], axis=0).astype(F32)
    v_t = v.T.astype(BF16)
    bias = bias_ref[0]
    heads = lambda kh: range(kh * g, (kh + 1) * g)
    scores = [jnp.concatenate(
        [jnp.dot(k[:, kh * HEAD_DIM:(kh + 1) * HEAD_DIM], q_t[h * HEAD_DIM:(h + 1) * HEAD_DIM, :],
                 preferred_element_type=F32) for h in heads(kh)], axis=-1) + bias
        for kh in range(KV_HEADS)]
    sinks = [jnp.concatenate([jnp.full((1, ATTN_BLOCK), sink_ref[h] * LOG2_E, F32) for h in heads(kh)], axis=-1)
             for kh in range(KV_HEADS)]
    maxes = [jnp.maximum(jnp.max(sc, axis=0, keepdims=True), sk) for sc, sk in zip(scores, sinks)]
    probs = [jnp.exp2(sc - mx) for sc, mx in zip(scores, maxes)]
    denoms = [jnp.sum(p, axis=0, keepdims=True) + jnp.exp2(sk - mx) for p, sk, mx in zip(probs, sinks, maxes)]
    outs = []
    for kh in range(KV_HEADS):
        o_t = jnp.dot(v_t[kh * HEAD_DIM:(kh + 1) * HEAD_DIM, :], probs[kh].astype(BF16),
                      preferred_element_type=F32) / denoms[kh]
        outs += [o_t[:, j * ATTN_BLOCK:(j + 1) * ATTN_BLOCK] for j in range(g)]
    o_ref[...] = jnp.concatenate(outs, axis=0).T.astype(o_ref.dtype)


def _swa_bias(g):
    blk = ATTN_BLOCK
    s_idx = jnp.arange(2 * blk)[:, None]
    t_idx = jnp.arange(blk)[None, :]
    band = (s_idx > t_idx) & (s_idx <= t_idx + blk)
    first = band & (s_idx >= blk)
    both = jnp.stack([first, band], axis=0)
    return jnp.tile(jnp.where(both, 0.0, -jnp.inf).astype(F32), (1, 1, g))


def _rope_tables(seq):
    half = ROT_DIM // 2
    inv_freq = ROPE_THETA ** (-2.0 * jnp.arange(half, dtype=F32) / ROT_DIM)
    ang = jnp.arange(seq, dtype=jnp.int32).astype(F32)[:, None] * inv_freq[None, :]
    cos, sin = jnp.cos(ang), jnp.sin(ang)
    ones = jnp.ones((seq, HEAD_DIM - ROT_DIM), F32)
    zeros = jnp.zeros((seq, HEAD_DIM - ROT_DIM), F32)
    zh = jnp.zeros((seq, half), F32)
    cos_h = jnp.concatenate([cos, cos, ones], axis=-1)
    up_h = jnp.concatenate([zh, sin, zeros], axis=-1)
    dn_h = jnp.concatenate([-sin, zh, zeros], axis=-1)
    two = lambda t: jnp.concatenate([t, t], axis=-1)
    return jnp.stack([two(cos_h), two(up_h), two(dn_h)], axis=0)


def _shared_kv_kernel(x_ref, nw_ref, w_ref, t_ref, o_ref):
    kvw = KV_HEADS * HEAD_DIM
    xn = _rms(x_ref[...], nw_ref[...]).astype(BF16)
    kv = jnp.dot(xn, w_ref[...], preferred_element_type=F32)
    k = _rope(kv[:, :kvw], t_ref[0], t_ref[1], t_ref[2])
    o_ref[...] = jnp.concatenate([k, kv[:, kvw:]], axis=-1).astype(o_ref.dtype)


def _shared_kv(h, norm_w, w, tables, seq):
    t, d = h.shape
    tm = min(ROW_TILE, seq)
    per_seq = seq // tm
    return pl.pallas_call(
        _shared_kv_kernel, grid=(t // tm,),
        in_specs=[pl.BlockSpec((tm, d), lambda i: (i, 0)), _resident((1, d)), _resident(w.shape),
                  pl.BlockSpec((3, tm, LANES), lambda i: (0, i % per_seq, 0))],
        out_specs=pl.BlockSpec((tm, w.shape[1]), lambda i: (i, 0)),
        out_shape=jax.ShapeDtypeStruct((t, w.shape[1]), BF16),
        compiler_params=_params(), name="shared_kv",
    )(h, norm_w.reshape(1, d), w, tables)


def _swa(q, kv, sinks, tables, bsz, seq):
    nb = seq // ATTN_BLOCK
    qw = Q_HEADS * HEAD_DIM
    kvw2 = 2 * KV_HEADS * HEAD_DIM
    prev = lambda i: jnp.maximum(i - 1, 0)
    g = Q_HEADS // KV_HEADS
    return pl.pallas_call(
        _swa_kernel, grid=(bsz, nb),
        in_specs=[pl.BlockSpec(memory_space=pltpu.SMEM),
                  pl.BlockSpec((ATTN_BLOCK, qw), lambda b, i: (b * nb + i, 0)),
                  pl.BlockSpec((ATTN_BLOCK, kvw2), lambda b, i: (b * nb + prev(i), 0)),
                  pl.BlockSpec((ATTN_BLOCK, kvw2), lambda b, i: (b * nb + i, 0)),
                  pl.BlockSpec((3, ATTN_BLOCK, LANES), lambda b, i: (0, i, 0)),
                  pl.BlockSpec((1, 2 * ATTN_BLOCK, g * ATTN_BLOCK), lambda b, i: (jnp.minimum(i, 1), 0, 0))],
        out_specs=pl.BlockSpec((ATTN_BLOCK, qw), lambda b, i: (b * nb + i, 0)),
        out_shape=jax.ShapeDtypeStruct((bsz * seq, qw), BF16),
        compiler_params=_params(), name="swa_attention",
    )(sinks, q, kv, kv, tables, _swa_bias(g))


def _route_tokens(h, nw_ref, wr_ref, br_ref, xt_ref, route_ref, cnt_ref):
    tm = h.shape[0]
    xn = _rms(h, nw_ref[...])
    x_hi = xn.astype(BF16)
    for j, word in enumerate(_pack_pairs(xn)):
        xt_ref[pl.ds(j, tm, stride=PACK_ROWS), :] = word
    x_lo = (xn - x_hi.astype(F32)).astype(BF16)
    both = jnp.dot(x_hi, wr_ref[...], preferred_element_type=F32)
    logits = (both[:, :LANES] + both[:, LANES:]
              + jnp.dot(x_lo, wr_ref[:, :LANES], preferred_element_type=F32) + br_ref[...])
    lt = logits.T
    neg = -jnp.inf
    grow = lax.broadcasted_iota(jnp.int32, (SUBLANES, tm), 0)
    gl = jnp.where(grow < N_GROUPS, lt[:SUBLANES], neg)
    gmax = jnp.max(gl, axis=0, keepdims=True)
    g_w = 1.0 / jnp.sum(jnp.exp(gl - gmax), axis=0, keepdims=True)
    g_idx = jnp.min(jnp.where(gl == gmax, grow, SUBLANES), axis=0, keepdims=True)
    erow = lax.broadcasted_iota(jnp.int32, (N_EXPERTS, tm), 0)
    el = jnp.where(erow // EXPERTS_PER_GROUP == g_idx, lt[SUBLANES:SUBLANES + N_EXPERTS], neg)
    m1 = jnp.max(el, axis=0, keepdims=True)
    i1 = jnp.min(jnp.where(el == m1, erow, N_EXPERTS), axis=0, keepdims=True)
    el2 = jnp.where(erow == i1, neg, el)
    m2 = jnp.max(el2, axis=0, keepdims=True)
    i2 = jnp.min(jnp.where(el2 == m2, erow, N_EXPERTS), axis=0, keepdims=True)
    d = jnp.exp(m2 - m1)
    w1 = g_w / (1.0 + d)
    w2 = g_w * d / (1.0 + d)
    zero = jnp.zeros((SUBLANES - 4, tm), F32)
    route_ref[...] = jnp.concatenate([i1.astype(F32), i2.astype(F32), w1, w2, zero], axis=0)
    chosen = ((erow == i1) | (erow == i2)).astype(F32)
    cnt_ref[...] = jnp.broadcast_to(jnp.sum(chosen, axis=1, keepdims=True), cnt_ref.shape)


def _router_weights(rg_w, rg_b, re_w, re_b):
    d = rg_w.shape[0]
    gap = jnp.zeros((d, SUBLANES - N_GROUPS), F32)
    tail = jnp.zeros((d, LANES - SUBLANES - N_EXPERTS), F32)
    wr = jnp.concatenate([rg_w, gap, re_w, tail], axis=1)
    w_hi = wr.astype(BF16)
    w_lo = (wr - w_hi.astype(F32)).astype(BF16)
    br = jnp.concatenate([rg_b, gap[0], re_b, tail[0]]).reshape(1, LANES)
    return jnp.concatenate([w_hi, w_lo], axis=1), br


def _moe_kernel(st_ref, cnt_ref, tok_ref, dst_ref, xs_ref, wgu_ref, wd_ref, o_ref,
                tile_ref, xb_ref, ybuf_ref, state_ref):
    b = pl.program_id(0)
    e = pl.program_id(1)
    stride = MOE_STRIDE
    unroll = SUBLANES
    READY, PEND_BASE, PEND_ROWS = 0, 1, 2

    @pl.when((b == 0) & (e == 0))
    def _():
        ybuf_ref[...] = jnp.zeros_like(ybuf_ref)

    @pl.when(e == 0)
    def _():
        spare = o_ref.shape[0] - PACK * PACK_ROWS
        o_ref[spare:, :] = jnp.zeros((PACK * PACK_ROWS, LANES), jnp.uint32)
        state_ref[READY] = 0
        state_ref[PEND_BASE] = 0
        state_ref[PEND_ROWS] = 0

    start = st_ref[b * N_EXPERTS + e]
    count = cnt_ref[b * N_EXPERTS + e]

    def gather_row(base, r):
        tok = tok_ref[0, 0, base + r]
        src = pl.ds(pl.multiple_of(tok * PACK_ROWS, PACK_ROWS), PACK_ROWS)
        tile_ref[pl.ds(r, PACK_ROWS, stride=stride), :] = xs_ref[src, :]

    def scatter_row(base, r):
        dst = dst_ref[0, 0, base + r]
        o_ref[pl.ds(pl.multiple_of(dst * PACK_ROWS, PACK_ROWS), PACK_ROWS), :] = (
            ybuf_ref[pl.ds(r, PACK_ROWS, stride=stride), :])

    def copy_rows(row_fn, base, lo, hi):
        def group(i, c):
            for u in range(unroll):
                row_fn(base, i * unroll + u)
            return c
        lax.fori_loop(lo // unroll, hi // unroll, group, 0)

    def expert_mlp(rows, next_base, pend_base):
        def run():
            words = [tile_ref[pl.ds(j * stride, rows), :] for j in range(PACK_ROWS)]
            xb_ref[0:rows, :] = _unpack_pairs(words).astype(BF16)
            for r in range(rows):
                gather_row(next_base, r)
                scatter_row(pend_base, r)
            gu = jnp.dot(xb_ref[0:rows, :], wgu_ref[...], preferred_element_type=F32)
            gate, up = gu[:, :D_EXPERT], gu[:, D_EXPERT:]
            he = (_silu(gate) * up).astype(BF16)
            y = jnp.dot(he, wd_ref[...], preferred_element_type=F32)
            for j, word in enumerate(_pack_pairs(y)):
                ybuf_ref[pl.ds(j * stride, rows), :] = word
        return run

    def tile_body(t, carry):
        base = start + t * MOE_ROWS
        left = jnp.minimum(count - t * MOE_ROWS, MOE_ROWS)
        rows = (left + MOE_UNIT - 1) // MOE_UNIT * MOE_UNIT
        copy_rows(gather_row, base, state_ref[READY], rows)
        pend_rows = state_ref[PEND_ROWS]
        pend_base = jnp.where(pend_rows > 0, state_ref[PEND_BASE], base)
        copy_rows(scatter_row, pend_base, jnp.minimum(rows, pend_rows), pend_rows)
        next_base = base + left
        lax.switch(rows // MOE_UNIT - 1, [expert_mlp(MOE_UNIT * (k + 1), next_base, pend_base)
                                          for k in range(MOE_CLASSES)])
        state_ref[READY] = rows
        state_ref[PEND_BASE] = base
        state_ref[PEND_ROWS] = rows
        return carry

    lax.fori_loop(0, (count + MOE_ROWS - 1) // MOE_ROWS, tile_body, 0)

    @pl.when(e == N_EXPERTS - 1)
    def _():
        copy_rows(scatter_row, state_ref[PEND_BASE], 0, state_ref[PEND_ROWS])


def _moe_plan(route, tile_counts, nb, m):
    pick = lambda r: route[r].reshape(nb, m)
    e_idx = jnp.concatenate([pick(0), pick(1)], axis=1).astype(jnp.int32)
    order = jnp.argsort(e_idx, axis=1, stable=True).astype(jnp.int32)
    counts = tile_counts[:, 0].reshape(nb, -1, N_EXPERTS).sum(axis=1).astype(jnp.int32)
    starts = jnp.cumsum(counts, axis=1) - counts
    tail = ((0, 0), (0, 2 * MOE_ROWS))
    tok_list = jnp.pad(order % m, tail)
    dst_list = jnp.pad(order, tail, constant_values=2 * m)
    as_list = lambda a: a.reshape(nb, 1, a.shape[-1])
    return starts.reshape(-1), counts.reshape(-1), as_list(tok_list), as_list(dst_list)


def _moe(xt, route, tile_counts, w_gu, w_down, layer, t, d):
    m = min(MOE_BLOCK, t)
    nb = t // m
    starts, counts, tok_list, dst_list = _moe_plan(route, tile_counts, nb, m)
    length = tok_list.shape[-1]
    slots = 2 * m + PACK
    grid_spec = pltpu.PrefetchScalarGridSpec(
        num_scalar_prefetch=2, grid=(nb, N_EXPERTS),
        in_specs=[pl.BlockSpec((1, 1, length), lambda b, e, *_: (b, 0, 0), memory_space=pltpu.SMEM),
                  pl.BlockSpec((1, 1, length), lambda b, e, *_: (b, 0, 0), memory_space=pltpu.SMEM),
                  pl.BlockSpec((m * PACK_ROWS, LANES), lambda b, e, *_: (b, 0), pipeline_mode=pl.Buffered(1)),
                  pl.BlockSpec((None, None, d, 2 * D_EXPERT), lambda b, e, *_: (layer, e, 0, 0)),
                  pl.BlockSpec((None, None, D_EXPERT, d), lambda b, e, *_: (layer, e, 0, 0))],
        out_specs=pl.BlockSpec((None, slots * PACK_ROWS, LANES), lambda b, e, *_: (b, 0, 0)),
        scratch_shapes=[pltpu.VMEM((PACK_ROWS * MOE_STRIDE, LANES), jnp.uint32),
                        pltpu.VMEM((MOE_ROWS, d), BF16),
                        pltpu.VMEM((PACK_ROWS * MOE_STRIDE, LANES), jnp.uint32),
                        pltpu.SMEM((3,), jnp.int32)])
    return pl.pallas_call(
        _moe_kernel, grid_spec=grid_spec,
        out_shape=jax.ShapeDtypeStruct((nb, slots * PACK_ROWS, LANES), jnp.uint32),
        compiler_params=_params(), name="moe_experts",
    )(starts, counts, tok_list, dst_list, xt, w_gu, w_down)


def _combine_kernel(h_ref, y0_ref, y1_ref, route_ref, nw_ref, o_ref, *, final_norm):
    tm = h_ref.shape[0]
    token_rows = lambda ref: _unpack_pairs([ref[pl.ds(j, tm, stride=PACK_ROWS), :] for j in range(PACK_ROWS)])
    gates = route_ref[...].T
    out = h_ref[...] + gates[:, 2:3] * token_rows(y0_ref) + gates[:, 3:4] * token_rows(y1_ref)
    if final_norm:
        out = _rms(out, nw_ref[...])
    o_ref[...] = out


def _combine(h, y, route, norm_w, final_norm, name):
    t, d = h.shape
    tm = min(ROW_TILE, t)
    m = min(MOE_BLOCK, t)
    per_block = m // tm

    def slot(s):
        return pl.BlockSpec((None, tm * PACK_ROWS, LANES),
                            lambda i: (i // per_block, s * per_block + i % per_block, 0))

    return pl.pallas_call(
        functools.partial(_combine_kernel, final_norm=final_norm), grid=(t // tm,),
        in_specs=[pl.BlockSpec((tm, d), lambda i: (i, 0)),
                  slot(0), slot(1),
                  pl.BlockSpec((SUBLANES, tm), lambda i: (0, i)),
                  _resident((1, d))],
        out_specs=pl.BlockSpec((tm, d), lambda i: (i, 0)),
        out_shape=jax.ShapeDtypeStruct((t, d), F32),
        compiler_params=_params(), name=name,
    )(h, y, y, route, norm_w.reshape(1, d))


def _combine_proj_kernel(h_ref, y0_ref, y1_ref, route_ref, kvn_ref, bn_ref, wkv_ref, wq_ref, wqm_ref,
                         t_ref, ho_ref, kv_ref, q_ref, qm_ref):
    tm = h_ref.shape[0]
    kvw = KV_HEADS * HEAD_DIM
    token_rows = lambda ref: _unpack_pairs([ref[pl.ds(j, tm, stride=PACK_ROWS), :] for j in range(PACK_ROWS)])
    gates = route_ref[...].T
    out = h_ref[...] + gates[:, 2:3] * token_rows(y0_ref) + gates[:, 3:4] * token_rows(y1_ref)
    ho_ref[...] = out
    unit = out * lax.rsqrt(jnp.mean(out * out, axis=-1, keepdims=True) + NORM_EPS)
    kv = jnp.dot((unit * kvn_ref[...]).astype(BF16), wkv_ref[...], preferred_element_type=F32)
    k = _rope(kv[:, :kvw], t_ref[0], t_ref[1], t_ref[2])
    kv_ref[...] = jnp.concatenate([k, kv[:, kvw:]], axis=-1).astype(kv_ref.dtype)
    xn = (unit * bn_ref[...]).astype(BF16)
    for c0 in range(0, wq_ref.shape[1], 512):
        q_ref[:, c0:c0 + 512] = jnp.dot(xn, wq_ref[:, c0:c0 + 512], preferred_element_type=F32).astype(q_ref.dtype)
    qm_ref[...] = jnp.dot(xn, wqm_ref[...], preferred_element_type=F32).astype(qm_ref.dtype)


def _combine_proj(h, y, route, kv_norm_w, b_norm_w, w_kv, w_q, w_qm, tables, seq):
    t, d = h.shape
    tm = min(ROW_TILE, seq)
    per_block = min(MOE_BLOCK, t) // tm
    per_seq = seq // tm
    slot = lambda s: pl.BlockSpec((None, tm * PACK_ROWS, LANES),
                                  lambda i: (i // per_block, s * per_block + i % per_block, 0))
    rows = lambda w: pl.BlockSpec((tm, w), lambda i: (i, 0))
    return pl.pallas_call(
        _combine_proj_kernel, grid=(t // tm,),
        in_specs=[rows(d), slot(0), slot(1),
                  pl.BlockSpec((SUBLANES, tm), lambda i: (0, i)),
                  _resident((1, d)), _resident((1, d)),
                  _resident(w_kv.shape), _resident(w_q.shape), _resident(w_qm.shape),
                  pl.BlockSpec((3, tm, LANES), lambda i: (0, i % per_seq, 0))],
        out_specs=[rows(d), rows(w_kv.shape[1]), rows(w_q.shape[1]), rows(w_qm.shape[1])],
        out_shape=[jax.ShapeDtypeStruct((t, d), F32),
                   jax.ShapeDtypeStruct((t, w_kv.shape[1]), BF16),
                   jax.ShapeDtypeStruct((t, w_q.shape[1]), BF16),
                   jax.ShapeDtypeStruct((t, w_qm.shape[1]), BF16)],
        compiler_params=_params(), name="moe_combine_swa_in_proj",
    )(h, y, y, route, kv_norm_w.reshape(1, d), b_norm_w.reshape(1, d), w_kv, w_q, w_qm, tables)


def kernel(x, mem, a_norm_w, a_in_w, a_conv_w, a_conv_b, a_dt_bias, a_a_log, a_d_skip, a_gnorm_w, a_out_w, kv_norm_w, w_kv, b_norm_w, b_in_w, b_sinks, b_out_w, mem_norm_w, mem_w_kv, ffn_norm_w, router_group_w, router_group_b, router_expert_w, router_expert_b, w_gate, w_up, w_down, final_norm_w):
    bsz, seq, d = x.shape
    mem_len = mem.shape[1]
    t = bsz * seq
    h = x.reshape(t, d)
    mem2 = mem.reshape(bsz * mem_len, d)
    conv_ch = a_conv_w.shape[-1]
    n_a = a_in_w.shape[0]
    n_layers = n_a + b_in_w.shape[0]
    qw = Q_HEADS * HEAD_DIM
    tables = _rope_tables(seq)
    kv_sh = q = q_mem = None
    w_gu = jnp.concatenate([w_gate, w_up], axis=-1).astype(BF16)
    w_dn = w_down.astype(BF16)
    for layer in range(n_layers):
        (mem_kv,) = _norm_matmul(mem2, mem_norm_w[layer], mem_w_kv[layer].astype(BF16), [2 * MEM_WIDTH], [BF16],
                                 name=f"mem_kv_{layer}")
        if layer < n_a:
            i = layer
            w_in = a_in_w[i]
            o_dt, o_q = SSD_INNER + conv_ch, SSD_INNER + conv_ch + SSD_HEADS
            w_all = jnp.concatenate(
                [w_in[:, :o_dt], w_in[:, o_q:], w_in[:, o_dt:o_q], jnp.zeros((d, LANES - SSD_HEADS), F32)],
                axis=1).astype(BF16)
            z, xbc, q_mem, dt = _norm_matmul(h, a_norm_w[i], w_all, [SSD_INNER, conv_ch, MEM_WIDTH, LANES],
                                             [BF16, BF16, BF16, F32], name=f"ssd_in_proj_{i}")
            y = _ssd(xbc, z, dt, a_conv_w[i], a_conv_b[i], a_dt_bias[i], a_a_log[i], a_d_skip[i],
                     a_gnorm_w[i], bsz, seq)
            m_att = _mem_attn(q_mem, mem_kv, bsz, seq, mem_len, name=f"mem_attn_{layer}")
            w_out = a_out_w[i].astype(BF16)
            h, xt, route, tile_counts = _mix_out_route(
                y, m_att, w_out[:SSD_INNER], w_out[SSD_INNER:], h, ffn_norm_w[layer],
                router_group_w[layer], router_group_b[layer], router_expert_w[layer], router_expert_b[layer],
                name=f"ssd_out_proj_route_{i}")
        else:
            j = layer - n_a
            if q is None:
                if j == 0:
                    kv_sh = _shared_kv(h, kv_norm_w, w_kv.astype(BF16), tables, seq)
                w_in = b_in_w[j].astype(BF16)
                q, q_mem = _norm_matmul(h, b_norm_w[j], w_in, [qw, MEM_WIDTH], [BF16, BF16],
                                        name=f"swa_in_proj_{j}")
            att = _swa(q, kv_sh, b_sinks[j], tables, bsz, seq)
            m_att = _mem_attn(q_mem, mem_kv, bsz, seq, mem_len, name=f"mem_attn_{layer}")
            w_out = b_out_w[j].astype(BF16)
            h, xt, route, tile_counts = _mix_out_route(
                att, m_att, w_out[:qw], w_out[qw:], h, ffn_norm_w[layer],
                router_group_w[layer], router_group_b[layer], router_expert_w[layer], router_expert_b[layer],
                name=f"swa_out_proj_route_{j}")
            q = None
        y_moe = _moe(xt, route, tile_counts, w_gu, w_dn, layer, t, d)
        if layer + 1 == n_a and layer + 1 < n_layers:
            w_in = b_in_w[0].astype(BF16)
            h, kv_sh, q, q_mem = _combine_proj(h, y_moe, route, kv_norm_w, b_norm_w[0], w_kv.astype(BF16),
                                               w_in[:, :qw], w_in[:, qw:], tables, seq)
        else:
            last = layer == n_layers - 1
            h = _combine(h, y_moe, route, final_norm_w, last, name=f"moe_combine_{layer}")
    return h.reshape(bsz, seq, d)
```
